```python
import math
import jax, jax.numpy as jnp
from jax import lax
import numpy as np

D_MODEL = 4096
BATCH = 1
SEQ = 8192
DEPTH = 2

D_MIX = D_MODEL
SSD_INNER = D_MIX // 2
SSD_HEAD_DIM = 64
SSD_HEADS = SSD_INNER // SSD_HEAD_DIM
SSD_GROUPS = 4
SSD_STATE = 128
SSD_CONV = 4
SSD_CHUNK = 128
SSD_CONV_DIM = SSD_INNER + 2 * SSD_GROUPS * SSD_STATE
MLSTM_WIDTH = D_MIX - SSD_INNER
MLSTM_HEADS = 4
MLSTM_DV = MLSTM_WIDTH // MLSTM_HEADS
MLSTM_DQK = MLSTM_DV // 2
MLSTM_CHUNK = 64
IN_SPLITS = (SSD_INNER, SSD_CONV_DIM, SSD_HEADS,
             MLSTM_HEADS * MLSTM_DQK, MLSTM_HEADS * MLSTM_DQK,
             MLSTM_WIDTH, MLSTM_WIDTH, MLSTM_HEADS, MLSTM_HEADS)
D_IN_PROJ = (SSD_INNER + SSD_CONV_DIM + SSD_HEADS + 2 * MLSTM_HEADS * MLSTM_DQK
             + 2 * MLSTM_WIDTH + 2 * MLSTM_HEADS)
FF_DENSE = 14336
N_EXPERTS = 8
TOP_K = 2
FF_EXPERT = 5632
NORM_EPS = 1e-6

kernel_name = "hybrid_ssd_mlstm_moe_trunk"


def _rms(xf):
    return xf * lax.rsqrt(jnp.mean(xf * xf, axis=-1, keepdims=True) + NORM_EPS)


def rms_norm(x, w):
    y = _rms(x.astype(jnp.float32)) * w.astype(jnp.float32)
    return y.astype(x.dtype)


def causal_dwconv(u, w, b):
    s = u.shape[1]
    up = jnp.pad(u, ((0, 0), (SSD_CONV - 1, 0), (0, 0)))
    out = b
    for j in range(SSD_CONV):
        out = out + up[:, j:j + s, :] * w[j]
    return out


def ssd_chunked(xh, dt, a, bmat, cmat):
    bsz, s = xh.shape[:2]
    nc = s // SSD_CHUNK
    hpg = SSD_HEADS // SSD_GROUPS
    X = (xh * dt[..., None]).reshape(bsz, nc, SSD_CHUNK, SSD_GROUPS, hpg, SSD_HEAD_DIM)
    dA = (dt * a).reshape(bsz, nc, SSD_CHUNK, SSD_GROUPS, hpg)
    Bc = bmat.reshape(bsz, nc, SSD_CHUNK, SSD_GROUPS, SSD_STATE)
    Cc = cmat.reshape(bsz, nc, SSD_CHUNK, SSD_GROUPS, SSD_STATE)
    a_cs = jnp.cumsum(dA, axis=2)
    a_t = jnp.moveaxis(a_cs, 2, -1)
    causal = jnp.tril(jnp.ones((SSD_CHUNK, SSD_CHUNK), dtype=bool))
    seg = jnp.where(causal, a_t[..., :, None] - a_t[..., None, :], -jnp.inf)
    decay_in = jnp.exp(seg)
    cb = jnp.einsum('bclgn,bcsgn->bcgls', Cc, Bc)
    y_diag = jnp.einsum('bcgjls,bcsgjp->bclgjp', cb[:, :, :, None] * decay_in, X)
    decay_states = jnp.exp(a_cs[:, :, -1:] - a_cs)
    states = jnp.einsum('bclgn,bclgjp->bcgjpn', Bc, X * decay_states[..., None])
    chunk_decay = jnp.exp(a_cs[:, :, -1])

    def step(hstate, inp):
        dec, st = inp
        return dec[..., None, None] * hstate + st, hstate

    h0 = jnp.zeros_like(states[:, 0])
    _, h_enter = lax.scan(step, h0, (jnp.moveaxis(chunk_decay, 1, 0), jnp.moveaxis(states, 1, 0)))
    h_enter = jnp.moveaxis(h_enter, 0, 1)
    y_off = jnp.einsum('bclgn,bcgjpn->bclgjp', Cc, h_enter) * jnp.exp(a_cs)[..., None]
    return (y_diag + y_off).reshape(bsz, s, SSD_HEADS, SSD_HEAD_DIM)


def mlstm_chunkwise(q, k, v, logi, logf):
    bsz, nh, s, dqk = q.shape
    dv = v.shape[-1]
    nc = s // MLSTM_CHUNK

    def chunks(t):
        return jnp.moveaxis(t.reshape(bsz, nh, nc, MLSTM_CHUNK, *t.shape[3:]), 2, 0)

    causal = jnp.tril(jnp.ones((MLSTM_CHUNK, MLSTM_CHUNK), dtype=bool))

    def step(carry, inp):
        C, n, m = carry
        qc, kc, vc, li, lf = inp
        b = jnp.cumsum(lf, axis=-1)
        inter = b + m[..., None]
        dmat = jnp.where(causal, b[..., :, None] - b[..., None, :] + li[..., None, :], -jnp.inf)
        m_t = jnp.maximum(inter, jnp.max(dmat, axis=-1))
        w_intra = jnp.exp(dmat - m_t[..., None])
        w_inter = jnp.exp(inter - m_t)
        sqk = jnp.einsum('bhld,bhsd->bhls', qc, kc) * w_intra
        num = w_inter[..., None] * jnp.einsum('bhld,bhdv->bhlv', qc, C) + jnp.einsum('bhls,bhsv->bhlv', sqk, vc)
        den = w_inter * jnp.einsum('bhld,bhd->bhl', qc, n) + jnp.sum(sqk, axis=-1)
        hout = num / jnp.maximum(jnp.abs(den), jnp.exp(-m_t))[..., None]
        b_last = b[..., -1]
        g = b_last[..., None] - b + li
        m_new = jnp.maximum(b_last + m, jnp.max(g, axis=-1))
        decay = jnp.exp(b_last + m - m_new)
        kw = kc * jnp.exp(g - m_new[..., None])[..., None]
        C_new = decay[..., None, None] * C + jnp.einsum('bhsd,bhsv->bhdv', kw, vc)
        n_new = decay[..., None] * n + jnp.sum(kw, axis=2)
        return (C_new, n_new, m_new), hout

    init = (jnp.zeros((bsz, nh, dqk, dv), q.dtype), jnp.zeros((bsz, nh, dqk), q.dtype),
            jnp.zeros((bsz, nh), q.dtype))
    _, hs = lax.scan(step, init, (chunks(q), chunks(k), chunks(v), chunks(logi), chunks(logf)))
    return jnp.moveaxis(hs, 0, 2).reshape(bsz, nh, s, dv)


def hybrid_mixer(u, w_in, conv_w, conv_b, dt_bias, a_log, d_skip, ssd_norm_w,
                 i_bias, f_bias, mlstm_norm_w, w_out):
    f32 = jnp.float32
    bsz, s, _ = u.shape
    proj = u @ w_in
    idx = [int(c) for c in np.cumsum(IN_SPLITS)[:-1]]
    z, xbc, dt_raw, q, k, v, o, i_pre, f_pre = jnp.split(proj, idx, axis=-1)

    xbc = jax.nn.silu(causal_dwconv(xbc, conv_w, conv_b)).astype(f32)
    xs, bm, cm = jnp.split(xbc, [SSD_INNER, SSD_INNER + SSD_GROUPS * SSD_STATE], axis=-1)
    xh = xs.reshape(bsz, s, SSD_HEADS, SSD_HEAD_DIM)
    dt = jax.nn.softplus(dt_raw.astype(f32) + dt_bias.astype(f32))
    a = -jnp.exp(a_log.astype(f32))
    y = ssd_chunked(xh, dt, a, bm.reshape(bsz, s, SSD_GROUPS, SSD_STATE),
                    cm.reshape(bsz, s, SSD_GROUPS, SSD_STATE))
    y = y + xh * d_skip.astype(f32)[:, None]
    y = y.reshape(bsz, s, SSD_INNER) * jax.nn.silu(z.astype(f32))
    y_ssd = _rms(y.reshape(bsz, s, SSD_GROUPS, -1)).reshape(bsz, s, SSD_INNER) * ssd_norm_w.astype(f32)

    def heads(t, d):
        return t.reshape(bsz, s, MLSTM_HEADS, d).transpose(0, 2, 1, 3).astype(f32)

    qh = heads(q, MLSTM_DQK) * (MLSTM_DQK ** -0.5)
    kh = heads(k, MLSTM_DQK)
    vh = heads(v, MLSTM_DV)
    logi = (i_pre.astype(f32) + i_bias.astype(f32)).transpose(0, 2, 1)
    logf = jax.nn.log_sigmoid(f_pre.astype(f32) + f_bias.astype(f32)).transpose(0, 2, 1)
    hm = mlstm_chunkwise(qh, kh, vh, logi, logf)
    hm = _rms(hm).transpose(0, 2, 1, 3).reshape(bsz, s, MLSTM_WIDTH) * mlstm_norm_w.astype(f32)
    y_ml = jax.nn.sigmoid(o.astype(f32)) * hm

    mixed = jnp.concatenate([y_ssd, y_ml], axis=-1).astype(u.dtype)
    return mixed @ w_out


def swiglu(u, wg, wu, wd):
    return (jax.nn.silu(u @ wg) * (u @ wu)) @ wd


def moe_swiglu(u, router_w, wg, wu, wd):
    logits = (u @ router_w).astype(jnp.float32)
    probs = jax.nn.softmax(logits, axis=-1)
    top_p, top_i = lax.top_k(probs, TOP_K)
    top_p = top_p / jnp.sum(top_p, axis=-1, keepdims=True)
    combine = jnp.sum(jax.nn.one_hot(top_i, N_EXPERTS, dtype=jnp.float32) * top_p[..., None], axis=-2)
    combine = combine.astype(u.dtype)
    out = jnp.zeros_like(u)
    for e in range(N_EXPERTS):
        out = out + combine[..., e, None] * swiglu(u, wg[e], wu[e], wd[e])
    return out


def setup_inputs(seed: int = 0) -> dict:
    key = jax.random.key(seed)
    ks = jax.random.split(key, 24)
    f32 = jnp.float32
    n_dense = (DEPTH + 1) // 2
    n_moe = DEPTH // 2

    def nrm(k, shape, scale):
        return jax.random.normal(k, shape, f32) * scale

    dt0 = jnp.exp(jax.random.uniform(ks[5], (DEPTH, SSD_HEADS), f32, math.log(1e-3), math.log(1e-1)))
    return {
        "x": nrm(ks[0], (BATCH, SEQ, D_MODEL), 1.0),
        "norm_mix_w": 1.0 + nrm(ks[1], (DEPTH, D_MODEL), 0.02),
        "w_in": nrm(ks[2], (DEPTH, D_MODEL, D_IN_PROJ), D_MODEL ** -0.5),
        "conv_w": nrm(ks[3], (DEPTH, SSD_CONV, SSD_CONV_DIM), SSD_CONV ** -0.5),
        "conv_b": nrm(ks[4], (DEPTH, SSD_CONV_DIM), 0.02),
        "dt_bias": dt0 + jnp.log(-jnp.expm1(-dt0)),
        "a_log": jnp.log(jax.random.uniform(ks[6], (DEPTH, SSD_HEADS), f32, 1.0, 16.0)),
        "d_skip": 1.0 + nrm(ks[7], (DEPTH, SSD_HEADS), 0.1),
        "ssd_norm_w": 1.0 + nrm(ks[8], (DEPTH, SSD_INNER), 0.02),
        "mlstm_i_bias": -1.0 + nrm(ks[9], (DEPTH, MLSTM_HEADS), 0.1),
        "mlstm_f_bias": jnp.linspace(3.0, 6.0, MLSTM_HEADS, dtype=f32)[None] + nrm(ks[10], (DEPTH, MLSTM_HEADS), 0.1),
        "mlstm_norm_w": 1.0 + nrm(ks[11], (DEPTH, MLSTM_WIDTH), 0.02),
        "w_out": nrm(ks[12], (DEPTH, D_MIX, D_MODEL), D_MIX ** -0.5),
        "norm_ffn_w": 1.0 + nrm(ks[13], (DEPTH, D_MODEL), 0.02),
        "ffn_w_gate": nrm(ks[14], (n_dense, D_MODEL, FF_DENSE), D_MODEL ** -0.5),
        "ffn_w_up": nrm(ks[15], (n_dense, D_MODEL, FF_DENSE), D_MODEL ** -0.5),
        "ffn_w_down": nrm(ks[16], (n_dense, FF_DENSE, D_MODEL), FF_DENSE ** -0.5),
        "router_w": nrm(ks[17], (n_moe, D_MODEL, N_EXPERTS), D_MODEL ** -0.5),
        "moe_w_gate": nrm(ks[18], (n_moe, N_EXPERTS, D_MODEL, FF_EXPERT), D_MODEL ** -0.5),
        "moe_w_up": nrm(ks[19], (n_moe, N_EXPERTS, D_MODEL, FF_EXPERT), D_MODEL ** -0.5),
        "moe_w_down": nrm(ks[20], (n_moe, N_EXPERTS, FF_EXPERT, D_MODEL), FF_EXPERT ** -0.5),
        "final_norm_w": 1.0 + nrm(ks[21], (D_MODEL,), 0.02),
    }


def reference(x, norm_mix_w, w_in, conv_w, conv_b, dt_bias, a_log, d_skip, ssd_norm_w,
              mlstm_i_bias, mlstm_f_bias, mlstm_norm_w, w_out, norm_ffn_w,
              ffn_w_gate, ffn_w_up, ffn_w_down, router_w, moe_w_gate, moe_w_up, moe_w_down,
              final_norm_w):
    h = x
    for layer in range(DEPTH):
        u = rms_norm(h, norm_mix_w[layer])
        h = h + hybrid_mixer(u, w_in[layer], conv_w[layer], conv_b[layer], dt_bias[layer],
                             a_log[layer], d_skip[layer], ssd_norm_w[layer],
                             mlstm_i_bias[layer], mlstm_f_bias[layer], mlstm_norm_w[layer],
                             w_out[layer])
        u = rms_norm(h, norm_ffn_w[layer])
        j = layer // 2
        if layer % 2 == 0:
            h = h + swiglu(u, ffn_w_gate[j], ffn_w_up[j], ffn_w_down[j])
        else:
            h = h + moe_swiglu(u, router_w[j], moe_w_gate[j], moe_w_up[j], moe_w_down[j])
    return rms_norm(h, final_norm_w)
```

```python
import functools

import jax
import jax.numpy as jnp
from jax import lax
from jax.experimental import pallas as pl
from jax.experimental.pallas import tpu as pltpu

F32 = jnp.float32
BF16 = jnp.bfloat16

D_MODEL = 4096
SSD_INNER = 2048
SSD_HEAD_DIM = 64
SSD_HEADS = 32
SSD_GROUPS = 4
SSD_STATE = 128
SSD_CONV = 4
SSD_CHUNK = 128
SSD_GROUP_WIDTH = SSD_INNER // SSD_GROUPS
MLSTM_WIDTH = 2048
MLSTM_HEADS = 4
MLSTM_DV = 512
MLSTM_DQK = 256
FF_DENSE = 14336
N_EXPERTS = 8
FF_EXPERT = 5632
NORM_EPS = 1e-6

LANES = 128
SUBLANES = 8
VMEM_LIMIT_BYTES = 56 * 1024 * 1024

COL_Z, COL_X, COL_V, COL_O = 0, 2048, 4096, 6144
COL_Q, COL_K = 8192, 9216
COL_B, COL_C = 10240, 10752
COL_SMALL = 11264
SMALL_DT, SMALL_I, SMALL_F = 0, 32, 36
D_PROJ = 11520
PROJ_TN = 1280

MLSTM_CHUNK = 128


def _cparams(*sem):
    return pltpu.CompilerParams(dimension_semantics=sem, vmem_limit_bytes=VMEM_LIMIT_BYTES)


def _rmsnorm_kernel(x_ref, w_ref, o_ref):
    x = x_ref[...]
    ms = jnp.mean(x * x, axis=-1, keepdims=True)
    o_ref[...] = (x * lax.rsqrt(ms + NORM_EPS) * w_ref[...]).astype(o_ref.dtype)


def rmsnorm(x, w, out_dtype, tr=512):
    s, d = x.shape
    return pl.pallas_call(
        _rmsnorm_kernel,
        grid=(s // tr,),
        in_specs=[pl.BlockSpec((tr, d), lambda i: (i, 0)),
                  pl.BlockSpec((1, d), lambda i: (0, 0))],
        out_specs=pl.BlockSpec((tr, d), lambda i: (i, 0)),
        out_shape=jax.ShapeDtypeStruct((s, d), out_dtype),
        compiler_params=_cparams("parallel"),
        name="rmsnorm",
    )(x, w.reshape(1, d))


def _rmsnorm_router_kernel(x_ref, w_ref, rw_ref, u_ref, comb_ref):
    x = x_ref[...]
    ms = jnp.mean(x * x, axis=-1, keepdims=True)
    u = x * lax.rsqrt(ms + NORM_EPS) * w_ref[...]
    u_ref[...] = u.astype(u_ref.dtype)
    logits = jnp.dot(u, rw_ref[...], preferred_element_type=F32, precision=lax.Precision.HIGHEST)
    lane = lax.broadcasted_iota(jnp.int32, logits.shape, 1)
    valid = lane < N_EXPERTS
    lg = jnp.where(valid, logits, -jnp.inf)
    ex = jnp.exp(lg - jnp.max(lg, axis=1, keepdims=True))
    probs = ex / jnp.sum(ex, axis=1, keepdims=True)
    probs = jnp.where(valid, probs, -1.0)
    p1 = jnp.max(probs, axis=1, keepdims=True)
    i1 = jnp.min(jnp.where(probs == p1, lane, LANES), axis=1, keepdims=True)
    rest = jnp.where(lane == i1, -1.0, probs)
    p2 = jnp.max(rest, axis=1, keepdims=True)
    i2 = jnp.min(jnp.where(rest == p2, lane, LANES), axis=1, keepdims=True)
    comb = jnp.where(lane == i1, p1, 0.0) + jnp.where(lane == i2, p2, 0.0)
    comb_ref[...] = comb / (p1 + p2)


def rmsnorm_router(x, w, router_w, tr=512):
    s, d = x.shape
    rw = jnp.pad(router_w, ((0, 0), (0, LANES - N_EXPERTS)))
    return pl.pallas_call(
        _rmsnorm_router_kernel,
        grid=(s // tr,),
        in_specs=[pl.BlockSpec((tr, d), lambda i: (i, 0)),
                  pl.BlockSpec((1, d), lambda i: (0, 0)),
                  pl.BlockSpec((d, LANES), lambda i: (0, 0))],
        out_specs=[pl.BlockSpec((tr, d), lambda i: (i, 0)),
                   pl.BlockSpec((tr, LANES), lambda i: (i, 0))],
        out_shape=[jax.ShapeDtypeStruct((s, d), BF16),
                   jax.ShapeDtypeStruct((s, LANES), F32)],
        compiler_params=_cparams("parallel"),
        name="rmsnorm_router",
    )(x, w.reshape(1, d), rw)


def _mm_kernel(x_ref, w_ref, o_ref):
    o_ref[...] = jnp.dot(x_ref[...], w_ref[...], preferred_element_type=F32).astype(o_ref.dtype)


def matmul(x, w, out_dtype, tm, tn):
    m, k = x.shape
    n = w.shape[1]
    return pl.pallas_call(
        _mm_kernel,
        grid=(m // tm, n // tn),
        in_specs=[pl.BlockSpec((tm, k), lambda i, j: (i, 0)),
                  pl.BlockSpec((k, tn), lambda i, j: (0, j))],
        out_specs=pl.BlockSpec((tm, tn), lambda i, j: (i, j)),
        out_shape=jax.ShapeDtypeStruct((m, n), out_dtype),
        compiler_params=_cparams("parallel", "arbitrary"),
        name="in_proj",
    )(x, w)


def _out_proj_kernel(ya_ref, yb_ref, wa_ref, wb_ref, r_ref, o_ref):
    acc = jnp.dot(ya_ref[...], wa_ref[...], preferred_element_type=F32)
    acc += jnp.dot(yb_ref[...], wb_ref[...], preferred_element_type=F32)
    o_ref[...] = r_ref[...] + acc


def out_proj(y_ssd, y_ml, w_out, resid, tm=1024, tn=1024):
    m, ka = y_ssd.shape
    n = w_out.shape[1]
    return pl.pallas_call(
        _out_proj_kernel,
        grid=(m // tm, n // tn),
        in_specs=[pl.BlockSpec((tm, ka), lambda i, j: (i, 0)),
                  pl.BlockSpec((tm, ka), lambda i, j: (i, 0)),
                  pl.BlockSpec((ka, tn), lambda i, j: (0, j)),
                  pl.BlockSpec((ka, tn), lambda i, j: (1, j)),
                  pl.BlockSpec((tm, tn), lambda i, j: (i, j))],
        out_specs=pl.BlockSpec((tm, tn), lambda i, j: (i, j)),
        out_shape=jax.ShapeDtypeStruct((m, n), F32),
        input_output_aliases={4: 0},
        compiler_params=_cparams("parallel", "arbitrary"),
        name="out_proj",
    )(y_ssd, y_ml, w_out, w_out, resid)


def _silu(x):
    return x * jax.nn.sigmoid(x)


def _swiglu_up_kernel(x_ref, wg_ref, wu_ref, o_ref):
    x = x_ref[...]
    g = jnp.dot(x, wg_ref[...], preferred_element_type=F32)
    u = jnp.dot(x, wu_ref[...], preferred_element_type=F32)
    o_ref[...] = (_silu(g) * u).astype(o_ref.dtype)


def swiglu_up(x, wg, wu, tm=1024, tf=512):
    m, k = x.shape
    f = wg.shape[1]
    return pl.pallas_call(
        _swiglu_up_kernel,
        grid=(m // tm, f // tf),
        in_specs=[pl.BlockSpec((tm, k), lambda i, j: (i, 0)),
                  pl.BlockSpec((k, tf), lambda i, j: (0, j)),
                  pl.BlockSpec((k, tf), lambda i, j: (0, j))],
        out_specs=pl.BlockSpec((tm, tf), lambda i, j: (i, j)),
        out_shape=jax.ShapeDtypeStruct((m, f), BF16),
        compiler_params=_cparams("parallel", "arbitrary"),
        name="swiglu_up",
    )(x, wg, wu)


def _moe_up_kernel(x_ref, comb_ref, wg_ref, wu_ref, o_ref):
    e = pl.program_id(1)
    x = x_ref[...]
    g = jnp.dot(x, wg_ref[...], preferred_element_type=F32)
    u = jnp.dot(x, wu_ref[...], preferred_element_type=F32)
    comb = comb_ref[...]
    lane = lax.broadcasted_iota(jnp.int32, comb.shape, 1)
    c = jnp.sum(jnp.where(lane == e, comb, 0.0), axis=1, keepdims=True)
    o_ref[...] = (_silu(g) * u * c).astype(o_ref.dtype)


def moe_up_dense(x, comb, wg, wu, tm=1024, tf=512):
    m, k = x.shape
    ne, _, f = wg.shape
    nf = f // tf
    return pl.pallas_call(
        _moe_up_kernel,
        grid=(m // tm, ne, nf),
        in_specs=[pl.BlockSpec((tm, k), lambda i, e, j: (i, 0)),
                  pl.BlockSpec((tm, LANES), lambda i, e, j: (i, 0)),
                  pl.BlockSpec((None, k, tf), lambda i, e, j: (e, 0, j)),
                  pl.BlockSpec((None, k, tf), lambda i, e, j: (e, 0, j))],
        out_specs=pl.BlockSpec((tm, tf), lambda i, e, j: (i, e * nf + j)),
        out_shape=jax.ShapeDtypeStruct((m, ne * f), BF16),
        compiler_params=_cparams("parallel", "arbitrary", "arbitrary"),
        name="moe_up",
    )(x, comb, wg, wu)


def _down_kernel(a_ref, w_ref, r_ref, o_ref):
    kk = pl.program_id(2)

    @pl.when(kk == 0)
    def _():
        o_ref[...] = r_ref[...]

    o_ref[...] += jnp.dot(a_ref[...], w_ref[...], preferred_element_type=F32)


def down_proj(a, w, resid, tk, tm=1024, tn=1024):
    m, k = a.shape
    n = w.shape[1]
    return pl.pallas_call(
        _down_kernel,
        grid=(m // tm, n // tn, k // tk),
        in_specs=[pl.BlockSpec((tm, tk), lambda i, j, kk: (i, kk)),
                  pl.BlockSpec((tk, tn), lambda i, j, kk: (kk, j)),
                  pl.BlockSpec((tm, tn), lambda i, j, kk: (i, j))],
        out_specs=pl.BlockSpec((tm, tn), lambda i, j, kk: (i, j)),
        out_shape=jax.ShapeDtypeStruct((m, n), F32),
        input_output_aliases={2: 0},
        compiler_params=_cparams("parallel", "arbitrary", "arbitrary"),
        name="down_proj",
    )(a, w, resid)


def _softplus(x):
    return jnp.maximum(x, 0.0) + jnp.log1p(jnp.exp(-jnp.abs(x)))


def _tril_ones(n):
    r = lax.broadcasted_iota(jnp.int32, (n, n), 0)
    c = lax.broadcasted_iota(jnp.int32, (n, n), 1)
    return r >= c


def _ssd_kernel(z_ref, x_ref, b_ref, c_ref, sm_ref, convw_ref, convb_ref, smb_ref, alog_ref,
                dskip_ref, nw_ref, expand_ref, o_ref, xpad_ref, state_ref):
    L = SSD_CHUNK
    HI = lax.Precision.HIGHEST

    @pl.when(pl.program_id(0) == 0)
    def _():
        xpad_ref[0:SUBLANES, :] = jnp.zeros((SUBLANES, xpad_ref.shape[1]), F32)
        state_ref[...] = jnp.zeros(state_ref.shape, F32)

    xpad_ref[SUBLANES:SUBLANES + L, 0:SSD_INNER] = x_ref[...]
    xpad_ref[SUBLANES:SUBLANES + L, SSD_INNER:SSD_INNER + 512] = b_ref[...]
    xpad_ref[SUBLANES:SUBLANES + L, SSD_INNER + 512:SSD_INNER + 1024] = c_ref[...]
    conv = convb_ref[...]
    for j in range(SSD_CONV):
        off = SUBLANES - (SSD_CONV - 1) + j
        conv = conv + xpad_ref[off:off + L, :] * convw_ref[j:j + 1, :]
    xpad_ref[0:SUBLANES, :] = xpad_ref[L:L + SUBLANES, :]
    xbc = _silu(conv)
    xs = xbc[:, 0:SSD_INNER]

    causal = _tril_ones(L)
    dt = _softplus(sm_ref[...] + smb_ref[...])
    da = dt * (-jnp.exp(alog_ref[...]))
    acs = jnp.dot(causal.astype(F32), da, preferred_element_type=F32, precision=HI)
    acs_t = acs.T
    a_last = acs[L - 1:L, :]
    expand = expand_ref[...]
    dt_e = jnp.dot(dt, expand, preferred_element_type=F32, precision=HI)
    eacs_e = jnp.dot(jnp.exp(acs), expand, preferred_element_type=F32, precision=HI)
    dst_e = jnp.dot(jnp.exp(a_last - acs), expand, preferred_element_type=F32, precision=HI)
    xdt = xs * dt_e
    xdec = (xdt * dst_e).astype(BF16)
    xdt_b = xdt.astype(BF16)
    chunk_decay_e = eacs_e[L - 1:L, :]

    y = xs * dskip_ref[...]
    zgate = _silu(z_ref[...])
    lane = lax.broadcasted_iota(jnp.int32, (L, LANES), 1)
    lo_half = lane < SSD_HEAD_DIM
    for g in range(SSD_GROUPS):
        gs = slice(g * SSD_GROUP_WIDTH, (g + 1) * SSD_GROUP_WIDTH)
        bg = xbc[:, SSD_INNER + g * SSD_STATE:SSD_INNER + (g + 1) * SSD_STATE]
        cg = xbc[:, SSD_INNER + 512 + g * SSD_STATE:SSD_INNER + 512 + (g + 1) * SSD_STATE]
        cg_b = cg.astype(BF16)
        cb = lax.dot_general(cg_b, bg.astype(BF16), (((1,), (1,)), ((), ())),
                             preferred_element_type=F32)
        h_enter = state_ref[g]
        y_off = jnp.dot(cg_b, h_enter.astype(BF16), preferred_element_type=F32)
        st_new = jnp.dot(bg.T.astype(BF16), xdec[:, gs], preferred_element_type=F32)
        state_ref[g] = chunk_decay_e[:, gs] * h_enter + st_new
        yd_parts = []
        for p in range(4):
            j0 = g * 8 + 2 * p
            ms = []
            for j in (j0, j0 + 1):
                seg = jnp.where(causal, acs[:, j:j + 1] - acs_t[j:j + 1, :], -jnp.inf)
                ms.append((cb * jnp.exp(seg)).astype(BF16))
            xp = xdt_b[:, g * SSD_GROUP_WIDTH + p * LANES:g * SSD_GROUP_WIDTH + (p + 1) * LANES]
            zero = jnp.zeros_like(xp)
            rhs = jnp.concatenate([jnp.where(lo_half, xp, zero), jnp.where(lo_half, zero, xp)], axis=0)
            yd_parts.append(jnp.dot(jnp.concatenate(ms, axis=1), rhs, preferred_element_type=F32))
        y_diag = jnp.concatenate(yd_parts, axis=1)
        yg = (y[:, gs] + y_diag + y_off * eacs_e[:, gs]) * zgate[:, gs]
        msq = jnp.mean(yg * yg, axis=1, keepdims=True)
        o_ref[:, gs] = (yg * lax.rsqrt(msq + NORM_EPS) * nw_ref[:, gs]).astype(o_ref.dtype)


def ssd_mixer(proj, conv_w, conv_b, small_bias, a_log, d_skip, norm_w):
    s = proj.shape[0]
    L = SSD_CHUNK
    conv_dim = conv_w.shape[1]
    alog = jnp.pad(a_log, (0, LANES - SSD_HEADS)).reshape(1, LANES)
    dskip_e = jnp.repeat(d_skip, SSD_HEAD_DIM).reshape(1, SSD_INNER)
    head = jnp.arange(LANES)[:, None]
    expand = (jnp.arange(SSD_INNER)[None, :] // SSD_HEAD_DIM == head).astype(F32)
    full = lambda shp: pl.BlockSpec(shp, lambda i: (0,) * len(shp))
    return pl.pallas_call(
        _ssd_kernel,
        grid=(s // L,),
        in_specs=[pl.BlockSpec((L, SSD_INNER), lambda i: (i, COL_Z // SSD_INNER)),
                  pl.BlockSpec((L, SSD_INNER), lambda i: (i, COL_X // SSD_INNER)),
                  pl.BlockSpec((L, 512), lambda i: (i, COL_B // 512)),
                  pl.BlockSpec((L, 512), lambda i: (i, COL_C // 512)),
                  pl.BlockSpec((L, LANES), lambda i: (i, COL_SMALL // LANES)),
                  full((SSD_CONV, conv_dim)), full((1, conv_dim)), full((1, LANES)), full((1, LANES)),
                  full((1, SSD_INNER)), full((1, SSD_INNER)), full((LANES, SSD_INNER))],
        out_specs=pl.BlockSpec((L, SSD_INNER), lambda i: (i, 0)),
        out_shape=jax.ShapeDtypeStruct((s, SSD_INNER), BF16),
        scratch_shapes=[pltpu.VMEM((L + 2 * SUBLANES, conv_dim), F32),
                        pltpu.VMEM((SSD_GROUPS, SSD_STATE, SSD_GROUP_WIDTH), F32)],
        compiler_params=_cparams("arbitrary"),
        name="ssd_mixer",
    )(proj, proj, proj, proj, proj, conv_w, conv_b.reshape(1, conv_dim), small_bias, alog,
      dskip_e, norm_w.reshape(1, SSD_INNER), expand)


def _log_sigmoid(x):
    return jnp.minimum(x, 0.0) - jnp.log1p(jnp.exp(-jnp.abs(x)))


def _mlstm_kernel(q_ref, k_ref, v_ref, o_ref, sm_ref, smb_ref, nw_ref, out_ref, c_ref, n_ref, m_ref):
    L = MLSTM_CHUNK
    HI = lax.Precision.HIGHEST

    @pl.when(pl.program_id(0) == 0)
    def _():
        c_ref[...] = jnp.zeros(c_ref.shape, F32)
        n_ref[...] = jnp.zeros(n_ref.shape, F32)
        m_ref[...] = jnp.zeros(m_ref.shape, F32)

    causal = _tril_ones(L)
    sm = sm_ref[...] + smb_ref[...]
    bcs = jnp.dot(causal.astype(F32), _log_sigmoid(sm), preferred_element_type=F32, precision=HI)
    sm_t = sm.T
    bcs_t = bcs.T
    for h in range(MLSTM_HEADS):
        li_col = sm[:, SMALL_I + h:SMALL_I + h + 1]
        b_col = bcs[:, SMALL_F + h:SMALL_F + h + 1]
        r_row = sm_t[SMALL_I + h:SMALL_I + h + 1, :] - bcs_t[SMALL_F + h:SMALL_F + h + 1, :]
        m_prev = m_ref[h:h + 1, 0:1]
        inter = b_col + m_prev
        dm = jnp.where(causal, b_col + r_row, -jnp.inf)
        m_t = jnp.maximum(inter, jnp.max(dm, axis=1, keepdims=True))
        w_intra = jnp.exp(dm - m_t)
        w_inter = jnp.exp(inter - m_t)
        qf = q_ref[:, h * MLSTM_DQK:(h + 1) * MLSTM_DQK] * (MLSTM_DQK ** -0.5)
        kf = k_ref[:, h * MLSTM_DQK:(h + 1) * MLSTM_DQK]
        qb = qf.astype(BF16)
        vb = v_ref[:, h * MLSTM_DV:(h + 1) * MLSTM_DV].astype(BF16)
        sqk = lax.dot_general(qb, kf.astype(BF16), (((1,), (1,)), ((), ())),
                              preferred_element_type=F32) * w_intra
        c_st = c_ref[h]
        n_row = n_ref[h:h + 1, :]
        num = (w_inter * jnp.dot(qb, c_st.astype(BF16), preferred_element_type=F32)
               + jnp.dot(sqk.astype(BF16), vb, preferred_element_type=F32))
        qn = jnp.sum(qf * n_row, axis=1, keepdims=True)
        den = w_inter * qn + jnp.sum(sqk, axis=1, keepdims=True)
        hout = num / jnp.maximum(jnp.abs(den), jnp.exp(-m_t))
        b_last = b_col[L - 1:L, :]
        g_col = b_last - b_col + li_col
        m_new = jnp.maximum(b_last + m_prev, jnp.max(g_col, axis=0, keepdims=True))
        decay = jnp.exp(b_last + m_prev - m_new)
        kw = kf * jnp.exp(g_col - m_new)
        c_ref[h] = decay * c_st + jnp.dot(kw.T.astype(BF16), vb, preferred_element_type=F32)
        n_ref[h:h + 1, :] = decay * n_row + jnp.sum(kw, axis=0, keepdims=True)
        m_ref[h:h + 1, :] = jnp.broadcast_to(m_new, (1, LANES))
        vs = slice(h * MLSTM_DV, (h + 1) * MLSTM_DV)
        msq = jnp.mean(hout * hout, axis=1, keepdims=True)
        hn = hout * lax.rsqrt(msq + NORM_EPS) * nw_ref[:, vs]
        out_ref[:, vs] = (jax.nn.sigmoid(o_ref[:, vs]) * hn).astype(out_ref.dtype)


def mlstm_mixer(proj, small_bias, norm_w):
    s = proj.shape[0]
    L = MLSTM_CHUNK
    qk_w = MLSTM_HEADS * MLSTM_DQK
    full = lambda shp: pl.BlockSpec(shp, lambda i: (0,) * len(shp))
    return pl.pallas_call(
        _mlstm_kernel,
        grid=(s // L,),
        in_specs=[pl.BlockSpec((L, qk_w), lambda i: (i, COL_Q // qk_w)),
                  pl.BlockSpec((L, qk_w), lambda i: (i, COL_K // qk_w)),
                  pl.BlockSpec((L, MLSTM_WIDTH), lambda i: (i, COL_V // MLSTM_WIDTH)),
                  pl.BlockSpec((L, MLSTM_WIDTH), lambda i: (i, COL_O // MLSTM_WIDTH)),
                  pl.BlockSpec((L, LANES), lambda i: (i, COL_SMALL // LANES)),
                  full((1, LANES)), full((1, MLSTM_WIDTH))],
        out_specs=pl.BlockSpec((L, MLSTM_WIDTH), lambda i: (i, 0)),
        out_shape=jax.ShapeDtypeStruct((s, MLSTM_WIDTH), BF16),
        scratch_shapes=[pltpu.VMEM((MLSTM_HEADS, MLSTM_DQK, MLSTM_DV), F32),
                        pltpu.VMEM((SUBLANES, MLSTM_DQK), F32),
                        pltpu.VMEM((SUBLANES, LANES), F32)],
        compiler_params=_cparams("arbitrary"),
        name="mlstm_mixer",
    )(proj, proj, proj, proj, proj, small_bias, norm_w.reshape(1, MLSTM_WIDTH))


def _permute_w_in(w_in):
    o_z, o_xbc, o_dt = 0, 2048, 5120
    o_q, o_k, o_v, o_o, o_i, o_f = 5152, 6176, 7200, 9248, 11296, 11300
    cols = lambda a, n: w_in[:, a:a + n]
    parts = [cols(o_z, 2048), cols(o_xbc, 2048), cols(o_v, 2048), cols(o_o, 2048),
             cols(o_q, 1024), cols(o_k, 1024), cols(o_xbc + 2048, 512), cols(o_xbc + 2560, 512),
             cols(o_dt, 32), cols(o_i, 4), cols(o_f, 4)]
    used = COL_SMALL + 40
    parts.append(jnp.zeros((w_in.shape[0], D_PROJ - used), w_in.dtype))
    return jnp.concatenate(parts, axis=1).astype(BF16)


def _small_bias(dt_bias, i_bias, f_bias):
    v = jnp.concatenate([dt_bias, i_bias, f_bias, jnp.zeros((LANES - 40,), F32)])
    return v.reshape(1, LANES)


def kernel(x, norm_mix_w, w_in, conv_w, conv_b, dt_bias, a_log, d_skip, ssd_norm_w, mlstm_i_bias,
           mlstm_f_bias, mlstm_norm_w, w_out, norm_ffn_w, ffn_w_gate, ffn_w_up, ffn_w_down, router_w,
           moe_w_gate, moe_w_up, moe_w_down, final_norm_w):
    bsz, s, d = x.shape
    depth = w_in.shape[0]
    outs = []
    for b in range(bsz):
        h = x[b]
        for layer in range(depth):
            u = rmsnorm(h, norm_mix_w[layer], BF16)
            proj = matmul(u, _permute_w_in(w_in[layer]), F32, tm=1024, tn=PROJ_TN)
            sbias = _small_bias(dt_bias[layer], mlstm_i_bias[layer], mlstm_f_bias[layer])
            y_ssd = ssd_mixer(proj, conv_w[layer], conv_b[layer], sbias, a_log[layer], d_skip[layer],
                              ssd_norm_w[layer])
            y_ml = mlstm_mixer(proj, sbias, mlstm_norm_w[layer])
            h = out_proj(y_ssd, y_ml, w_out[layer].astype(BF16), h)
            j = layer // 2
            if layer % 2 == 0:
                u = rmsnorm(h, norm_ffn_w[layer], BF16)
                a = swiglu_up(u, ffn_w_gate[j].astype(BF16), ffn_w_up[j].astype(BF16))
                h = down_proj(a, ffn_w_down[j].astype(BF16), h, tk=2048)
            else:
                u, comb = rmsnorm_router(h, norm_ffn_w[layer], router_w[j])
                a = moe_up_dense(u, comb, moe_w_gate[j].astype(BF16), moe_w_up[j].astype(BF16))
                wd = moe_w_down[j].astype(BF16).reshape(N_EXPERTS * FF_EXPERT, d)
                h = down_proj(a, wd, h, tk=FF_EXPERT // 2)
        outs.append(rmsnorm(h, final_norm_w, F32))
    return jnp.stack(outs, axis=0)
```

```python
import functools

import jax
import jax.numpy as jnp
from jax import lax
from jax.experimental import pallas as pl
from jax.experimental.pallas import tpu as pltpu

F32 = jnp.float32
BF16 = jnp.bfloat16

D_MODEL = 4096
SSD_INNER = 2048
SSD_HEAD_DIM = 64
SSD_HEADS = 32
SSD_GROUPS = 4
SSD_STATE = 128
SSD_CONV = 4
SSD_CHUNK = 128
SSD_GROUP_WIDTH = SSD_INNER // SSD_GROUPS
MLSTM_WIDTH = 2048
MLSTM_HEADS = 4
MLSTM_DV = 512
MLSTM_DQK = 256
FF_DENSE = 14336
N_EXPERTS = 8
FF_EXPERT = 5632
NORM_EPS = 1e-6

LANES = 128
SUBLANES = 8
VMEM_LIMIT_BYTES = 56 * 1024 * 1024

COL_Z, COL_X, COL_V, COL_O = 0, 2048, 4096, 6144
COL_Q, COL_K = 8192, 9216
COL_B, COL_C = 10240, 10752
COL_SMALL = 11264
SMALL_DT, SMALL_I, SMALL_F = 0, 32, 36
D_PROJ = 11520
PROJ_TN = 1280

MLSTM_CHUNK = 128


def _cparams(*sem):
    return pltpu.CompilerParams(dimension_semantics=sem, vmem_limit_bytes=VMEM_LIMIT_BYTES)


def _rmsnorm_kernel(x_ref, w_ref, o_ref):
    x = x_ref[...]
    ms = jnp.mean(x * x, axis=-1, keepdims=True)
    o_ref[...] = (x * lax.rsqrt(ms + NORM_EPS) * w_ref[...]).astype(o_ref.dtype)


def rmsnorm(x, w, out_dtype, tr=512):
    s, d = x.shape
    return pl.pallas_call(
        _rmsnorm_kernel,
        grid=(s // tr,),
        in_specs=[pl.BlockSpec((tr, d), lambda i: (i, 0)),
                  pl.BlockSpec((1, d), lambda i: (0, 0))],
        out_specs=pl.BlockSpec((tr, d), lambda i: (i, 0)),
        out_shape=jax.ShapeDtypeStruct((s, d), out_dtype),
        compiler_params=_cparams("parallel"),
        name="rmsnorm",
    )(x, w.reshape(1, d))


RT_SEL = 0
RT_C1, RT_C2 = 8, 9
RT_I1, RT_I2 = 10, 11


def _router_kernel(x_ref, w_ref, rw_ref, rt_ref):
    x = x_ref[...]
    ms = jnp.mean(x * x, axis=-1, keepdims=True)
    u = x * lax.rsqrt(ms + NORM_EPS) * w_ref[...]
    logits = jnp.dot(u, rw_ref[...], preferred_element_type=F32, precision=lax.Precision.HIGHEST)
    lane = lax.broadcasted_iota(jnp.int32, logits.shape, 1)
    valid = lane < N_EXPERTS
    lg = jnp.where(valid, logits, -jnp.inf)
    ex = jnp.exp(lg - jnp.max(lg, axis=1, keepdims=True))
    probs = ex / jnp.sum(ex, axis=1, keepdims=True)
    probs = jnp.where(valid, probs, -1.0)
    p1 = jnp.max(probs, axis=1, keepdims=True)
    i1 = jnp.min(jnp.where(probs == p1, lane, LANES), axis=1, keepdims=True)
    rest = jnp.where(lane == i1, -1.0, probs)
    p2 = jnp.max(rest, axis=1, keepdims=True)
    i2 = jnp.min(jnp.where(rest == p2, lane, LANES), axis=1, keepdims=True)
    tot = p1 + p2
    rt = jnp.where((lane == i1) | (lane == i2), 1.0, 0.0)
    rt = jnp.where(lane == RT_C1, p1 / tot, rt)
    rt = jnp.where(lane == RT_C2, p2 / tot, rt)
    rt = jnp.where(lane == RT_I1, i1.astype(F32), rt)
    rt = jnp.where(lane == RT_I2, i2.astype(F32), rt)
    rt_ref[...] = rt


def router(x, w, router_w, tr=512):
    s, d = x.shape
    rw = jnp.pad(router_w, ((0, 0), (0, LANES - N_EXPERTS)))
    return pl.pallas_call(
        _router_kernel,
        grid=(s // tr,),
        in_specs=[pl.BlockSpec((tr, d), lambda i: (i, 0)),
                  pl.BlockSpec((1, d), lambda i: (0, 0)),
                  pl.BlockSpec((d, LANES), lambda i: (0, 0))],
        out_specs=pl.BlockSpec((tr, LANES), lambda i: (i, 0)),
        out_shape=jax.ShapeDtypeStruct((s, LANES), F32),
        compiler_params=_cparams("parallel"),
        name="router",
    )(x, w.reshape(1, d), rw)


MOE_TM = 512
PLAN_BLK = 512


def _plan_kernel(rt_ref, dest_ref, meta_ref):
    s = rt_ref.shape[0]
    nblk = s // PLAN_BLK
    lane = lax.broadcasted_iota(jnp.int32, (PLAN_BLK, LANES), 1)
    is_exp = lane < N_EXPERTS
    r = lax.broadcasted_iota(jnp.int32, (PLAN_BLK, PLAN_BLK), 0)
    c = lax.broadcasted_iota(jnp.int32, (PLAN_BLK, PLAN_BLK), 1)
    strict_tril = (r > c).astype(BF16)

    def rank_body(b, carry):
        rows = pl.ds(pl.multiple_of(b * PLAN_BLK, PLAN_BLK), PLAN_BLK)
        sel = jnp.where(is_exp, rt_ref[rows, :], 0.0)
        rank = jnp.dot(strict_tril, sel.astype(BF16), preferred_element_type=F32) + carry
        dest_ref[rows, :] = rank.astype(jnp.int32)
        return carry + jnp.sum(sel, axis=0, keepdims=True)

    counts = lax.fori_loop(0, nblk, rank_body, jnp.zeros((1, LANES), F32))
    tiles = jnp.floor((counts + (MOE_TM - 1)) * (1.0 / MOE_TM))
    er = lax.broadcasted_iota(jnp.int32, (LANES, LANES), 0)
    ec = lax.broadcasted_iota(jnp.int32, (LANES, LANES), 1)
    before = (er < ec).astype(F32)
    tile_start = jnp.dot(jnp.broadcast_to(tiles, (SUBLANES, LANES)), before, preferred_element_type=F32,
                         precision=lax.Precision.HIGHEST)[0:1, :]
    row_off = tile_start * MOE_TM
    meta_ref[0:1, :] = counts.astype(jnp.int32)
    meta_ref[1:2, :] = tiles.astype(jnp.int32)
    meta_ref[2:3, :] = tile_start.astype(jnp.int32)
    meta_ref[3:SUBLANES, :] = jnp.zeros((SUBLANES - 3, LANES), jnp.int32)

    def dest_body(b, _):
        rows = pl.ds(pl.multiple_of(b * PLAN_BLK, PLAN_BLK), PLAN_BLK)
        rt = rt_ref[rows, :]
        pos = dest_ref[rows, :].astype(F32) + row_off
        i1 = rt[:, RT_I1:RT_I1 + 1].astype(jnp.int32)
        i2 = rt[:, RT_I2:RT_I2 + 1].astype(jnp.int32)
        d1 = jnp.sum(jnp.where(lane == i1, pos, 0.0), axis=1, keepdims=True)
        d2 = jnp.sum(jnp.where(lane == i2, pos, 0.0), axis=1, keepdims=True)
        dest_ref[rows, :] = jnp.where(lane == 0, d1, jnp.where(lane == 1, d2, 0.0)).astype(jnp.int32)
        return 0

    lax.fori_loop(0, nblk, dest_body, 0)


def moe_plan(rt):
    s = rt.shape[0]
    return pl.pallas_call(
        _plan_kernel,
        out_shape=[jax.ShapeDtypeStruct((s, LANES), jnp.int32),
                   jax.ShapeDtypeStruct((SUBLANES, LANES), jnp.int32)],
        compiler_params=pltpu.CompilerParams(vmem_limit_bytes=VMEM_LIMIT_BYTES),
        name="moe_plan",
    )(rt)


def _invert_kernel(d1_ref, d2_ref, src_ref):
    n_rows = src_ref.shape[0]
    n_tok = d1_ref.shape[0]

    def zero(i, _):
        src_ref[i] = 0
        return 0

    lax.fori_loop(0, n_rows, zero, 0)

    def put(t, _):
        src_ref[d1_ref[t]] = t
        src_ref[d2_ref[t]] = t
        return 0

    lax.fori_loop(0, n_tok, put, 0)


def moe_invert(d1, d2, n_rows):
    smem = pl.BlockSpec(memory_space=pltpu.SMEM)
    return pl.pallas_call(
        _invert_kernel,
        in_specs=[smem, smem],
        out_specs=smem,
        out_shape=jax.ShapeDtypeStruct((n_rows,), jnp.int32),
        name="moe_invert",
    )(d1, d2)


GATHER_TM = 256


def _gather_norm_kernel(src_ref, h_hbm, w_ref, o_ref, buf_ref, sem):
    base = pl.program_id(0) * GATHER_TM

    def row_copy(i):
        return pltpu.make_async_copy(h_hbm.at[pl.ds(src_ref[base + i], 1), :],
                                     buf_ref.at[pl.ds(i, 1), :], sem)

    def start(i, _):
        row_copy(i).start()
        return 0

    def wait(i, _):
        row_copy(i).wait()
        return 0

    lax.fori_loop(0, GATHER_TM, start, 0)
    lax.fori_loop(0, GATHER_TM, wait, 0)
    x = buf_ref[...]
    ms = jnp.mean(x * x, axis=-1, keepdims=True)
    o_ref[...] = (x * lax.rsqrt(ms + NORM_EPS) * w_ref[...]).astype(o_ref.dtype)


def moe_gather_norm(src, h, w):
    n_rows = src.shape[0]
    d = h.shape[1]
    return pl.pallas_call(
        _gather_norm_kernel,
        grid_spec=pltpu.PrefetchScalarGridSpec(
            num_scalar_prefetch=1,
            grid=(n_rows // GATHER_TM,),
            in_specs=[pl.BlockSpec(memory_space=pl.ANY),
                      pl.BlockSpec((1, d), lambda i, src: (0, 0))],
            out_specs=pl.BlockSpec((GATHER_TM, d), lambda i, src: (i, 0)),
            scratch_shapes=[pltpu.VMEM((GATHER_TM, d), F32), pltpu.SemaphoreType.DMA(())]),
        out_shape=jax.ShapeDtypeStruct((n_rows, d), BF16),
        compiler_params=_cparams("arbitrary"),
        name="moe_gather_norm",
    )(src, h, w.reshape(1, d))


def _combine_kernel(d1_ref, d2_ref, h_ref, rt_ref, y_hbm, o_ref, ya_ref, yb_ref, sem):
    base = pl.program_id(0) * GATHER_TM

    def copy_a(i):
        return pltpu.make_async_copy(y_hbm.at[pl.ds(d1_ref[base + i], 1), :], ya_ref.at[pl.ds(i, 1), :], sem.at[0])

    def copy_b(i):
        return pltpu.make_async_copy(y_hbm.at[pl.ds(d2_ref[base + i], 1), :], yb_ref.at[pl.ds(i, 1), :], sem.at[1])

    def start(i, _):
        copy_a(i).start()
        copy_b(i).start()
        return 0

    def wait(i, _):
        copy_a(i).wait()
        copy_b(i).wait()
        return 0

    lax.fori_loop(0, GATHER_TM, start, 0)
    lax.fori_loop(0, GATHER_TM, wait, 0)
    rt = rt_ref[...]
    o_ref[...] = (h_ref[...] + rt[:, RT_C1:RT_C1 + 1] * ya_ref[...] + rt[:, RT_C2:RT_C2 + 1] * yb_ref[...])


def moe_combine(d1, d2, h, rt, y):
    s, d = h.shape
    return pl.pallas_call(
        _combine_kernel,
        grid_spec=pltpu.PrefetchScalarGridSpec(
            num_scalar_prefetch=2,
            grid=(s // GATHER_TM,),
            in_specs=[pl.BlockSpec((GATHER_TM, d), lambda i, a, b: (i, 0)),
                      pl.BlockSpec((GATHER_TM, LANES), lambda i, a, b: (i, 0)),
                      pl.BlockSpec(memory_space=pl.ANY)],
            out_specs=pl.BlockSpec((GATHER_TM, d), lambda i, a, b: (i, 0)),
            scratch_shapes=[pltpu.VMEM((GATHER_TM, d), F32), pltpu.VMEM((GATHER_TM, d), F32),
                            pltpu.SemaphoreType.DMA((2,))]),
        out_shape=jax.ShapeDtypeStruct((s, d), F32),
        compiler_params=_cparams("arbitrary"),
        name="moe_combine",
    )(d1, d2, h, rt, y)


def _mm_kernel(x_ref, w_ref, o_ref):
    o_ref[...] = jnp.dot(x_ref[...], w_ref[...], preferred_element_type=F32).astype(o_ref.dtype)


def matmul(x, w, out_dtype, tm, tn):
    m, k = x.shape
    n = w.shape[1]
    return pl.pallas_call(
        _mm_kernel,
        grid=(m // tm, n // tn),
        in_specs=[pl.BlockSpec((tm, k), lambda i, j: (i, 0)),
                  pl.BlockSpec((k, tn), lambda i, j: (0, j))],
        out_specs=pl.BlockSpec((tm, tn), lambda i, j: (i, j)),
        out_shape=jax.ShapeDtypeStruct((m, n), out_dtype),
        compiler_params=_cparams("parallel", "arbitrary"),
        name="in_proj",
    )(x, w)


def _out_proj_kernel(ya_ref, yb_ref, wa_ref, wb_ref, r_ref, o_ref):
    acc = jnp.dot(ya_ref[...], wa_ref[...], preferred_element_type=F32)
    acc += jnp.dot(yb_ref[...], wb_ref[...], preferred_element_type=F32)
    o_ref[...] = r_ref[...] + acc


def out_proj(y_ssd, y_ml, w_out, resid, tm=1024, tn=1024):
    m, ka = y_ssd.shape
    n = w_out.shape[1]
    return pl.pallas_call(
        _out_proj_kernel,
        grid=(m // tm, n // tn),
        in_specs=[pl.BlockSpec((tm, ka), lambda i, j: (i, 0)),
                  pl.BlockSpec((tm, ka), lambda i, j: (i, 0)),
                  pl.BlockSpec((ka, tn), lambda i, j: (0, j)),
                  pl.BlockSpec((ka, tn), lambda i, j: (1, j)),
                  pl.BlockSpec((tm, tn), lambda i, j: (i, j))],
        out_specs=pl.BlockSpec((tm, tn), lambda i, j: (i, j)),
        out_shape=jax.ShapeDtypeStruct((m, n), F32),
        input_output_aliases={4: 0},
        compiler_params=_cparams("parallel", "arbitrary"),
        name="out_proj",
    )(y_ssd, y_ml, w_out, w_out, resid)


def _silu(x):
    return x * jax.nn.sigmoid(x)


def _swiglu_up_kernel(x_ref, wg_ref, wu_ref, o_ref):
    x = x_ref[...]
    g = jnp.dot(x, wg_ref[...], preferred_element_type=F32)
    u = jnp.dot(x, wu_ref[...], preferred_element_type=F32)
    o_ref[...] = (_silu(g) * u).astype(o_ref.dtype)


def swiglu_up(x, wg, wu, tm=1024, tf=512):
    m, k = x.shape
    f = wg.shape[1]
    return pl.pallas_call(
        _swiglu_up_kernel,
        grid=(m // tm, f // tf),
        in_specs=[pl.BlockSpec((tm, k), lambda i, j: (i, 0)),
                  pl.BlockSpec((k, tf), lambda i, j: (0, j)),
                  pl.BlockSpec((k, tf), lambda i, j: (0, j))],
        out_specs=pl.BlockSpec((tm, tf), lambda i, j: (i, j)),
        out_shape=jax.ShapeDtypeStruct((m, f), BF16),
        compiler_params=_cparams("parallel", "arbitrary"),
        name="swiglu_up",
    )(x, wg, wu)


def _moe_up_kernel(eot_ref, nt_ref, x_ref, wg_ref, wu_ref, o_ref):
    @pl.when(pl.program_id(1) < nt_ref[0])
    def _():
        x = x_ref[...]
        g = jnp.dot(x, wg_ref[...], preferred_element_type=F32)
        u = jnp.dot(x, wu_ref[...], preferred_element_type=F32)
        o_ref[...] = (_silu(g) * u).astype(o_ref.dtype)

    @pl.when(pl.program_id(1) >= nt_ref[0])
    def _():
        o_ref[...] = jnp.zeros(o_ref.shape, o_ref.dtype)


def moe_up(eot, nt, xs, wg, wu, tf=512):
    r, k = xs.shape
    f = wg.shape[2]
    row = lambda j, t, eot, nt: (jnp.minimum(t, nt[0] - 1), 0)
    wmap = lambda j, t, eot, nt: (eot[t], 0, j)
    return pl.pallas_call(
        _moe_up_kernel,
        grid_spec=pltpu.PrefetchScalarGridSpec(
            num_scalar_prefetch=2,
            grid=(f // tf, r // MOE_TM),
            in_specs=[pl.BlockSpec((MOE_TM, k), row),
                      pl.BlockSpec((None, k, tf), wmap),
                      pl.BlockSpec((None, k, tf), wmap)],
            out_specs=pl.BlockSpec((MOE_TM, tf), lambda j, t, eot, nt: (t, j))),
        out_shape=jax.ShapeDtypeStruct((r, f), BF16),
        compiler_params=_cparams("arbitrary", "arbitrary"),
        name="moe_up",
    )(eot, nt, xs, wg, wu)


def _moe_down_kernel(eot_ref, nt_ref, a_ref, w_ref, o_ref):
    @pl.when(pl.program_id(1) < nt_ref[0])
    def _():
        o_ref[...] = jnp.dot(a_ref[...], w_ref[...], preferred_element_type=F32)

    @pl.when(pl.program_id(1) >= nt_ref[0])
    def _():
        o_ref[...] = jnp.zeros(o_ref.shape, o_ref.dtype)


def moe_down(eot, nt, a, wd, tn=1024):
    r, f = a.shape
    n = wd.shape[2]
    return pl.pallas_call(
        _moe_down_kernel,
        grid_spec=pltpu.PrefetchScalarGridSpec(
            num_scalar_prefetch=2,
            grid=(n // tn, r // MOE_TM),
            in_specs=[pl.BlockSpec((MOE_TM, f), lambda j, t, eot, nt: (jnp.minimum(t, nt[0] - 1), 0)),
                      pl.BlockSpec((None, f, tn), lambda j, t, eot, nt: (eot[t], 0, j))],
            out_specs=pl.BlockSpec((MOE_TM, tn), lambda j, t, eot, nt: (t, j))),
        out_shape=jax.ShapeDtypeStruct((r, n), F32),
        compiler_params=_cparams("arbitrary", "arbitrary"),
        name="moe_down",
    )(eot, nt, a, wd)


def moe_swiglu(h, norm_w, router_w, wg, wu, wd):
    s, d = h.shape
    n_tiles_max = (2 * s) // MOE_TM + N_EXPERTS
    n_rows = n_tiles_max * MOE_TM
    rt = router(h, norm_w, router_w)
    dest, meta = moe_plan(rt)
    d1, d2 = dest[:, 0], dest[:, 1]
    tiles, tile_start = meta[1, :N_EXPERTS], meta[2, :N_EXPERTS]
    nt = jnp.sum(tiles).reshape(1)
    tile_end = tile_start + tiles
    tid = jnp.minimum(jnp.arange(n_tiles_max, dtype=jnp.int32), nt - 1)
    eot = jnp.sum((tid[:, None] >= tile_end[None, :]).astype(jnp.int32), axis=1)
    src = moe_invert(d1, d2, n_rows)
    xs = moe_gather_norm(src, h, norm_w)
    a = moe_up(eot, nt, xs, wg, wu)
    y = moe_down(eot, nt, a, wd)
    return moe_combine(d1, d2, h, rt, y)


def _down_kernel(a_ref, w_ref, r_ref, o_ref):
    kk = pl.program_id(2)

    @pl.when(kk == 0)
    def _():
        o_ref[...] = r_ref[...]

    o_ref[...] += jnp.dot(a_ref[...], w_ref[...], preferred_element_type=F32)


def down_proj(a, w, resid, tk, tm=1024, tn=1024):
    m, k = a.shape
    n = w.shape[1]
    return pl.pallas_call(
        _down_kernel,
        grid=(m // tm, n // tn, k // tk),
        in_specs=[pl.BlockSpec((tm, tk), lambda i, j, kk: (i, kk)),
                  pl.BlockSpec((tk, tn), lambda i, j, kk: (kk, j)),
                  pl.BlockSpec((tm, tn), lambda i, j, kk: (i, j))],
        out_specs=pl.BlockSpec((tm, tn), lambda i, j, kk: (i, j)),
        out_shape=jax.ShapeDtypeStruct((m, n), F32),
        input_output_aliases={2: 0},
        compiler_params=_cparams("parallel", "arbitrary", "arbitrary"),
        name="down_proj",
    )(a, w, resid)


def _softplus(x):
    return jnp.maximum(x, 0.0) + jnp.log1p(jnp.exp(-jnp.abs(x)))


def _tril_ones(n):
    r = lax.broadcasted_iota(jnp.int32, (n, n), 0)
    c = lax.broadcasted_iota(jnp.int32, (n, n), 1)
    return r >= c


def _ssd_kernel(z_ref, x_ref, b_ref, c_ref, sm_ref, convw_ref, convb_ref, smb_ref, alog_ref,
                dskip_ref, nw_ref, expand_ref, o_ref, xpad_ref, state_ref):
    L = SSD_CHUNK
    HI = lax.Precision.HIGHEST

    @pl.when(pl.program_id(0) == 0)
    def _():
        xpad_ref[0:SUBLANES, :] = jnp.zeros((SUBLANES, xpad_ref.shape[1]), F32)
        state_ref[...] = jnp.zeros(state_ref.shape, F32)

    xpad_ref[SUBLANES:SUBLANES + L, 0:SSD_INNER] = x_ref[...]
    xpad_ref[SUBLANES:SUBLANES + L, SSD_INNER:SSD_INNER + 512] = b_ref[...]
    xpad_ref[SUBLANES:SUBLANES + L, SSD_INNER + 512:SSD_INNER + 1024] = c_ref[...]
    conv = convb_ref[...]
    for j in range(SSD_CONV):
        off = SUBLANES - (SSD_CONV - 1) + j
        conv = conv + xpad_ref[off:off + L, :] * convw_ref[j:j + 1, :]
    xpad_ref[0:SUBLANES, :] = xpad_ref[L:L + SUBLANES, :]
    xbc = _silu(conv)
    xs = xbc[:, 0:SSD_INNER]

    causal = _tril_ones(L)
    dt = _softplus(sm_ref[...] + smb_ref[...])
    da = dt * (-jnp.exp(alog_ref[...]))
    acs = jnp.dot(causal.astype(F32), da, preferred_element_type=F32, precision=HI)
    acs_t = acs.T
    a_last = acs[L - 1:L, :]
    expand = expand_ref[...]
    dt_e = jnp.dot(dt, expand, preferred_element_type=F32, precision=HI)
    eacs_e = jnp.dot(jnp.exp(acs), expand, preferred_element_type=F32, precision=HI)
    dst_e = jnp.dot(jnp.exp(a_last - acs), expand, preferred_element_type=F32, precision=HI)
    xdt = xs * dt_e
    xdec = (xdt * dst_e).astype(BF16)
    xdt_b = xdt.astype(BF16)
    chunk_decay_e = eacs_e[L - 1:L, :]

    y = xs * dskip_ref[...]
    zgate = _silu(z_ref[...])
    lane = lax.broadcasted_iota(jnp.int32, (L, LANES), 1)
    lo_half = lane < SSD_HEAD_DIM
    for g in range(SSD_GROUPS):
        gs = slice(g * SSD_GROUP_WIDTH, (g + 1) * SSD_GROUP_WIDTH)
        bg = xbc[:, SSD_INNER + g * SSD_STATE:SSD_INNER + (g + 1) * SSD_STATE]
        cg = xbc[:, SSD_INNER + 512 + g * SSD_STATE:SSD_INNER + 512 + (g + 1) * SSD_STATE]
        cg_b = cg.astype(BF16)
        cb = lax.dot_general(cg_b, bg.astype(BF16), (((1,), (1,)), ((), ())),
                             preferred_element_type=F32)
        h_enter = state_ref[g]
        y_off = jnp.dot(cg_b, h_enter.astype(BF16), preferred_element_type=F32)
        st_new = jnp.dot(bg.T.astype(BF16), xdec[:, gs], preferred_element_type=F32)
        state_ref[g] = chunk_decay_e[:, gs] * h_enter + st_new
        yd_parts = []
        for p in range(4):
            j0 = g * 8 + 2 * p
            ms = []
            for j in (j0, j0 + 1):
                seg = jnp.where(causal, acs[:, j:j + 1] - acs_t[j:j + 1, :], -jnp.inf)
                ms.append((cb * jnp.exp(seg)).astype(BF16))
            xp = xdt_b[:, g * SSD_GROUP_WIDTH + p * LANES:g * SSD_GROUP_WIDTH + (p + 1) * LANES]
            zero = jnp.zeros_like(xp)
            rhs = jnp.concatenate([jnp.where(lo_half, xp, zero), jnp.where(lo_half, zero, xp)], axis=0)
            yd_parts.append(jnp.dot(jnp.concatenate(ms, axis=1), rhs, preferred_element_type=F32))
        y_diag = jnp.concatenate(yd_parts, axis=1)
        yg = (y[:, gs] + y_diag + y_off * eacs_e[:, gs]) * zgate[:, gs]
        msq = jnp.mean(yg * yg, axis=1, keepdims=True)
        o_ref[:, gs] = (yg * lax.rsqrt(msq + NORM_EPS) * nw_ref[:, gs]).astype(o_ref.dtype)


def ssd_mixer(proj, conv_w, conv_b, small_bias, a_log, d_skip, norm_w):
    s = proj.shape[0]
    L = SSD_CHUNK
    conv_dim = conv_w.shape[1]
    alog = jnp.pad(a_log, (0, LANES - SSD_HEADS)).reshape(1, LANES)
    dskip_e = jnp.repeat(d_skip, SSD_HEAD_DIM).reshape(1, SSD_INNER)
    head = jnp.arange(LANES)[:, None]
    expand = (jnp.arange(SSD_INNER)[None, :] // SSD_HEAD_DIM == head).astype(F32)
    full = lambda shp: pl.BlockSpec(shp, lambda i: (0,) * len(shp))
    return pl.pallas_call(
        _ssd_kernel,
        grid=(s // L,),
        in_specs=[pl.BlockSpec((L, SSD_INNER), lambda i: (i, COL_Z // SSD_INNER)),
                  pl.BlockSpec((L, SSD_INNER), lambda i: (i, COL_X // SSD_INNER)),
                  pl.BlockSpec((L, 512), lambda i: (i, COL_B // 512)),
                  pl.BlockSpec((L, 512), lambda i: (i, COL_C // 512)),
                  pl.BlockSpec((L, LANES), lambda i: (i, COL_SMALL // LANES)),
                  full((SSD_CONV, conv_dim)), full((1, conv_dim)), full((1, LANES)), full((1, LANES)),
                  full((1, SSD_INNER)), full((1, SSD_INNER)), full((LANES, SSD_INNER))],
        out_specs=pl.BlockSpec((L, SSD_INNER), lambda i: (i, 0)),
        out_shape=jax.ShapeDtypeStruct((s, SSD_INNER), BF16),
        scratch_shapes=[pltpu.VMEM((L + 2 * SUBLANES, conv_dim), F32),
                        pltpu.VMEM((SSD_GROUPS, SSD_STATE, SSD_GROUP_WIDTH), F32)],
        compiler_params=_cparams("arbitrary"),
        name="ssd_mixer",
    )(proj, proj, proj, proj, proj, conv_w, conv_b.reshape(1, conv_dim), small_bias, alog,
      dskip_e, norm_w.reshape(1, SSD_INNER), expand)


def _log_sigmoid(x):
    return jnp.minimum(x, 0.0) - jnp.log1p(jnp.exp(-jnp.abs(x)))


def _mlstm_kernel(q_ref, k_ref, v_ref, o_ref, sm_ref, smb_ref, nw_ref, out_ref, c_ref, n_ref, m_ref):
    L = MLSTM_CHUNK
    HI = lax.Precision.HIGHEST

    @pl.when(pl.program_id(0) == 0)
    def _():
        c_ref[...] = jnp.zeros(c_ref.shape, F32)
        n_ref[...] = jnp.zeros(n_ref.shape, F32)
        m_ref[...] = jnp.zeros(m_ref.shape, F32)

    causal = _tril_ones(L)
    sm = sm_ref[...] + smb_ref[...]
    bcs = jnp.dot(causal.astype(F32), _log_sigmoid(sm), preferred_element_type=F32, precision=HI)
    sm_t = sm.T
    bcs_t = bcs.T
    for h in range(MLSTM_HEADS):
        li_col = sm[:, SMALL_I + h:SMALL_I + h + 1]
        b_col = bcs[:, SMALL_F + h:SMALL_F + h + 1]
        r_row = sm_t[SMALL_I + h:SMALL_I + h + 1, :] - bcs_t[SMALL_F + h:SMALL_F + h + 1, :]
        m_prev = m_ref[h:h + 1, 0:1]
        inter = b_col + m_prev
        dm = jnp.where(causal, b_col + r_row, -jnp.inf)
        m_t = jnp.maximum(inter, jnp.max(dm, axis=1, keepdims=True))
        w_intra = jnp.exp(dm - m_t)
        w_inter = jnp.exp(inter - m_t)
        qf = q_ref[:, h * MLSTM_DQK:(h + 1) * MLSTM_DQK] * (MLSTM_DQK ** -0.5)
        kf = k_ref[:, h * MLSTM_DQK:(h + 1) * MLSTM_DQK]
        qb = qf.astype(BF16)
        vb = v_ref[:, h * MLSTM_DV:(h + 1) * MLSTM_DV].astype(BF16)
        sqk = lax.dot_general(qb, kf.astype(BF16), (((1,), (1,)), ((), ())),
                              preferred_element_type=F32) * w_intra
        c_st = c_ref[h]
        n_row = n_ref[h:h + 1, :]
        num = (w_inter * jnp.dot(qb, c_st.astype(BF16), preferred_element_type=F32)
               + jnp.dot(sqk.astype(BF16), vb, preferred_element_type=F32))
        qn = jnp.sum(qf * n_row, axis=1, keepdims=True)
        den = w_inter * qn + jnp.sum(sqk, axis=1, keepdims=True)
        hout = num / jnp.maximum(jnp.abs(den), jnp.exp(-m_t))
        b_last = b_col[L - 1:L, :]
        g_col = b_last - b_col + li_col
        m_new = jnp.maximum(b_last + m_prev, jnp.max(g_col, axis=0, keepdims=True))
        decay = jnp.exp(b_last + m_prev - m_new)
        kw = kf * jnp.exp(g_col - m_new)
        c_ref[h] = decay * c_st + jnp.dot(kw.T.astype(BF16), vb, preferred_element_type=F32)
        n_ref[h:h + 1, :] = decay * n_row + jnp.sum(kw, axis=0, keepdims=True)
        m_ref[h:h + 1, :] = jnp.broadcast_to(m_new, (1, LANES))
        vs = slice(h * MLSTM_DV, (h + 1) * MLSTM_DV)
        msq = jnp.mean(hout * hout, axis=1, keepdims=True)
        hn = hout * lax.rsqrt(msq + NORM_EPS) * nw_ref[:, vs]
        out_ref[:, vs] = (jax.nn.sigmoid(o_ref[:, vs]) * hn).astype(out_ref.dtype)


def mlstm_mixer(proj, small_bias, norm_w):
    s = proj.shape[0]
    L = MLSTM_CHUNK
    qk_w = MLSTM_HEADS * MLSTM_DQK
    full = lambda shp: pl.BlockSpec(shp, lambda i: (0,) * len(shp))
    return pl.pallas_call(
        _mlstm_kernel,
        grid=(s // L,),
        in_specs=[pl.BlockSpec((L, qk_w), lambda i: (i, COL_Q // qk_w)),
                  pl.BlockSpec((L, qk_w), lambda i: (i, COL_K // qk_w)),
                  pl.BlockSpec((L, MLSTM_WIDTH), lambda i: (i, COL_V // MLSTM_WIDTH)),
                  pl.BlockSpec((L, MLSTM_WIDTH), lambda i: (i, COL_O // MLSTM_WIDTH)),
                  pl.BlockSpec((L, LANES), lambda i: (i, COL_SMALL // LANES)),
                  full((1, LANES)), full((1, MLSTM_WIDTH))],
        out_specs=pl.BlockSpec((L, MLSTM_WIDTH), lambda i: (i, 0)),
        out_shape=jax.ShapeDtypeStruct((s, MLSTM_WIDTH), BF16),
        scratch_shapes=[pltpu.VMEM((MLSTM_HEADS, MLSTM_DQK, MLSTM_DV), F32),
                        pltpu.VMEM((SUBLANES, MLSTM_DQK), F32),
                        pltpu.VMEM((SUBLANES, LANES), F32)],
        compiler_params=_cparams("arbitrary"),
        name="mlstm_mixer",
    )(proj, proj, proj, proj, proj, small_bias, norm_w.reshape(1, MLSTM_WIDTH))


def _permute_w_in(w_in):
    o_z, o_xbc, o_dt = 0, 2048, 5120
    o_q, o_k, o_v, o_o, o_i, o_f = 5152, 6176, 7200, 9248, 11296, 11300
    cols = lambda a, n: w_in[:, a:a + n]
    parts = [cols(o_z, 2048), cols(o_xbc, 2048), cols(o_v, 2048), cols(o_o, 2048),
             cols(o_q, 1024), cols(o_k, 1024), cols(o_xbc + 2048, 512), cols(o_xbc + 2560, 512),
             cols(o_dt, 32), cols(o_i, 4), cols(o_f, 4)]
    used = COL_SMALL + 40
    parts.append(jnp.zeros((w_in.shape[0], D_PROJ - used), w_in.dtype))
    return jnp.concatenate(parts, axis=1).astype(BF16)


def _small_bias(dt_bias, i_bias, f_bias):
    v = jnp.concatenate([dt_bias, i_bias, f_bias, jnp.zeros((LANES - 40,), F32)])
    return v.reshape(1, LANES)


def kernel(x, norm_mix_w, w_in, conv_w, conv_b, dt_bias, a_log, d_skip, ssd_norm_w, mlstm_i_bias,
           mlstm_f_bias, mlstm_norm_w, w_out, norm_ffn_w, ffn_w_gate, ffn_w_up, ffn_w_down, router_w,
           moe_w_gate, moe_w_up, moe_w_down, final_norm_w):
    bsz, s, d = x.shape
    depth = w_in.shape[0]
    outs = []
    for b in range(bsz):
        h = x[b]
        for layer in range(depth):
            u = rmsnorm(h, norm_mix_w[layer], BF16)
            proj = matmul(u, _permute_w_in(w_in[layer]), F32, tm=1024, tn=PROJ_TN)
            sbias = _small_bias(dt_bias[layer], mlstm_i_bias[layer], mlstm_f_bias[layer])
            y_ssd = ssd_mixer(proj, conv_w[layer], conv_b[layer], sbias, a_log[layer], d_skip[layer],
                              ssd_norm_w[layer])
            y_ml = mlstm_mixer(proj, sbias, mlstm_norm_w[layer])
            h = out_proj(y_ssd, y_ml, w_out[layer].astype(BF16), h)
            j = layer // 2
            if layer % 2 == 0:
                u = rmsnorm(h, norm_ffn_w[layer], BF16)
                a = swiglu_up(u, ffn_w_gate[j].astype(BF16), ffn_w_up[j].astype(BF16))
                h = down_proj(a, ffn_w_down[j].astype(BF16), h, tk=2048)
            else:
                h = moe_swiglu(h, norm_ffn_w[layer], router_w[j], moe_w_gate[j].astype(BF16),
                               moe_w_up[j].astype(BF16), moe_w_down[j].astype(BF16))
        outs.append(rmsnorm(h, final_norm_w, F32))
    return jnp.stack(outs, axis=0)
```

```python
import functools

import jax
import jax.numpy as jnp
from jax import lax
from jax.experimental import pallas as pl
from jax.experimental.pallas import tpu as pltpu

F32 = jnp.float32
BF16 = jnp.bfloat16

D_MODEL = 4096
SSD_INNER = 2048
SSD_HEAD_DIM = 64
SSD_HEADS = 32
SSD_GROUPS = 4
SSD_STATE = 128
SSD_CONV = 4
SSD_CHUNK = 128
SSD_GROUP_WIDTH = SSD_INNER // SSD_GROUPS
MLSTM_WIDTH = 2048
MLSTM_HEADS = 4
MLSTM_DV = 512
MLSTM_DQK = 256
FF_DENSE = 14336
N_EXPERTS = 8
FF_EXPERT = 5632
NORM_EPS = 1e-6

LANES = 128
SUBLANES = 8
VMEM_LIMIT_BYTES = 56 * 1024 * 1024

COL_Z, COL_X, COL_V, COL_O = 0, 2048, 4096, 6144
COL_Q, COL_K = 8192, 9216
COL_B, COL_C = 10240, 10752
COL_SMALL = 11264
SMALL_DT, SMALL_I, SMALL_F = 0, 32, 36
D_PROJ = 11520
PROJ_TN = 1280

MLSTM_CHUNK = 128


def _cparams(*sem):
    return pltpu.CompilerParams(dimension_semantics=sem, vmem_limit_bytes=VMEM_LIMIT_BYTES)


def _rmsnorm_kernel(x_ref, w_ref, o_ref):
    x = x_ref[...]
    ms = jnp.mean(x * x, axis=-1, keepdims=True)
    o_ref[...] = (x * lax.rsqrt(ms + NORM_EPS) * w_ref[...]).astype(o_ref.dtype)


def rmsnorm(x, w, out_dtype, tr=512):
    s, d = x.shape
    return pl.pallas_call(
        _rmsnorm_kernel,
        grid=(s // tr,),
        in_specs=[pl.BlockSpec((tr, d), lambda i: (i, 0)),
                  pl.BlockSpec((1, d), lambda i: (0, 0))],
        out_specs=pl.BlockSpec((tr, d), lambda i: (i, 0)),
        out_shape=jax.ShapeDtypeStruct((s, d), out_dtype),
        compiler_params=_cparams("parallel"),
        name="rmsnorm",
    )(x, w.reshape(1, d))


RT_SEL = 0
RT_C1, RT_C2 = 8, 9
RT_I1, RT_I2 = 10, 11


def _router_kernel(x_ref, w_ref, rw_ref, rt_ref):
    x = x_ref[...]
    ms = jnp.mean(x * x, axis=-1, keepdims=True)
    u = x * lax.rsqrt(ms + NORM_EPS) * w_ref[...]
    logits = jnp.dot(u, rw_ref[...], preferred_element_type=F32, precision=lax.Precision.HIGHEST)
    lane = lax.broadcasted_iota(jnp.int32, logits.shape, 1)
    valid = lane < N_EXPERTS
    lg = jnp.where(valid, logits, -jnp.inf)
    ex = jnp.exp(lg - jnp.max(lg, axis=1, keepdims=True))
    probs = ex / jnp.sum(ex, axis=1, keepdims=True)
    probs = jnp.where(valid, probs, -1.0)
    p1 = jnp.max(probs, axis=1, keepdims=True)
    i1 = jnp.min(jnp.where(probs == p1, lane, LANES), axis=1, keepdims=True)
    rest = jnp.where(lane == i1, -1.0, probs)
    p2 = jnp.max(rest, axis=1, keepdims=True)
    i2 = jnp.min(jnp.where(rest == p2, lane, LANES), axis=1, keepdims=True)
    tot = p1 + p2
    rt = jnp.where((lane == i1) | (lane == i2), 1.0, 0.0)
    rt = jnp.where(lane == RT_C1, p1 / tot, rt)
    rt = jnp.where(lane == RT_C2, p2 / tot, rt)
    rt = jnp.where(lane == RT_I1, i1.astype(F32), rt)
    rt = jnp.where(lane == RT_I2, i2.astype(F32), rt)
    rt_ref[...] = rt


def router(x, w, router_w, tr=512):
    s, d = x.shape
    rw = jnp.pad(router_w, ((0, 0), (0, LANES - N_EXPERTS)))
    return pl.pallas_call(
        _router_kernel,
        grid=(s // tr,),
        in_specs=[pl.BlockSpec((tr, d), lambda i: (i, 0)),
                  pl.BlockSpec((1, d), lambda i: (0, 0)),
                  pl.BlockSpec((d, LANES), lambda i: (0, 0))],
        out_specs=pl.BlockSpec((tr, LANES), lambda i: (i, 0)),
        out_shape=jax.ShapeDtypeStruct((s, LANES), F32),
        compiler_params=_cparams("parallel"),
        name="router",
    )(x, w.reshape(1, d), rw)


MOE_TM = 512
PLAN_BLK = 512


def _plan_kernel(rt_ref, dest_ref, meta_ref):
    s = rt_ref.shape[0]
    nblk = s // PLAN_BLK
    lane = lax.broadcasted_iota(jnp.int32, (PLAN_BLK, LANES), 1)
    is_exp = lane < N_EXPERTS
    r = lax.broadcasted_iota(jnp.int32, (PLAN_BLK, PLAN_BLK), 0)
    c = lax.broadcasted_iota(jnp.int32, (PLAN_BLK, PLAN_BLK), 1)
    strict_tril = (r > c).astype(BF16)

    def rank_body(b, carry):
        rows = pl.ds(pl.multiple_of(b * PLAN_BLK, PLAN_BLK), PLAN_BLK)
        sel = jnp.where(is_exp, rt_ref[rows, :], 0.0)
        rank = jnp.dot(strict_tril, sel.astype(BF16), preferred_element_type=F32) + carry
        dest_ref[rows, :] = rank.astype(jnp.int32)
        return carry + jnp.sum(sel, axis=0, keepdims=True)

    counts = lax.fori_loop(0, nblk, rank_body, jnp.zeros((1, LANES), F32))
    tiles = jnp.floor((counts + (MOE_TM - 1)) * (1.0 / MOE_TM))
    er = lax.broadcasted_iota(jnp.int32, (LANES, LANES), 0)
    ec = lax.broadcasted_iota(jnp.int32, (LANES, LANES), 1)
    before = (er < ec).astype(F32)
    tile_start = jnp.dot(jnp.broadcast_to(tiles, (SUBLANES, LANES)), before, preferred_element_type=F32,
                         precision=lax.Precision.HIGHEST)[0:1, :]
    row_off = tile_start * MOE_TM
    meta_ref[0:1, :] = counts.astype(jnp.int32)
    meta_ref[1:2, :] = tiles.astype(jnp.int32)
    meta_ref[2:3, :] = tile_start.astype(jnp.int32)
    meta_ref[3:SUBLANES, :] = jnp.zeros((SUBLANES - 3, LANES), jnp.int32)

    def dest_body(b, _):
        rows = pl.ds(pl.multiple_of(b * PLAN_BLK, PLAN_BLK), PLAN_BLK)
        rt = rt_ref[rows, :]
        pos = dest_ref[rows, :].astype(F32) + row_off
        i1 = rt[:, RT_I1:RT_I1 + 1].astype(jnp.int32)
        i2 = rt[:, RT_I2:RT_I2 + 1].astype(jnp.int32)
        d1 = jnp.sum(jnp.where(lane == i1, pos, 0.0), axis=1, keepdims=True)
        d2 = jnp.sum(jnp.where(lane == i2, pos, 0.0), axis=1, keepdims=True)
        dest_ref[rows, :] = jnp.where(lane == 0, d1, jnp.where(lane == 1, d2, 0.0)).astype(jnp.int32)
        return 0

    lax.fori_loop(0, nblk, dest_body, 0)


def moe_plan(rt):
    s = rt.shape[0]
    return pl.pallas_call(
        _plan_kernel,
        out_shape=[jax.ShapeDtypeStruct((s, LANES), jnp.int32),
                   jax.ShapeDtypeStruct((SUBLANES, LANES), jnp.int32)],
        compiler_params=pltpu.CompilerParams(vmem_limit_bytes=VMEM_LIMIT_BYTES),
        name="moe_plan",
    )(rt)


def _invert_kernel(d1_ref, d2_ref, src_ref):
    n_rows = src_ref.shape[0]
    n_tok = d1_ref.shape[0]

    def zero(i, _):
        src_ref[i] = 0
        return 0

    lax.fori_loop(0, n_rows, zero, 0, unroll=16)

    def put(t, _):
        src_ref[d1_ref[t]] = t
        src_ref[d2_ref[t]] = t
        return 0

    lax.fori_loop(0, n_tok, put, 0, unroll=8)


def moe_invert(d1, d2, n_rows):
    smem = pl.BlockSpec(memory_space=pltpu.SMEM)
    return pl.pallas_call(
        _invert_kernel,
        in_specs=[smem, smem],
        out_specs=smem,
        out_shape=jax.ShapeDtypeStruct((n_rows,), jnp.int32),
        name="moe_invert",
    )(d1, d2)


GATHER_TM = 256
GATHER_UNROLL = 8


def _gather_norm_kernel(src_ref, nt_ref, h_hbm, w_ref, o_ref, buf_ref, sem):
    i = pl.program_id(0)
    nt = nt_ref[0]

    def row_copy(tile, slot, r):
        return pltpu.make_async_copy(h_hbm.at[pl.ds(src_ref[tile * MOE_TM + r], 1), :],
                                     buf_ref.at[slot, pl.ds(r, 1), :], sem.at[slot])

    def issue(tile, slot):
        def body(r, _):
            row_copy(tile, slot, r).start()
            return 0
        lax.fori_loop(0, MOE_TM, body, 0, unroll=GATHER_UNROLL)

    @pl.when(i == 0)
    def _():
        issue(0, 0)

    @pl.when(i + 1 < nt)
    def _():
        issue(i + 1, (i + 1) % 2)

    @pl.when(i < nt)
    def _():
        slot = i % 2

        def body(r, _):
            row_copy(i, slot, r).wait()
            return 0
        lax.fori_loop(0, MOE_TM, body, 0, unroll=GATHER_UNROLL)
        x = buf_ref[slot]
        ms = jnp.mean(x * x, axis=-1, keepdims=True)
        o_ref[...] = (x * lax.rsqrt(ms + NORM_EPS) * w_ref[...]).astype(o_ref.dtype)

    @pl.when(i >= nt)
    def _():
        o_ref[...] = jnp.zeros(o_ref.shape, o_ref.dtype)


def moe_gather_norm(src, nt, h, w):
    n_rows = src.shape[0]
    d = h.shape[1]
    return pl.pallas_call(
        _gather_norm_kernel,
        grid_spec=pltpu.PrefetchScalarGridSpec(
            num_scalar_prefetch=2,
            grid=(n_rows // MOE_TM,),
            in_specs=[pl.BlockSpec(memory_space=pl.ANY),
                      pl.BlockSpec((1, d), lambda i, src, nt: (0, 0))],
            out_specs=pl.BlockSpec((MOE_TM, d), lambda i, src, nt: (i, 0)),
            scratch_shapes=[pltpu.VMEM((2, MOE_TM, d), F32), pltpu.SemaphoreType.DMA((2,))]),
        out_shape=jax.ShapeDtypeStruct((n_rows, d), BF16),
        compiler_params=_cparams("arbitrary"),
        name="moe_gather_norm",
    )(src, nt, h, w.reshape(1, d))


def _combine_kernel(d1_ref, d2_ref, h_ref, rt_ref, y_hbm, o_ref, ya_ref, yb_ref, sem):
    i = pl.program_id(0)
    n = pl.num_programs(0)

    def copy_a(blk, slot, r):
        return pltpu.make_async_copy(y_hbm.at[pl.ds(d1_ref[blk * GATHER_TM + r], 1), :],
                                     ya_ref.at[slot, pl.ds(r, 1), :], sem.at[0, slot])

    def copy_b(blk, slot, r):
        return pltpu.make_async_copy(y_hbm.at[pl.ds(d2_ref[blk * GATHER_TM + r], 1), :],
                                     yb_ref.at[slot, pl.ds(r, 1), :], sem.at[1, slot])

    def issue(blk, slot):
        def body(r, _):
            copy_a(blk, slot, r).start()
            copy_b(blk, slot, r).start()
            return 0
        lax.fori_loop(0, GATHER_TM, body, 0, unroll=GATHER_UNROLL)

    @pl.when(i == 0)
    def _():
        issue(0, 0)

    @pl.when(i + 1 < n)
    def _():
        issue(i + 1, (i + 1) % 2)

    slot = i % 2

    def wait(r, _):
        copy_a(i, slot, r).wait()
        copy_b(i, slot, r).wait()
        return 0
    lax.fori_loop(0, GATHER_TM, wait, 0, unroll=GATHER_UNROLL)
    rt = rt_ref[...]
    o_ref[...] = (h_ref[...] + rt[:, RT_C1:RT_C1 + 1] * ya_ref[slot] + rt[:, RT_C2:RT_C2 + 1] * yb_ref[slot])


def moe_combine(d1, d2, h, rt, y):
    s, d = h.shape
    return pl.pallas_call(
        _combine_kernel,
        grid_spec=pltpu.PrefetchScalarGridSpec(
            num_scalar_prefetch=2,
            grid=(s // GATHER_TM,),
            in_specs=[pl.BlockSpec((GATHER_TM, d), lambda i, a, b: (i, 0)),
                      pl.BlockSpec((GATHER_TM, LANES), lambda i, a, b: (i, 0)),
                      pl.BlockSpec(memory_space=pl.ANY)],
            out_specs=pl.BlockSpec((GATHER_TM, d), lambda i, a, b: (i, 0)),
            scratch_shapes=[pltpu.VMEM((2, GATHER_TM, d), F32), pltpu.VMEM((2, GATHER_TM, d), F32),
                            pltpu.SemaphoreType.DMA((2, 2))]),
        out_shape=jax.ShapeDtypeStruct((s, d), F32),
        compiler_params=_cparams("arbitrary"),
        name="moe_combine",
    )(d1, d2, h, rt, y)


def _mm_kernel(x_ref, w_ref, o_ref):
    o_ref[...] = jnp.dot(x_ref[...], w_ref[...], preferred_element_type=F32).astype(o_ref.dtype)


def matmul(x, w, out_dtype, tm, tn):
    m, k = x.shape
    n = w.shape[1]
    return pl.pallas_call(
        _mm_kernel,
        grid=(m // tm, n // tn),
        in_specs=[pl.BlockSpec((tm, k), lambda i, j: (i, 0)),
                  pl.BlockSpec((k, tn), lambda i, j: (0, j))],
        out_specs=pl.BlockSpec((tm, tn), lambda i, j: (i, j)),
        out_shape=jax.ShapeDtypeStruct((m, n), out_dtype),
        compiler_params=_cparams("parallel", "arbitrary"),
        name="in_proj",
    )(x, w)


def _out_proj_kernel(ya_ref, yb_ref, wa_ref, wb_ref, r_ref, o_ref):
    acc = jnp.dot(ya_ref[...], wa_ref[...], preferred_element_type=F32)
    acc += jnp.dot(yb_ref[...], wb_ref[...], preferred_element_type=F32)
    o_ref[...] = r_ref[...] + acc


def out_proj(y_ssd, y_ml, w_out, resid, tm=1024, tn=1024):
    m, ka = y_ssd.shape
    n = w_out.shape[1]
    return pl.pallas_call(
        _out_proj_kernel,
        grid=(m // tm, n // tn),
        in_specs=[pl.BlockSpec((tm, ka), lambda i, j: (i, 0)),
                  pl.BlockSpec((tm, ka), lambda i, j: (i, 0)),
                  pl.BlockSpec((ka, tn), lambda i, j: (0, j)),
                  pl.BlockSpec((ka, tn), lambda i, j: (1, j)),
                  pl.BlockSpec((tm, tn), lambda i, j: (i, j))],
        out_specs=pl.BlockSpec((tm, tn), lambda i, j: (i, j)),
        out_shape=jax.ShapeDtypeStruct((m, n), F32),
        compiler_params=_cparams("parallel", "arbitrary"),
        name="out_proj",
    )(y_ssd, y_ml, w_out, w_out, resid)


def _silu(x):
    return x * jax.nn.sigmoid(x)


def _swiglu_up_kernel(x_ref, wg_ref, wu_ref, o_ref):
    x = x_ref[...]
    g = jnp.dot(x, wg_ref[...], preferred_element_type=F32)
    u = jnp.dot(x, wu_ref[...], preferred_element_type=F32)
    o_ref[...] = (_silu(g) * u).astype(o_ref.dtype)


def swiglu_up(x, wg, wu, tm=1024, tf=512):
    m, k = x.shape
    f = wg.shape[1]
    return pl.pallas_call(
        _swiglu_up_kernel,
        grid=(m // tm, f // tf),
        in_specs=[pl.BlockSpec((tm, k), lambda i, j: (i, 0)),
                  pl.BlockSpec((k, tf), lambda i, j: (0, j)),
                  pl.BlockSpec((k, tf), lambda i, j: (0, j))],
        out_specs=pl.BlockSpec((tm, tf), lambda i, j: (i, j)),
        out_shape=jax.ShapeDtypeStruct((m, f), BF16),
        compiler_params=_cparams("parallel", "arbitrary"),
        name="swiglu_up",
    )(x, wg, wu)


def _moe_up_kernel(eot_ref, nt_ref, x_ref, wg_ref, wu_ref, o_ref, wgb_ref, wub_ref):
    t = pl.program_id(1)

    @pl.when(t < nt_ref[0])
    def _():
        @pl.when((t == 0) | (eot_ref[t] != eot_ref[jnp.maximum(t - 1, 0)]))
        def _():
            wgb_ref[...] = wg_ref[...].astype(BF16)
            wub_ref[...] = wu_ref[...].astype(BF16)

        x = x_ref[...]
        g = jnp.dot(x, wgb_ref[...], preferred_element_type=F32)
        u = jnp.dot(x, wub_ref[...], preferred_element_type=F32)
        o_ref[...] = (_silu(g) * u).astype(o_ref.dtype)

    @pl.when(pl.program_id(1) >= nt_ref[0])
    def _():
        o_ref[...] = jnp.zeros(o_ref.shape, o_ref.dtype)


def moe_up(eot, nt, xs, wg, wu, tf=512):
    r, k = xs.shape
    f = wg.shape[2]
    row = lambda j, t, eot, nt: (jnp.minimum(t, nt[0] - 1), 0)
    wmap = lambda j, t, eot, nt: (eot[t], 0, j)
    return pl.pallas_call(
        _moe_up_kernel,
        grid_spec=pltpu.PrefetchScalarGridSpec(
            num_scalar_prefetch=2,
            grid=(f // tf, r // MOE_TM),
            in_specs=[pl.BlockSpec((MOE_TM, k), row),
                      pl.BlockSpec((None, k, tf), wmap),
                      pl.BlockSpec((None, k, tf), wmap)],
            out_specs=pl.BlockSpec((MOE_TM, tf), lambda j, t, eot, nt: (t, j)),
            scratch_shapes=[pltpu.VMEM((k, tf), BF16), pltpu.VMEM((k, tf), BF16)]),
        out_shape=jax.ShapeDtypeStruct((r, f), BF16),
        compiler_params=_cparams("arbitrary", "arbitrary"),
        name="moe_up",
    )(eot, nt, xs, wg, wu)


def _moe_down_kernel(eot_ref, nt_ref, a_ref, w_ref, o_ref):
    @pl.when(pl.program_id(1) < nt_ref[0])
    def _():
        o_ref[...] = jnp.dot(a_ref[...], w_ref[...], preferred_element_type=F32)

    @pl.when(pl.program_id(1) >= nt_ref[0])
    def _():
        o_ref[...] = jnp.zeros(o_ref.shape, o_ref.dtype)


def moe_down(eot, nt, a, wd, tn=1024):
    r, f = a.shape
    n = wd.shape[2]
    return pl.pallas_call(
        _moe_down_kernel,
        grid_spec=pltpu.PrefetchScalarGridSpec(
            num_scalar_prefetch=2,
            grid=(n // tn, r // MOE_TM),
            in_specs=[pl.BlockSpec((MOE_TM, f), lambda j, t, eot, nt: (jnp.minimum(t, nt[0] - 1), 0)),
                      pl.BlockSpec((None, f, tn), lambda j, t, eot, nt: (eot[t], 0, j))],
            out_specs=pl.BlockSpec((MOE_TM, tn), lambda j, t, eot, nt: (t, j))),
        out_shape=jax.ShapeDtypeStruct((r, n), F32),
        compiler_params=_cparams("arbitrary", "arbitrary"),
        name="moe_down",
    )(eot, nt, a, wd)


def moe_swiglu(h, norm_w, router_w, wg, wu, wd):
    s, d = h.shape
    n_tiles_max = (2 * s) // MOE_TM + N_EXPERTS
    n_rows = n_tiles_max * MOE_TM
    rt = router(h, norm_w, router_w)
    dest, meta = moe_plan(rt)
    d1, d2 = dest[:, 0], dest[:, 1]
    tiles, tile_start = meta[1, :N_EXPERTS], meta[2, :N_EXPERTS]
    nt = jnp.sum(tiles).reshape(1)
    tile_end = tile_start + tiles
    tid = jnp.minimum(jnp.arange(n_tiles_max, dtype=jnp.int32), nt - 1)
    eot = jnp.sum((tid[:, None] >= tile_end[None, :]).astype(jnp.int32), axis=1)
    src = moe_invert(d1, d2, n_rows)
    xs = moe_gather_norm(src, nt, h, norm_w)
    a = moe_up(eot, nt, xs, wg, wu)
    y = moe_down(eot, nt, a, wd)
    return moe_combine(d1, d2, h, rt, y)


def _down_kernel(a_ref, w_ref, r_ref, o_ref):
    kk = pl.program_id(2)

    @pl.when(kk == 0)
    def _():
        o_ref[...] = r_ref[...]

    o_ref[...] += jnp.dot(a_ref[...], w_ref[...], preferred_element_type=F32)


def down_proj(a, w, resid, tk, tm=1024, tn=1024):
    m, k = a.shape
    n = w.shape[1]
    return pl.pallas_call(
        _down_kernel,
        grid=(m // tm, n // tn, k // tk),
        in_specs=[pl.BlockSpec((tm, tk), lambda i, j, kk: (i, kk)),
                  pl.BlockSpec((tk, tn), lambda i, j, kk: (kk, j)),
                  pl.BlockSpec((tm, tn), lambda i, j, kk: (i, j))],
        out_specs=pl.BlockSpec((tm, tn), lambda i, j, kk: (i, j)),
        out_shape=jax.ShapeDtypeStruct((m, n), F32),
        compiler_params=_cparams("parallel", "arbitrary", "arbitrary"),
        name="down_proj",
    )(a, w, resid)


def _softplus(x):
    return jnp.maximum(x, 0.0) + jnp.log1p(jnp.exp(-jnp.abs(x)))


def _tril_ones(n):
    r = lax.broadcasted_iota(jnp.int32, (n, n), 0)
    c = lax.broadcasted_iota(jnp.int32, (n, n), 1)
    return r >= c


def _ssd_kernel(z_ref, x_ref, b_ref, c_ref, sm_ref, convw_ref, convb_ref, smb_ref, alog_ref,
                dskip_ref, nw_ref, expand_ref, o_ref, tail_ref, state_ref):
    L = SSD_CHUNK
    HI = lax.Precision.HIGHEST

    @pl.when(pl.program_id(0) == 0)
    def _():
        tail_ref[...] = jnp.zeros(tail_ref.shape, F32)
        state_ref[...] = jnp.zeros(state_ref.shape, F32)

    row8 = lax.broadcasted_iota(jnp.int32, (SUBLANES, LANES), 0)

    def conv_silu(u, col0):
        w = u.shape[1]
        cols = slice(col0, col0 + w)
        tail = tail_ref[:, cols]
        acc = convb_ref[:, cols] + u * convw_ref[SSD_CONV - 1:SSD_CONV, cols]
        for k in range(1, SSD_CONV):
            rolled = pltpu.roll(u, k, axis=0)
            fix = jnp.where(jnp.tile(row8, (1, w // LANES)) < k, pltpu.roll(tail, k, axis=0), rolled[0:SUBLANES])
            shifted = jnp.concatenate([fix, rolled[SUBLANES:]], axis=0)
            acc = acc + shifted * convw_ref[SSD_CONV - 1 - k:SSD_CONV - k, cols]
        tail_ref[:, cols] = u[L - SUBLANES:L]
        return _silu(acc)

    xs = conv_silu(x_ref[...], 0)
    bmat = conv_silu(b_ref[...], SSD_INNER)
    cmat = conv_silu(c_ref[...], SSD_INNER + SSD_GROUPS * SSD_STATE)

    lane = lax.broadcasted_iota(jnp.int32, (L, LANES), 1)

    def expand_heads(v):
        hi = v.astype(BF16).astype(F32)
        r1 = v - hi
        mid = r1.astype(BF16).astype(F32)
        lo = r1 - mid
        packed = jnp.where(lane < 32, hi,
                           jnp.where(lane < 64, pltpu.roll(mid, 32, axis=1),
                                     jnp.where(lane < 96, pltpu.roll(lo, 64, axis=1), 0.0)))
        return jnp.dot(packed.astype(BF16), expand_ref[...], preferred_element_type=F32)

    causal = _tril_ones(L)
    dt = jnp.where(lane < SSD_HEADS, _softplus(sm_ref[...] + smb_ref[...]), 0.0)
    da = dt * (-jnp.exp(alog_ref[...]))
    acs = jnp.dot(causal.astype(F32), da, preferred_element_type=F32, precision=HI)
    acs_t = acs.T
    a_last = acs[L - 1:L, :]
    dt_e = expand_heads(dt)
    eacs_e = expand_heads(jnp.exp(acs))
    dst_e = expand_heads(jnp.exp(a_last - acs))
    xdt = xs * dt_e
    xdec = (xdt * dst_e).astype(BF16)
    xdt_b = xdt.astype(BF16)
    chunk_decay_e = eacs_e[L - 1:L, :]

    y = xs * dskip_ref[...]
    zgate = _silu(z_ref[...])
    lo_half = lane < SSD_HEAD_DIM
    for g in range(SSD_GROUPS):
        gs = slice(g * SSD_GROUP_WIDTH, (g + 1) * SSD_GROUP_WIDTH)
        bg = bmat[:, g * SSD_STATE:(g + 1) * SSD_STATE]
        cg = cmat[:, g * SSD_STATE:(g + 1) * SSD_STATE]
        cg_b = cg.astype(BF16)
        cb = lax.dot_general(cg_b, bg.astype(BF16), (((1,), (1,)), ((), ())),
                             preferred_element_type=F32)
        h_enter = state_ref[g]
        y_off = jnp.dot(cg_b, h_enter.astype(BF16), preferred_element_type=F32)
        st_new = jnp.dot(bg.T.astype(BF16), xdec[:, gs], preferred_element_type=F32)
        state_ref[g] = chunk_decay_e[:, gs] * h_enter + st_new
        yd_parts = []
        for p in range(4):
            j0 = g * 8 + 2 * p
            ms = []
            for j in (j0, j0 + 1):
                seg = jnp.where(causal, acs[:, j:j + 1] - acs_t[j:j + 1, :], -jnp.inf)
                ms.append((cb * jnp.exp(seg)).astype(BF16))
            xp = xdt_b[:, g * SSD_GROUP_WIDTH + p * LANES:g * SSD_GROUP_WIDTH + (p + 1) * LANES]
            zero = jnp.zeros_like(xp)
            rhs = jnp.concatenate([jnp.where(lo_half, xp, zero), jnp.where(lo_half, zero, xp)], axis=0)
            yd_parts.append(jnp.dot(jnp.concatenate(ms, axis=1), rhs, preferred_element_type=F32))
        y_diag = jnp.concatenate(yd_parts, axis=1)
        yg = (y[:, gs] + y_diag + y_off * eacs_e[:, gs]) * zgate[:, gs]
        msq = jnp.mean(yg * yg, axis=1, keepdims=True)
        o_ref[:, gs] = (yg * lax.rsqrt(msq + NORM_EPS) * nw_ref[:, gs]).astype(o_ref.dtype)


def ssd_mixer(proj, conv_w, conv_b, small_bias, a_log, d_skip, norm_w):
    s = proj.shape[0]
    L = SSD_CHUNK
    conv_dim = conv_w.shape[1]
    alog = jnp.pad(a_log, (0, LANES - SSD_HEADS)).reshape(1, LANES)
    dskip_e = jnp.repeat(d_skip, SSD_HEAD_DIM).reshape(1, SSD_INNER)
    piece_row = jnp.arange(LANES)[:, None]
    expand = ((jnp.arange(SSD_INNER)[None, :] // SSD_HEAD_DIM == piece_row % SSD_HEADS)
              & (piece_row < 3 * SSD_HEADS)).astype(BF16)
    full = lambda shp: pl.BlockSpec(shp, lambda i: (0,) * len(shp))
    return pl.pallas_call(
        _ssd_kernel,
        grid=(s // L,),
        in_specs=[pl.BlockSpec((L, SSD_INNER), lambda i: (i, COL_Z // SSD_INNER)),
                  pl.BlockSpec((L, SSD_INNER), lambda i: (i, COL_X // SSD_INNER)),
                  pl.BlockSpec((L, 512), lambda i: (i, COL_B // 512)),
                  pl.BlockSpec((L, 512), lambda i: (i, COL_C // 512)),
                  pl.BlockSpec((L, LANES), lambda i: (i, COL_SMALL // LANES)),
                  full((SSD_CONV, conv_dim)), full((1, conv_dim)), full((1, LANES)), full((1, LANES)),
                  full((1, SSD_INNER)), full((1, SSD_INNER)), full((LANES, SSD_INNER))],
        out_specs=pl.BlockSpec((L, SSD_INNER), lambda i: (i, 0)),
        out_shape=jax.ShapeDtypeStruct((s, SSD_INNER), BF16),
        scratch_shapes=[pltpu.VMEM((SUBLANES, conv_dim), F32),
                        pltpu.VMEM((SSD_GROUPS, SSD_STATE, SSD_GROUP_WIDTH), F32)],
        compiler_params=_cparams("arbitrary"),
        name="ssd_mixer",
    )(proj, proj, proj, proj, proj, conv_w, conv_b.reshape(1, conv_dim), small_bias, alog,
      dskip_e, norm_w.reshape(1, SSD_INNER), expand)


def _log_sigmoid(x):
    return jnp.minimum(x, 0.0) - jnp.log1p(jnp.exp(-jnp.abs(x)))


def _mlstm_kernel(q_ref, k_ref, v_ref, o_ref, sm_ref, smb_ref, nw_ref, out_ref, c_ref, n_ref, m_ref):
    L = MLSTM_CHUNK
    HI = lax.Precision.HIGHEST

    @pl.when(pl.program_id(0) == 0)
    def _():
        c_ref[...] = jnp.zeros(c_ref.shape, F32)
        n_ref[...] = jnp.zeros(n_ref.shape, F32)
        m_ref[...] = jnp.zeros(m_ref.shape, F32)

    causal = _tril_ones(L)
    sm = sm_ref[...] + smb_ref[...]
    bcs = jnp.dot(causal.astype(F32), _log_sigmoid(sm), preferred_element_type=F32, precision=HI)
    sm_t = sm.T
    bcs_t = bcs.T
    for h in range(MLSTM_HEADS):
        li_col = sm[:, SMALL_I + h:SMALL_I + h + 1]
        b_col = bcs[:, SMALL_F + h:SMALL_F + h + 1]
        r_row = sm_t[SMALL_I + h:SMALL_I + h + 1, :] - bcs_t[SMALL_F + h:SMALL_F + h + 1, :]
        m_prev = m_ref[h:h + 1, 0:1]
        inter = b_col + m_prev
        dm = jnp.where(causal, b_col + r_row, -jnp.inf)
        m_t = jnp.maximum(inter, jnp.max(dm, axis=1, keepdims=True))
        w_intra = jnp.exp(dm - m_t)
        w_inter = jnp.exp(inter - m_t)
        qf = q_ref[:, h * MLSTM_DQK:(h + 1) * MLSTM_DQK] * (MLSTM_DQK ** -0.5)
        kf = k_ref[:, h * MLSTM_DQK:(h + 1) * MLSTM_DQK]
        qb = qf.astype(BF16)
        vb = v_ref[:, h * MLSTM_DV:(h + 1) * MLSTM_DV].astype(BF16)
        sqk = lax.dot_general(qb, kf.astype(BF16), (((1,), (1,)), ((), ())),
                              preferred_element_type=F32) * w_intra
        c_st = c_ref[h]
        n_row = n_ref[h:h + 1, :]
        num = (w_inter * jnp.dot(qb, c_st.astype(BF16), preferred_element_type=F32)
               + jnp.dot(sqk.astype(BF16), vb, preferred_element_type=F32))
        qn = jnp.sum(qf * n_row, axis=1, keepdims=True)
        den = w_inter * qn + jnp.sum(sqk, axis=1, keepdims=True)
        hout = num / jnp.maximum(jnp.abs(den), jnp.exp(-m_t))
        b_last = b_col[L - 1:L, :]
        g_col = b_last - b_col + li_col
        m_new = jnp.maximum(b_last + m_prev, jnp.max(g_col, axis=0, keepdims=True))
        decay = jnp.exp(b_last + m_prev - m_new)
        kw = kf * jnp.exp(g_col - m_new)
        c_ref[h] = decay * c_st + jnp.dot(kw.T.astype(BF16), vb, preferred_element_type=F32)
        n_ref[h:h + 1, :] = decay * n_row + jnp.sum(kw, axis=0, keepdims=True)
        m_ref[h:h + 1, :] = jnp.broadcast_to(m_new, (1, LANES))
        vs = slice(h * MLSTM_DV, (h + 1) * MLSTM_DV)
        msq = jnp.mean(hout * hout, axis=1, keepdims=True)
        hn = hout * lax.rsqrt(msq + NORM_EPS) * nw_ref[:, vs]
        out_ref[:, vs] = (jax.nn.sigmoid(o_ref[:, vs]) * hn).astype(out_ref.dtype)


def mlstm_mixer(proj, small_bias, norm_w):
    s = proj.shape[0]
    L = MLSTM_CHUNK
    qk_w = MLSTM_HEADS * MLSTM_DQK
    full = lambda shp: pl.BlockSpec(shp, lambda i: (0,) * len(shp))
    return pl.pallas_call(
        _mlstm_kernel,
        grid=(s // L,),
        in_specs=[pl.BlockSpec((L, qk_w), lambda i: (i, COL_Q // qk_w)),
                  pl.BlockSpec((L, qk_w), lambda i: (i, COL_K // qk_w)),
                  pl.BlockSpec((L, MLSTM_WIDTH), lambda i: (i, COL_V // MLSTM_WIDTH)),
                  pl.BlockSpec((L, MLSTM_WIDTH), lambda i: (i, COL_O // MLSTM_WIDTH)),
                  pl.BlockSpec((L, LANES), lambda i: (i, COL_SMALL // LANES)),
                  full((1, LANES)), full((1, MLSTM_WIDTH))],
        out_specs=pl.BlockSpec((L, MLSTM_WIDTH), lambda i: (i, 0)),
        out_shape=jax.ShapeDtypeStruct((s, MLSTM_WIDTH), BF16),
        scratch_shapes=[pltpu.VMEM((MLSTM_HEADS, MLSTM_DQK, MLSTM_DV), F32),
                        pltpu.VMEM((SUBLANES, MLSTM_DQK), F32),
                        pltpu.VMEM((SUBLANES, LANES), F32)],
        compiler_params=_cparams("arbitrary"),
        name="mlstm_mixer",
    )(proj, proj, proj, proj, proj, small_bias, norm_w.reshape(1, MLSTM_WIDTH))


def _permute_w_in(w_in):
    o_z, o_xbc, o_dt = 0, 2048, 5120
    o_q, o_k, o_v, o_o, o_i, o_f = 5152, 6176, 7200, 9248, 11296, 11300
    cols = lambda a, n: w_in[:, a:a + n]
    parts = [cols(o_z, 2048), cols(o_xbc, 2048), cols(o_v, 2048), cols(o_o, 2048),
             cols(o_q, 1024), cols(o_k, 1024), cols(o_xbc + 2048, 512), cols(o_xbc + 2560, 512),
             cols(o_dt, 32), cols(o_i, 4), cols(o_f, 4)]
    used = COL_SMALL + 40
    parts.append(jnp.zeros((w_in.shape[0], D_PROJ - used), w_in.dtype))
    return jnp.concatenate(parts, axis=1).astype(BF16)


def _small_bias(dt_bias, i_bias, f_bias):
    v = jnp.concatenate([dt_bias, i_bias, f_bias, jnp.zeros((LANES - 40,), F32)])
    return v.reshape(1, LANES)


def kernel(x, norm_mix_w, w_in, conv_w, conv_b, dt_bias, a_log, d_skip, ssd_norm_w, mlstm_i_bias,
           mlstm_f_bias, mlstm_norm_w, w_out, norm_ffn_w, ffn_w_gate, ffn_w_up, ffn_w_down, router_w,
           moe_w_gate, moe_w_up, moe_w_down, final_norm_w):
    bsz, s, d = x.shape
    depth = w_in.shape[0]
    outs = []
    for b in range(bsz):
        h = x[b]
        for layer in range(depth):
            u = rmsnorm(h, norm_mix_w[layer], BF16)
            proj = matmul(u, _permute_w_in(w_in[layer]), F32, tm=1024, tn=PROJ_TN)
            sbias = _small_bias(dt_bias[layer], mlstm_i_bias[layer], mlstm_f_bias[layer])
            y_ssd = ssd_mixer(proj, conv_w[layer], conv_b[layer], sbias, a_log[layer], d_skip[layer],
                              ssd_norm_w[layer])
            y_ml = mlstm_mixer(proj, sbias, mlstm_norm_w[layer])
            h = out_proj(y_ssd, y_ml, w_out[layer].astype(BF16), h)
            j = layer // 2
            if layer % 2 == 0:
                u = rmsnorm(h, norm_ffn_w[layer], BF16)
                a = swiglu_up(u, ffn_w_gate[j].astype(BF16), ffn_w_up[j].astype(BF16))
                h = down_proj(a, ffn_w_down[j].astype(BF16), h, tk=2048)
            else:
                h = moe_swiglu(h, norm_ffn_w[layer], router_w[j], moe_w_gate[j], moe_w_up[j],
                               moe_w_down[j].astype(BF16))
        outs.append(rmsnorm(h, final_norm_w, F32))
    return jnp.stack(outs, axis=0)
```

```python
import functools

import jax
import jax.numpy as jnp
from jax import lax
from jax.experimental import pallas as pl
from jax.experimental.pallas import tpu as pltpu

F32 = jnp.float32
BF16 = jnp.bfloat16

D_MODEL = 4096
SSD_INNER = 2048
SSD_HEAD_DIM = 64
SSD_HEADS = 32
SSD_GROUPS = 4
SSD_STATE = 128
SSD_CONV = 4
SSD_CHUNK = 128
SSD_GROUP_WIDTH = SSD_INNER // SSD_GROUPS
MLSTM_WIDTH = 2048
MLSTM_HEADS = 4
MLSTM_DV = 512
MLSTM_DQK = 256
FF_DENSE = 14336
N_EXPERTS = 8
FF_EXPERT = 5632
NORM_EPS = 1e-6

LANES = 128
SUBLANES = 8
VMEM_LIMIT_BYTES = 56 * 1024 * 1024

PA_WIDTH, PA_TN = 5120, 1280
PA_Z, PA_X, PA_B, PA_C = 0, 2048, 4096, 4608
PB_START, PB_WIDTH, PB_TN = 5152, 6144, 1024
PB_Q, PB_K, PB_V, PB_O = 0, 1024, 2048, 4096
IN_DT, IN_I, IN_F = 5120, 11296, 11300
SMALL_DT, SMALL_I, SMALL_F = 0, 32, 36

MLSTM_CHUNK = 128


def _cparams(*sem):
    return pltpu.CompilerParams(dimension_semantics=sem, vmem_limit_bytes=VMEM_LIMIT_BYTES)


def _rmsnorm_kernel(x_ref, w_ref, o_ref):
    x = x_ref[...]
    ms = jnp.mean(x * x, axis=-1, keepdims=True)
    o_ref[...] = (x * lax.rsqrt(ms + NORM_EPS) * w_ref[...]).astype(o_ref.dtype)


def rmsnorm(x, w, out_dtype, tr=512):
    s, d = x.shape
    return pl.pallas_call(
        _rmsnorm_kernel,
        grid=(s // tr,),
        in_specs=[pl.BlockSpec((tr, d), lambda i: (i, 0)),
                  pl.BlockSpec((1, d), lambda i: (0, 0))],
        out_specs=pl.BlockSpec((tr, d), lambda i: (i, 0)),
        out_shape=jax.ShapeDtypeStruct((s, d), out_dtype),
        compiler_params=_cparams("parallel"),
        name="rmsnorm",
    )(x, w.reshape(1, d))


def _rmsnorm_small_kernel(x_ref, w_ref, ws_ref, u_ref, ps_ref):
    x = x_ref[...]
    ms = jnp.mean(x * x, axis=-1, keepdims=True)
    u = (x * lax.rsqrt(ms + NORM_EPS) * w_ref[...]).astype(BF16)
    u_ref[...] = u
    ps_ref[...] = jnp.dot(u, ws_ref[...], preferred_element_type=F32)


def rmsnorm_small(x, w, w_small, tr=512):
    s, d = x.shape
    return pl.pallas_call(
        _rmsnorm_small_kernel,
        grid=(s // tr,),
        in_specs=[pl.BlockSpec((tr, d), lambda i: (i, 0)),
                  pl.BlockSpec((1, d), lambda i: (0, 0)),
                  pl.BlockSpec((d, LANES), lambda i: (0, 0))],
        out_specs=[pl.BlockSpec((tr, d), lambda i: (i, 0)),
                   pl.BlockSpec((tr, LANES), lambda i: (i, 0))],
        out_shape=[jax.ShapeDtypeStruct((s, d), BF16),
                   jax.ShapeDtypeStruct((s, LANES), F32)],
        compiler_params=_cparams("parallel"),
        name="rmsnorm_small",
    )(x, w.reshape(1, d), w_small)


RT_SEL = 0
RT_C1, RT_C2 = 8, 9
RT_I1, RT_I2 = 10, 11


def _router_kernel(x_ref, w_ref, rw_ref, rt_ref):
    x = x_ref[...]
    ms = jnp.mean(x * x, axis=-1, keepdims=True)
    u = x * lax.rsqrt(ms + NORM_EPS) * w_ref[...]
    logits = jnp.dot(u, rw_ref[...], preferred_element_type=F32, precision=lax.Precision.HIGHEST)
    lane = lax.broadcasted_iota(jnp.int32, logits.shape, 1)
    valid = lane < N_EXPERTS
    lg = jnp.where(valid, logits, -jnp.inf)
    ex = jnp.exp(lg - jnp.max(lg, axis=1, keepdims=True))
    probs = ex / jnp.sum(ex, axis=1, keepdims=True)
    probs = jnp.where(valid, probs, -1.0)
    p1 = jnp.max(probs, axis=1, keepdims=True)
    i1 = jnp.min(jnp.where(probs == p1, lane, LANES), axis=1, keepdims=True)
    rest = jnp.where(lane == i1, -1.0, probs)
    p2 = jnp.max(rest, axis=1, keepdims=True)
    i2 = jnp.min(jnp.where(rest == p2, lane, LANES), axis=1, keepdims=True)
    tot = p1 + p2
    rt = jnp.where((lane == i1) | (lane == i2), 1.0, 0.0)
    rt = jnp.where(lane == RT_C1, p1 / tot, rt)
    rt = jnp.where(lane == RT_C2, p2 / tot, rt)
    rt = jnp.where(lane == RT_I1, i1.astype(F32), rt)
    rt = jnp.where(lane == RT_I2, i2.astype(F32), rt)
    rt_ref[...] = rt


def router(x, w, router_w, tr=512):
    s, d = x.shape
    rw = jnp.pad(router_w, ((0, 0), (0, LANES - N_EXPERTS)))
    return pl.pallas_call(
        _router_kernel,
        grid=(s // tr,),
        in_specs=[pl.BlockSpec((tr, d), lambda i: (i, 0)),
                  pl.BlockSpec((1, d), lambda i: (0, 0)),
                  pl.BlockSpec((d, LANES), lambda i: (0, 0))],
        out_specs=pl.BlockSpec((tr, LANES), lambda i: (i, 0)),
        out_shape=jax.ShapeDtypeStruct((s, LANES), F32),
        compiler_params=_cparams("parallel"),
        name="router",
    )(x, w.reshape(1, d), rw)


MOE_TM = 512
PLAN_BLK = 512


def _plan_kernel(rt_ref, dest_ref, meta_ref):
    s = rt_ref.shape[0]
    nblk = s // PLAN_BLK
    lane = lax.broadcasted_iota(jnp.int32, (PLAN_BLK, LANES), 1)
    is_exp = lane < N_EXPERTS
    r = lax.broadcasted_iota(jnp.int32, (PLAN_BLK, PLAN_BLK), 0)
    c = lax.broadcasted_iota(jnp.int32, (PLAN_BLK, PLAN_BLK), 1)
    strict_tril = (r > c).astype(BF16)

    def rank_body(b, carry):
        rows = pl.ds(pl.multiple_of(b * PLAN_BLK, PLAN_BLK), PLAN_BLK)
        sel = jnp.where(is_exp, rt_ref[rows, :], 0.0)
        rank = jnp.dot(strict_tril, sel.astype(BF16), preferred_element_type=F32) + carry
        dest_ref[rows, :] = rank.astype(jnp.int32)
        return carry + jnp.sum(sel, axis=0, keepdims=True)

    counts = lax.fori_loop(0, nblk, rank_body, jnp.zeros((1, LANES), F32))
    tiles = jnp.floor((counts + (MOE_TM - 1)) * (1.0 / MOE_TM))
    er = lax.broadcasted_iota(jnp.int32, (LANES, LANES), 0)
    ec = lax.broadcasted_iota(jnp.int32, (LANES, LANES), 1)
    before = (er < ec).astype(F32)
    tile_start = jnp.dot(jnp.broadcast_to(tiles, (SUBLANES, LANES)), before, preferred_element_type=F32,
                         precision=lax.Precision.HIGHEST)[0:1, :]
    row_off = tile_start * MOE_TM
    meta_ref[0:1, :] = counts.astype(jnp.int32)
    meta_ref[1:2, :] = tiles.astype(jnp.int32)
    meta_ref[2:3, :] = tile_start.astype(jnp.int32)
    meta_ref[3:SUBLANES, :] = jnp.zeros((SUBLANES - 3, LANES), jnp.int32)

    def dest_body(b, _):
        rows = pl.ds(pl.multiple_of(b * PLAN_BLK, PLAN_BLK), PLAN_BLK)
        rt = rt_ref[rows, :]
        pos = dest_ref[rows, :].astype(F32) + row_off
        i1 = rt[:, RT_I1:RT_I1 + 1].astype(jnp.int32)
        i2 = rt[:, RT_I2:RT_I2 + 1].astype(jnp.int32)
        d1 = jnp.sum(jnp.where(lane == i1, pos, 0.0), axis=1, keepdims=True)
        d2 = jnp.sum(jnp.where(lane == i2, pos, 0.0), axis=1, keepdims=True)
        dest_ref[rows, :] = jnp.where(lane == 0, d1, jnp.where(lane == 1, d2, 0.0)).astype(jnp.int32)
        return 0

    lax.fori_loop(0, nblk, dest_body, 0)


def moe_plan(rt):
    s = rt.shape[0]
    return pl.pallas_call(
        _plan_kernel,
        out_shape=[jax.ShapeDtypeStruct((s, LANES), jnp.int32),
                   jax.ShapeDtypeStruct((SUBLANES, LANES), jnp.int32)],
        compiler_params=pltpu.CompilerParams(vmem_limit_bytes=VMEM_LIMIT_BYTES),
        name="moe_plan",
    )(rt)


def _invert_kernel(d1_ref, d2_ref, src_ref):
    n_rows = src_ref.shape[0]
    n_tok = d1_ref.shape[0]

    def zero(i, _):
        src_ref[i] = 0
        return 0

    lax.fori_loop(0, n_rows, zero, 0, unroll=16)

    def put(t, _):
        src_ref[d1_ref[t]] = t
        src_ref[d2_ref[t]] = t
        return 0

    lax.fori_loop(0, n_tok, put, 0, unroll=8)


def moe_invert(d1, d2, n_rows):
    smem = pl.BlockSpec(memory_space=pltpu.SMEM)
    return pl.pallas_call(
        _invert_kernel,
        in_specs=[smem, smem],
        out_specs=smem,
        out_shape=jax.ShapeDtypeStruct((n_rows,), jnp.int32),
        name="moe_invert",
    )(d1, d2)


GATHER_TM = 256
GATHER_UNROLL = 8


def _gather_norm_kernel(src_ref, nt_ref, h_hbm, w_ref, o_ref, buf_ref, sem):
    i = pl.program_id(0)
    nt = nt_ref[0]

    def row_copy(tile, slot, r):
        return pltpu.make_async_copy(h_hbm.at[pl.ds(src_ref[tile * MOE_TM + r], 1), :],
                                     buf_ref.at[slot, pl.ds(r, 1), :], sem.at[slot])

    def issue(tile, slot):
        def body(r, _):
            row_copy(tile, slot, r).start()
            return 0
        lax.fori_loop(0, MOE_TM, body, 0, unroll=GATHER_UNROLL)

    @pl.when(i == 0)
    def _():
        issue(0, 0)

    @pl.when(i + 1 < nt)
    def _():
        issue(i + 1, (i + 1) % 2)

    @pl.when(i < nt)
    def _():
        slot = i % 2

        def body(r, _):
            row_copy(i, slot, r).wait()
            return 0
        lax.fori_loop(0, MOE_TM, body, 0, unroll=GATHER_UNROLL)
        x = buf_ref[slot]
        ms = jnp.mean(x * x, axis=-1, keepdims=True)
        o_ref[...] = (x * lax.rsqrt(ms + NORM_EPS) * w_ref[...]).astype(o_ref.dtype)

    @pl.when(i >= nt)
    def _():
        o_ref[...] = jnp.zeros(o_ref.shape, o_ref.dtype)


def moe_gather_norm(src, nt, h, w):
    n_rows = src.shape[0]
    d = h.shape[1]
    return pl.pallas_call(
        _gather_norm_kernel,
        grid_spec=pltpu.PrefetchScalarGridSpec(
            num_scalar_prefetch=2,
            grid=(n_rows // MOE_TM,),
            in_specs=[pl.BlockSpec(memory_space=pl.ANY),
                      pl.BlockSpec((1, d), lambda i, src, nt: (0, 0))],
            out_specs=pl.BlockSpec((MOE_TM, d), lambda i, src, nt: (i, 0)),
            scratch_shapes=[pltpu.VMEM((2, MOE_TM, d), F32), pltpu.SemaphoreType.DMA((2,))]),
        out_shape=jax.ShapeDtypeStruct((n_rows, d), BF16),
        compiler_params=_cparams("arbitrary"),
        name="moe_gather_norm",
    )(src, nt, h, w.reshape(1, d))


def _combine_kernel(d1_ref, d2_ref, h_ref, rt_ref, pw_ref, y_hbm, o_ref, ya_ref, yb_ref, sem, *, post_norm):
    i = pl.program_id(0)
    n = pl.num_programs(0)

    def copy_a(blk, slot, r):
        return pltpu.make_async_copy(y_hbm.at[pl.ds(d1_ref[blk * GATHER_TM + r], 1), :],
                                     ya_ref.at[slot, pl.ds(r, 1), :], sem.at[0, slot])

    def copy_b(blk, slot, r):
        return pltpu.make_async_copy(y_hbm.at[pl.ds(d2_ref[blk * GATHER_TM + r], 1), :],
                                     yb_ref.at[slot, pl.ds(r, 1), :], sem.at[1, slot])

    def issue(blk, slot):
        def body(r, _):
            copy_a(blk, slot, r).start()
            copy_b(blk, slot, r).start()
            return 0
        lax.fori_loop(0, GATHER_TM, body, 0, unroll=GATHER_UNROLL)

    @pl.when(i == 0)
    def _():
        issue(0, 0)

    @pl.when(i + 1 < n)
    def _():
        issue(i + 1, (i + 1) % 2)

    slot = i % 2

    def wait(r, _):
        copy_a(i, slot, r).wait()
        copy_b(i, slot, r).wait()
        return 0
    lax.fori_loop(0, GATHER_TM, wait, 0, unroll=GATHER_UNROLL)
    rt = rt_ref[...]
    out = h_ref[...] + rt[:, RT_C1:RT_C1 + 1] * ya_ref[slot] + rt[:, RT_C2:RT_C2 + 1] * yb_ref[slot]
    if post_norm:
        ms = jnp.mean(out * out, axis=-1, keepdims=True)
        out = out * lax.rsqrt(ms + NORM_EPS) * pw_ref[...]
    o_ref[...] = out


def moe_combine(d1, d2, h, rt, y, post_norm_w):
    s, d = h.shape
    post_norm = post_norm_w is not None
    pw = (post_norm_w if post_norm else jnp.ones((d,), F32)).reshape(1, d)
    return pl.pallas_call(
        functools.partial(_combine_kernel, post_norm=post_norm),
        grid_spec=pltpu.PrefetchScalarGridSpec(
            num_scalar_prefetch=2,
            grid=(s // GATHER_TM,),
            in_specs=[pl.BlockSpec((GATHER_TM, d), lambda i, a, b: (i, 0)),
                      pl.BlockSpec((GATHER_TM, LANES), lambda i, a, b: (i, 0)),
                      pl.BlockSpec((1, d), lambda i, a, b: (0, 0)),
                      pl.BlockSpec(memory_space=pl.ANY)],
            out_specs=pl.BlockSpec((GATHER_TM, d), lambda i, a, b: (i, 0)),
            scratch_shapes=[pltpu.VMEM((2, GATHER_TM, d), F32), pltpu.VMEM((2, GATHER_TM, d), F32),
                            pltpu.SemaphoreType.DMA((2, 2))]),
        out_shape=jax.ShapeDtypeStruct((s, d), F32),
        compiler_params=_cparams("arbitrary"),
        name="moe_combine",
    )(d1, d2, h, rt, pw, y)


def _mm_kernel(x_ref, w_ref, o_ref):
    o_ref[...] = jnp.dot(x_ref[...], w_ref[...], preferred_element_type=F32).astype(o_ref.dtype)


def matmul(x, w, out_dtype, tm, tn):
    m, k = x.shape
    n = w.shape[1]
    return pl.pallas_call(
        _mm_kernel,
        grid=(m // tm, n // tn),
        in_specs=[pl.BlockSpec((tm, k), lambda i, j: (i, 0)),
                  pl.BlockSpec((k, tn), lambda i, j: (0, j))],
        out_specs=pl.BlockSpec((tm, tn), lambda i, j: (i, j)),
        out_shape=jax.ShapeDtypeStruct((m, n), out_dtype),
        compiler_params=_cparams("parallel", "arbitrary"),
        name="in_proj",
    )(x, w)


def _out_proj_kernel(ya_ref, yb_ref, wa_ref, wb_ref, r_ref, o_ref):
    acc = jnp.dot(ya_ref[...], wa_ref[...], preferred_element_type=F32)
    acc += jnp.dot(yb_ref[...], wb_ref[...], preferred_element_type=F32)
    o_ref[...] = r_ref[...] + acc


def out_proj(y_ssd, y_ml, w_out, resid, tm=1024, tn=1024):
    m, ka = y_ssd.shape
    n = w_out.shape[1]
    return pl.pallas_call(
        _out_proj_kernel,
        grid=(m // tm, n // tn),
        in_specs=[pl.BlockSpec((tm, ka), lambda i, j: (i, 0)),
                  pl.BlockSpec((tm, ka), lambda i, j: (i, 0)),
                  pl.BlockSpec((ka, tn), lambda i, j: (0, j)),
                  pl.BlockSpec((ka, tn), lambda i, j: (1, j)),
                  pl.BlockSpec((tm, tn), lambda i, j: (i, j))],
        out_specs=pl.BlockSpec((tm, tn), lambda i, j: (i, j)),
        out_shape=jax.ShapeDtypeStruct((m, n), F32),
        compiler_params=_cparams("parallel", "arbitrary"),
        name="out_proj",
    )(y_ssd, y_ml, w_out, w_out, resid)


def _silu(x):
    return x * jax.nn.sigmoid(x)


def _swiglu_up_kernel(x_ref, wg_ref, wu_ref, o_ref):
    x = x_ref[...]
    g = jnp.dot(x, wg_ref[...], preferred_element_type=F32)
    u = jnp.dot(x, wu_ref[...], preferred_element_type=F32)
    o_ref[...] = (_silu(g) * u).astype(o_ref.dtype)


def swiglu_up(x, wg, wu, tm=1024, tf=512):
    m, k = x.shape
    f = wg.shape[1]
    return pl.pallas_call(
        _swiglu_up_kernel,
        grid=(m // tm, f // tf),
        in_specs=[pl.BlockSpec((tm, k), lambda i, j: (i, 0)),
                  pl.BlockSpec((k, tf), lambda i, j: (0, j)),
                  pl.BlockSpec((k, tf), lambda i, j: (0, j))],
        out_specs=pl.BlockSpec((tm, tf), lambda i, j: (i, j)),
        out_shape=jax.ShapeDtypeStruct((m, f), BF16),
        compiler_params=_cparams("parallel", "arbitrary"),
        name="swiglu_up",
    )(x, wg, wu)


def _moe_up_kernel(eot_ref, nt_ref, x_ref, wg_ref, wu_ref, o_ref, wgb_ref, wub_ref):
    t = pl.program_id(1)

    @pl.when(t < nt_ref[0])
    def _():
        @pl.when((t == 0) | (eot_ref[t] != eot_ref[jnp.maximum(t - 1, 0)]))
        def _():
            wgb_ref[...] = wg_ref[...].astype(BF16)
            wub_ref[...] = wu_ref[...].astype(BF16)

        x = x_ref[...]
        g = jnp.dot(x, wgb_ref[...], preferred_element_type=F32)
        u = jnp.dot(x, wub_ref[...], preferred_element_type=F32)
        o_ref[...] = (_silu(g) * u).astype(o_ref.dtype)

    @pl.when(pl.program_id(1) >= nt_ref[0])
    def _():
        o_ref[...] = jnp.zeros(o_ref.shape, o_ref.dtype)


def moe_up(eot, nt, xs, wg, wu, tf=512):
    r, k = xs.shape
    f = wg.shape[2]
    row = lambda j, t, eot, nt: (jnp.maximum(jnp.minimum(t, nt[0] - 1), 0), 0)
    wmap = lambda j, t, eot, nt: (eot[t], 0, j)
    return pl.pallas_call(
        _moe_up_kernel,
        grid_spec=pltpu.PrefetchScalarGridSpec(
            num_scalar_prefetch=2,
            grid=(f // tf, r // MOE_TM),
            in_specs=[pl.BlockSpec((MOE_TM, k), row),
                      pl.BlockSpec((None, k, tf), wmap),
                      pl.BlockSpec((None, k, tf), wmap)],
            out_specs=pl.BlockSpec((MOE_TM, tf), lambda j, t, eot, nt: (t, j)),
            scratch_shapes=[pltpu.VMEM((k, tf), BF16), pltpu.VMEM((k, tf), BF16)]),
        out_shape=jax.ShapeDtypeStruct((r, f), BF16),
        compiler_params=_cparams("arbitrary", "arbitrary"),
        name="moe_up",
    )(eot, nt, xs, wg, wu)


def _moe_down_kernel(eot_ref, nt_ref, a_ref, w_ref, o_ref):
    @pl.when(pl.program_id(1) < nt_ref[0])
    def _():
        o_ref[...] = jnp.dot(a_ref[...], w_ref[...], preferred_element_type=F32)

    @pl.when(pl.program_id(1) >= nt_ref[0])
    def _():
        o_ref[...] = jnp.zeros(o_ref.shape, o_ref.dtype)


def moe_down(eot, nt, a, wd, tn=1024):
    r, f = a.shape
    n = wd.shape[2]
    return pl.pallas_call(
        _moe_down_kernel,
        grid_spec=pltpu.PrefetchScalarGridSpec(
            num_scalar_prefetch=2,
            grid=(n // tn, r // MOE_TM),
            in_specs=[pl.BlockSpec((MOE_TM, f), lambda j, t, eot, nt: (jnp.maximum(jnp.minimum(t, nt[0] - 1), 0), 0)),
                      pl.BlockSpec((None, f, tn), lambda j, t, eot, nt: (eot[t], 0, j))],
            out_specs=pl.BlockSpec((MOE_TM, tn), lambda j, t, eot, nt: (t, j))),
        out_shape=jax.ShapeDtypeStruct((r, n), F32),
        compiler_params=_cparams("arbitrary", "arbitrary"),
        name="moe_down",
    )(eot, nt, a, wd)


def moe_swiglu(h, norm_w, router_w, wg, wu, wd, post_norm_w=None):
    s, d = h.shape
    n_tiles_max = (2 * s) // MOE_TM + N_EXPERTS
    n_rows = n_tiles_max * MOE_TM
    rt = router(h, norm_w, router_w)
    dest, meta = moe_plan(rt)
    d1, d2 = dest[:, 0], dest[:, 1]
    tiles, tile_start = meta[1, :N_EXPERTS], meta[2, :N_EXPERTS]
    nt = jnp.sum(tiles).reshape(1)
    tile_end = tile_start + tiles
    tid = jnp.minimum(jnp.arange(n_tiles_max, dtype=jnp.int32), nt - 1)
    eot = jnp.sum((tid[:, None] >= tile_end[None, :]).astype(jnp.int32), axis=1)
    src = moe_invert(d1, d2, n_rows)
    xs = moe_gather_norm(src, nt, h, norm_w)
    a = moe_up(eot, nt, xs, wg, wu)
    y = moe_down(eot, nt, a, wd)
    return moe_combine(d1, d2, h, rt, y, post_norm_w)


def _down_kernel(a_ref, w_ref, r_ref, o_ref):
    kk = pl.program_id(2)

    @pl.when(kk == 0)
    def _():
        o_ref[...] = r_ref[...]

    o_ref[...] += jnp.dot(a_ref[...], w_ref[...], preferred_element_type=F32)


def down_proj(a, w, resid, tk, tm=1024, tn=1024):
    m, k = a.shape
    n = w.shape[1]
    return pl.pallas_call(
        _down_kernel,
        grid=(m // tm, n // tn, k // tk),
        in_specs=[pl.BlockSpec((tm, tk), lambda i, j, kk: (i, kk)),
                  pl.BlockSpec((tk, tn), lambda i, j, kk: (kk, j)),
                  pl.BlockSpec((tm, tn), lambda i, j, kk: (i, j))],
        out_specs=pl.BlockSpec((tm, tn), lambda i, j, kk: (i, j)),
        out_shape=jax.ShapeDtypeStruct((m, n), F32),
        compiler_params=_cparams("parallel", "arbitrary", "arbitrary"),
        name="down_proj",
    )(a, w, resid)


def _softplus(x):
    return jnp.maximum(x, 0.0) + jnp.log1p(jnp.exp(-jnp.abs(x)))


def _tril_ones(n):
    r = lax.broadcasted_iota(jnp.int32, (n, n), 0)
    c = lax.broadcasted_iota(jnp.int32, (n, n), 1)
    return r >= c


def _ssd_kernel(z_ref, x_ref, b_ref, c_ref, sm_ref, convw_ref, convb_ref, smb_ref, alog_ref,
                dskip_ref, nw_ref, expand_ref, o_ref, tail_ref, state_ref):
    L = SSD_CHUNK
    HI = lax.Precision.HIGHEST

    @pl.when(pl.program_id(0) == 0)
    def _():
        tail_ref[...] = jnp.zeros(tail_ref.shape, F32)
        state_ref[...] = jnp.zeros(state_ref.shape, F32)

    row8 = lax.broadcasted_iota(jnp.int32, (SUBLANES, LANES), 0)

    def conv_silu(u, col0):
        w = u.shape[1]
        cols = slice(col0, col0 + w)
        tail = tail_ref[:, cols]
        acc = convb_ref[:, cols] + u * convw_ref[SSD_CONV - 1:SSD_CONV, cols]
        for k in range(1, SSD_CONV):
            rolled = pltpu.roll(u, k, axis=0)
            fix = jnp.where(jnp.tile(row8, (1, w // LANES)) < k, pltpu.roll(tail, k, axis=0), rolled[0:SUBLANES])
            shifted = jnp.concatenate([fix, rolled[SUBLANES:]], axis=0)
            acc = acc + shifted * convw_ref[SSD_CONV - 1 - k:SSD_CONV - k, cols]
        tail_ref[:, cols] = u[L - SUBLANES:L]
        return _silu(acc)

    xs = conv_silu(x_ref[...], 0)
    bmat = conv_silu(b_ref[...], SSD_INNER)
    cmat = conv_silu(c_ref[...], SSD_INNER + SSD_GROUPS * SSD_STATE)

    lane = lax.broadcasted_iota(jnp.int32, (L, LANES), 1)

    def expand_heads(v):
        hi = v.astype(BF16).astype(F32)
        r1 = v - hi
        mid = r1.astype(BF16).astype(F32)
        lo = r1 - mid
        packed = jnp.where(lane < 32, hi,
                           jnp.where(lane < 64, pltpu.roll(mid, 32, axis=1),
                                     jnp.where(lane < 96, pltpu.roll(lo, 64, axis=1), 0.0)))
        return jnp.dot(packed.astype(BF16), expand_ref[...], preferred_element_type=F32)

    causal = _tril_ones(L)
    dt = jnp.where(lane < SSD_HEADS, _softplus(sm_ref[...] + smb_ref[...]), 0.0)
    da = dt * (-jnp.exp(alog_ref[...]))
    acs = jnp.dot(causal.astype(F32), da, preferred_element_type=F32, precision=HI)
    acs_t = acs.T
    a_last = acs[L - 1:L, :]
    dt_e = expand_heads(dt)
    eacs_e = expand_heads(jnp.exp(acs))
    dst_e = expand_heads(jnp.exp(a_last - acs))
    xdt = xs * dt_e
    xdec = (xdt * dst_e).astype(BF16)
    xdt_b = xdt.astype(BF16)
    chunk_decay_e = eacs_e[L - 1:L, :]

    y = xs * dskip_ref[...]
    zgate = _silu(z_ref[...])
    lo_half = lane < SSD_HEAD_DIM

    def group(g):
        gs = slice(g * SSD_GROUP_WIDTH, (g + 1) * SSD_GROUP_WIDTH)
        bg = bmat[:, g * SSD_STATE:(g + 1) * SSD_STATE]
        cg = cmat[:, g * SSD_STATE:(g + 1) * SSD_STATE]
        cg_b = cg.astype(BF16)
        cb = lax.dot_general(cg_b, bg.astype(BF16), (((1,), (1,)), ((), ())),
                             preferred_element_type=F32)
        h_enter = state_ref[g]
        y_off = jnp.dot(cg_b, h_enter.astype(BF16), preferred_element_type=F32)
        st_new = jnp.dot(bg.T.astype(BF16), xdec[:, gs], preferred_element_type=F32)
        state_ref[g] = chunk_decay_e[:, gs] * h_enter + st_new
        yd_parts = []
        for p in range(4):
            j0 = g * 8 + 2 * p
            ms = []
            for j in (j0, j0 + 1):
                seg = jnp.where(causal, acs[:, j:j + 1] - acs_t[j:j + 1, :], -jnp.inf)
                ms.append((cb * jnp.exp(seg)).astype(BF16))
            xp = xdt_b[:, g * SSD_GROUP_WIDTH + p * LANES:g * SSD_GROUP_WIDTH + (p + 1) * LANES]
            zero = jnp.zeros_like(xp)
            rhs = jnp.concatenate([jnp.where(lo_half, xp, zero), jnp.where(lo_half, zero, xp)], axis=0)
            yd_parts.append(jnp.dot(jnp.concatenate(ms, axis=1), rhs, preferred_element_type=F32))
        y_diag = jnp.concatenate(yd_parts, axis=1)
        yg = (y[:, gs] + y_diag + y_off * eacs_e[:, gs]) * zgate[:, gs]
        msq = jnp.mean(yg * yg, axis=1, keepdims=True)
        o_ref[:, gs] = (yg * lax.rsqrt(msq + NORM_EPS) * nw_ref[:, gs]).astype(o_ref.dtype)

    return group


def _log_sigmoid(x):
    return jnp.minimum(x, 0.0) - jnp.log1p(jnp.exp(-jnp.abs(x)))


def _mlstm_kernel(q_ref, k_ref, v_ref, o_ref, sm_ref, smb_ref, nw_ref, out_ref, c_ref, n_ref, m_ref):
    L = MLSTM_CHUNK
    HI = lax.Precision.HIGHEST

    @pl.when(pl.program_id(0) == 0)
    def _():
        c_ref[...] = jnp.zeros(c_ref.shape, F32)
        n_ref[...] = jnp.zeros(n_ref.shape, F32)
        m_ref[...] = jnp.zeros(m_ref.shape, F32)

    causal = _tril_ones(L)
    sm = sm_ref[...] + smb_ref[...]
    bcs = jnp.dot(causal.astype(F32), _log_sigmoid(sm), preferred_element_type=F32, precision=HI)
    sm_t = sm.T
    bcs_t = bcs.T

    def head(h):
        li_col = sm[:, SMALL_I + h:SMALL_I + h + 1]
        b_col = bcs[:, SMALL_F + h:SMALL_F + h + 1]
        r_row = sm_t[SMALL_I + h:SMALL_I + h + 1, :] - bcs_t[SMALL_F + h:SMALL_F + h + 1, :]
        m_prev = m_ref[h:h + 1, 0:1]
        inter = b_col + m_prev
        dm = jnp.where(causal, b_col + r_row, -jnp.inf)
        m_t = jnp.maximum(inter, jnp.max(dm, axis=1, keepdims=True))
        w_intra = jnp.exp(dm - m_t)
        w_inter = jnp.exp(inter - m_t)
        qf = q_ref[:, h * MLSTM_DQK:(h + 1) * MLSTM_DQK] * (MLSTM_DQK ** -0.5)
        kf = k_ref[:, h * MLSTM_DQK:(h + 1) * MLSTM_DQK]
        qb = qf.astype(BF16)
        vb = v_ref[:, h * MLSTM_DV:(h + 1) * MLSTM_DV].astype(BF16)
        sqk = lax.dot_general(qb, kf.astype(BF16), (((1,), (1,)), ((), ())),
                              preferred_element_type=F32) * w_intra
        c_st = c_ref[h]
        n_row = n_ref[h:h + 1, :]
        num = (w_inter * jnp.dot(qb, c_st.astype(BF16), preferred_element_type=F32)
               + jnp.dot(sqk.astype(BF16), vb, preferred_element_type=F32))
        qn = jnp.sum(qf * n_row, axis=1, keepdims=True)
        den = w_inter * qn + jnp.sum(sqk, axis=1, keepdims=True)
        hout = num / jnp.maximum(jnp.abs(den), jnp.exp(-m_t))
        b_last = b_col[L - 1:L, :]
        g_col = b_last - b_col + li_col
        m_new = jnp.maximum(b_last + m_prev, jnp.max(g_col, axis=0, keepdims=True))
        decay = jnp.exp(b_last + m_prev - m_new)
        kw = kf * jnp.exp(g_col - m_new)
        c_ref[h] = decay * c_st + jnp.dot(kw.T.astype(BF16), vb, preferred_element_type=F32)
        n_ref[h:h + 1, :] = decay * n_row + jnp.sum(kw, axis=0, keepdims=True)
        m_ref[h:h + 1, :] = jnp.broadcast_to(m_new, (1, LANES))
        vs = slice(h * MLSTM_DV, (h + 1) * MLSTM_DV)
        msq = jnp.mean(hout * hout, axis=1, keepdims=True)
        hn = hout * lax.rsqrt(msq + NORM_EPS) * nw_ref[:, vs]
        out_ref[:, vs] = (jax.nn.sigmoid(o_ref[:, vs]) * hn).astype(out_ref.dtype)

    return head


assert MLSTM_CHUNK == SSD_CHUNK


def _mixer_kernel(z_ref, x_ref, b_ref, c_ref, q_ref, k_ref, v_ref, og_ref, sm_ref,
                  convw_ref, convb_ref, smb_ref, alog_ref, dskip_ref, ssd_nw_ref, expand_ref, ml_nw_ref,
                  y_ssd_ref, y_ml_ref, tail_ref, state_ref, cst_ref, nst_ref, mst_ref):
    ssd_group = _ssd_kernel(z_ref, x_ref, b_ref, c_ref, sm_ref, convw_ref, convb_ref, smb_ref, alog_ref,
                            dskip_ref, ssd_nw_ref, expand_ref, y_ssd_ref, tail_ref, state_ref)
    for g in range(SSD_GROUPS):
        ssd_group(g)
    mlstm_head = _mlstm_kernel(q_ref, k_ref, v_ref, og_ref, sm_ref, smb_ref, ml_nw_ref, y_ml_ref,
                               cst_ref, nst_ref, mst_ref)
    for h in range(MLSTM_HEADS):
        mlstm_head(h)


def hybrid_mixer(proj_a, proj_b, proj_s, conv_w, conv_b, small_bias, a_log, d_skip, ssd_norm_w, ml_norm_w):
    s = proj_a.shape[0]
    L = SSD_CHUNK
    conv_dim = conv_w.shape[1]
    qk_w = MLSTM_HEADS * MLSTM_DQK
    alog = jnp.pad(a_log, (0, LANES - SSD_HEADS)).reshape(1, LANES)
    dskip_e = jnp.repeat(d_skip, SSD_HEAD_DIM).reshape(1, SSD_INNER)
    piece_row = jnp.arange(LANES)[:, None]
    expand = ((jnp.arange(SSD_INNER)[None, :] // SSD_HEAD_DIM == piece_row % SSD_HEADS)
              & (piece_row < 3 * SSD_HEADS)).astype(BF16)
    full = lambda shp: pl.BlockSpec(shp, lambda i: (0,) * len(shp))
    blk = lambda w, col: pl.BlockSpec((L, w), lambda i: (i, col // w))
    return pl.pallas_call(
        _mixer_kernel,
        grid=(s // L,),
        in_specs=[blk(SSD_INNER, PA_Z), blk(SSD_INNER, PA_X), blk(512, PA_B), blk(512, PA_C),
                  blk(qk_w, PB_Q), blk(qk_w, PB_K), blk(MLSTM_WIDTH, PB_V), blk(MLSTM_WIDTH, PB_O),
                  blk(LANES, 0),
                  full((SSD_CONV, conv_dim)), full((1, conv_dim)), full((1, LANES)), full((1, LANES)),
                  full((1, SSD_INNER)), full((1, SSD_INNER)), full((LANES, SSD_INNER)),
                  full((1, MLSTM_WIDTH))],
        out_specs=[pl.BlockSpec((L, SSD_INNER), lambda i: (i, 0)),
                   pl.BlockSpec((L, MLSTM_WIDTH), lambda i: (i, 0))],
        out_shape=[jax.ShapeDtypeStruct((s, SSD_INNER), BF16),
                   jax.ShapeDtypeStruct((s, MLSTM_WIDTH), BF16)],
        scratch_shapes=[pltpu.VMEM((SUBLANES, conv_dim), F32),
                        pltpu.VMEM((SSD_GROUPS, SSD_STATE, SSD_GROUP_WIDTH), F32),
                        pltpu.VMEM((MLSTM_HEADS, MLSTM_DQK, MLSTM_DV), F32),
                        pltpu.VMEM((SUBLANES, MLSTM_DQK), F32),
                        pltpu.VMEM((SUBLANES, LANES), F32)],
        compiler_params=_cparams("arbitrary"),
        name="hybrid_mixer",
    )(proj_a, proj_a, proj_a, proj_a, proj_b, proj_b, proj_b, proj_b, proj_s,
      conv_w, conv_b.reshape(1, conv_dim), small_bias, alog, dskip_e,
      ssd_norm_w.reshape(1, SSD_INNER), expand, ml_norm_w.reshape(1, MLSTM_WIDTH))


def _split_w_in(w_in):
    w_a = w_in[:, :PA_WIDTH].astype(BF16)
    w_b = w_in[:, PB_START:PB_START + PB_WIDTH].astype(BF16)
    w_s = jnp.concatenate([w_in[:, IN_DT:IN_DT + SSD_HEADS], w_in[:, IN_I:IN_I + MLSTM_HEADS],
                           w_in[:, IN_F:IN_F + MLSTM_HEADS],
                           jnp.zeros((w_in.shape[0], LANES - SSD_HEADS - 2 * MLSTM_HEADS), w_in.dtype)],
                          axis=1).astype(BF16)
    return w_a, w_b, w_s


def _small_bias(dt_bias, i_bias, f_bias):
    v = jnp.concatenate([dt_bias, i_bias, f_bias, jnp.zeros((LANES - 40,), F32)])
    return v.reshape(1, LANES)


def kernel(x, norm_mix_w, w_in, conv_w, conv_b, dt_bias, a_log, d_skip, ssd_norm_w, mlstm_i_bias,
           mlstm_f_bias, mlstm_norm_w, w_out, norm_ffn_w, ffn_w_gate, ffn_w_up, ffn_w_down, router_w,
           moe_w_gate, moe_w_up, moe_w_down, final_norm_w):
    bsz, s, d = x.shape
    depth = w_in.shape[0]
    outs = []
    for b in range(bsz):
        h = x[b]
        normed = False
        for layer in range(depth):
            w_a, w_b, w_s = _split_w_in(w_in[layer])
            u, proj_s = rmsnorm_small(h, norm_mix_w[layer], w_s)
            proj_a = matmul(u, w_a, F32, tm=1024, tn=PA_TN)
            proj_b = matmul(u, w_b, F32, tm=1024, tn=PB_TN)
            sbias = _small_bias(dt_bias[layer], mlstm_i_bias[layer], mlstm_f_bias[layer])
            y_ssd, y_ml = hybrid_mixer(proj_a, proj_b, proj_s, conv_w[layer], conv_b[layer], sbias,
                                       a_log[layer], d_skip[layer], ssd_norm_w[layer], mlstm_norm_w[layer])
            h = out_proj(y_ssd, y_ml, w_out[layer].astype(BF16), h)
            j = layer // 2
            if layer % 2 == 0:
                u = rmsnorm(h, norm_ffn_w[layer], BF16)
                a = swiglu_up(u, ffn_w_gate[j].astype(BF16), ffn_w_up[j].astype(BF16))
                h = down_proj(a, ffn_w_down[j].astype(BF16), h, tk=2048)
            else:
                normed = layer == depth - 1
                h = moe_swiglu(h, norm_ffn_w[layer], router_w[j], moe_w_gate[j], moe_w_up[j],
                               moe_w_down[j].astype(BF16), final_norm_w if normed else None)
        outs.append(h if normed else rmsnorm(h, final_norm_w, F32))
    return jnp.stack(outs, axis=0)
```

```python
import functools
from typing import NamedTuple

import jax
import jax.numpy as jnp
from jax import lax
from jax.experimental import pallas as pl
from jax.experimental.pallas import tpu as pltpu

F32 = jnp.float32
BF16 = jnp.bfloat16

D_MODEL = 4096
SSD_INNER = 2048
SSD_HEAD_DIM = 64
SSD_HEADS = 32
SSD_GROUPS = 4
SSD_STATE = 128
SSD_CONV = 4
SSD_CHUNK = 128
SSD_GROUP_WIDTH = SSD_INNER // SSD_GROUPS
MLSTM_WIDTH = 2048
MLSTM_HEADS = 4
MLSTM_DV = 512
MLSTM_DQK = 256
FF_DENSE = 14336
N_EXPERTS = 8
FF_EXPERT = 5632
NORM_EPS = 1e-6

LANES = 128
SUBLANES = 8
VMEM_LIMIT_BYTES = 56 * 1024 * 1024

PA_WIDTH, PA_TN = 5120, 1280
PA_Z, PA_X, PA_B, PA_C = 0, 2048, 4096, 4608
PB_START, PB_WIDTH, PB_TN = 5152, 6144, 1024
PB_Q, PB_K, PB_V, PB_O = 0, 1024, 2048, 4096
IN_DT, IN_I, IN_F = 5120, 11296, 11300
SMALL_DT, SMALL_I, SMALL_F = 0, 32, 36

MLSTM_CHUNK = 128


def _cparams(*sem):
    return pltpu.CompilerParams(dimension_semantics=sem, vmem_limit_bytes=VMEM_LIMIT_BYTES)


def _rmsnorm_kernel(x_ref, w_ref, o_ref):
    x = x_ref[...]
    ms = jnp.mean(x * x, axis=-1, keepdims=True)
    o_ref[...] = (x * lax.rsqrt(ms + NORM_EPS) * w_ref[...]).astype(o_ref.dtype)


def rmsnorm(x, w, out_dtype, tr=512):
    s, d = x.shape
    return pl.pallas_call(
        _rmsnorm_kernel,
        grid=(s // tr,),
        in_specs=[pl.BlockSpec((tr, d), lambda i: (i, 0)),
                  pl.BlockSpec((1, d), lambda i: (0, 0))],
        out_specs=pl.BlockSpec((tr, d), lambda i: (i, 0)),
        out_shape=jax.ShapeDtypeStruct((s, d), out_dtype),
        compiler_params=_cparams("parallel"),
        name="rmsnorm",
    )(x, w.reshape(1, d))


def _rmsnorm_small_kernel(x_ref, w_ref, ws_ref, u_ref, ps_ref):
    x = x_ref[...]
    ms = jnp.mean(x * x, axis=-1, keepdims=True)
    u = (x * lax.rsqrt(ms + NORM_EPS) * w_ref[...]).astype(BF16)
    u_ref[...] = u
    ps_ref[...] = jnp.dot(u, ws_ref[...], preferred_element_type=F32)


def rmsnorm_small(x, w, w_small, tr=512):
    s, d = x.shape
    return pl.pallas_call(
        _rmsnorm_small_kernel,
        grid=(s // tr,),
        in_specs=[pl.BlockSpec((tr, d), lambda i: (i, 0)),
                  pl.BlockSpec((1, d), lambda i: (0, 0)),
                  pl.BlockSpec((d, LANES), lambda i: (0, 0))],
        out_specs=[pl.BlockSpec((tr, d), lambda i: (i, 0)),
                   pl.BlockSpec((tr, LANES), lambda i: (i, 0))],
        out_shape=[jax.ShapeDtypeStruct((s, d), BF16),
                   jax.ShapeDtypeStruct((s, LANES), F32)],
        compiler_params=_cparams("parallel"),
        name="rmsnorm_small",
    )(x, w.reshape(1, d), w_small)


RT_SEL = 0
RT_C1, RT_C2 = 8, 9
RT_I1, RT_I2 = 10, 11


def _router_kernel(x_ref, w_ref, rw_ref, rt_ref):
    x = x_ref[...]
    ms = jnp.mean(x * x, axis=-1, keepdims=True)
    u = x * lax.rsqrt(ms + NORM_EPS) * w_ref[...]
    logits = jnp.dot(u, rw_ref[...], preferred_element_type=F32, precision=lax.Precision.HIGHEST)
    lane = lax.broadcasted_iota(jnp.int32, logits.shape, 1)
    valid = lane < N_EXPERTS
    lg = jnp.where(valid, logits, -jnp.inf)
    ex = jnp.exp(lg - jnp.max(lg, axis=1, keepdims=True))
    probs = ex / jnp.sum(ex, axis=1, keepdims=True)
    probs = jnp.where(valid, probs, -1.0)
    p1 = jnp.max(probs, axis=1, keepdims=True)
    i1 = jnp.min(jnp.where(probs == p1, lane, LANES), axis=1, keepdims=True)
    rest = jnp.where(lane == i1, -1.0, probs)
    p2 = jnp.max(rest, axis=1, keepdims=True)
    i2 = jnp.min(jnp.where(rest == p2, lane, LANES), axis=1, keepdims=True)
    tot = p1 + p2
    rt = jnp.where((lane == i1) | (lane == i2), 1.0, 0.0)
    rt = jnp.where(lane == RT_C1, p1 / tot, rt)
    rt = jnp.where(lane == RT_C2, p2 / tot, rt)
    rt = jnp.where(lane == RT_I1, i1.astype(F32), rt)
    rt = jnp.where(lane == RT_I2, i2.astype(F32), rt)
    rt_ref[...] = rt


def router(x, w, router_w, tr=512):
    s, d = x.shape
    rw = jnp.pad(router_w, ((0, 0), (0, LANES - N_EXPERTS)))
    return pl.pallas_call(
        _router_kernel,
        grid=(s // tr,),
        in_specs=[pl.BlockSpec((tr, d), lambda i: (i, 0)),
                  pl.BlockSpec((1, d), lambda i: (0, 0)),
                  pl.BlockSpec((d, LANES), lambda i: (0, 0))],
        out_specs=pl.BlockSpec((tr, LANES), lambda i: (i, 0)),
        out_shape=jax.ShapeDtypeStruct((s, LANES), F32),
        compiler_params=_cparams("parallel"),
        name="router",
    )(x, w.reshape(1, d), rw)


MOE_TM = 512
UP_TF = 512
PLAN_BLK = 512


def _plan_kernel(rt_ref, dest_ref, meta_ref):
    s = rt_ref.shape[0]
    nblk = s // PLAN_BLK
    lane = lax.broadcasted_iota(jnp.int32, (PLAN_BLK, LANES), 1)
    is_exp = lane < N_EXPERTS
    r = lax.broadcasted_iota(jnp.int32, (PLAN_BLK, PLAN_BLK), 0)
    c = lax.broadcasted_iota(jnp.int32, (PLAN_BLK, PLAN_BLK), 1)
    strict_tril = (r > c).astype(BF16)

    def rank_body(b, carry):
        rows = pl.ds(pl.multiple_of(b * PLAN_BLK, PLAN_BLK), PLAN_BLK)
        sel = jnp.where(is_exp, rt_ref[rows, :], 0.0)
        rank = jnp.dot(strict_tril, sel.astype(BF16), preferred_element_type=F32) + carry
        dest_ref[rows, :] = rank.astype(jnp.int32)
        return carry + jnp.sum(sel, axis=0, keepdims=True)

    counts = lax.fori_loop(0, nblk, rank_body, jnp.zeros((1, LANES), F32))
    tiles = jnp.floor((counts + (MOE_TM - 1)) * (1.0 / MOE_TM))
    er = lax.broadcasted_iota(jnp.int32, (LANES, LANES), 0)
    ec = lax.broadcasted_iota(jnp.int32, (LANES, LANES), 1)
    before = (er < ec).astype(F32)
    tile_start = jnp.dot(jnp.broadcast_to(tiles, (SUBLANES, LANES)), before, preferred_element_type=F32,
                         precision=lax.Precision.HIGHEST)[0:1, :]
    row_off = tile_start * MOE_TM
    meta_ref[0:1, :] = counts.astype(jnp.int32)
    meta_ref[1:2, :] = tiles.astype(jnp.int32)
    meta_ref[2:3, :] = tile_start.astype(jnp.int32)
    meta_ref[3:SUBLANES, :] = jnp.zeros((SUBLANES - 3, LANES), jnp.int32)

    def dest_body(b, _):
        rows = pl.ds(pl.multiple_of(b * PLAN_BLK, PLAN_BLK), PLAN_BLK)
        rt = rt_ref[rows, :]
        pos = dest_ref[rows, :].astype(F32) + row_off
        i1 = rt[:, RT_I1:RT_I1 + 1].astype(jnp.int32)
        i2 = rt[:, RT_I2:RT_I2 + 1].astype(jnp.int32)
        d1 = jnp.sum(jnp.where(lane == i1, pos, 0.0), axis=1, keepdims=True)
        d2 = jnp.sum(jnp.where(lane == i2, pos, 0.0), axis=1, keepdims=True)
        dest_ref[rows, :] = jnp.where(lane == 0, d1, jnp.where(lane == 1, d2, 0.0)).astype(jnp.int32)
        return 0

    lax.fori_loop(0, nblk, dest_body, 0)


def moe_plan(rt):
    s = rt.shape[0]
    return pl.pallas_call(
        _plan_kernel,
        out_shape=[jax.ShapeDtypeStruct((s, LANES), jnp.int32),
                   jax.ShapeDtypeStruct((SUBLANES, LANES), jnp.int32)],
        compiler_params=pltpu.CompilerParams(vmem_limit_bytes=VMEM_LIMIT_BYTES),
        name="moe_plan",
    )(rt)


def _invert_kernel(d1_ref, d2_ref, src_ref):
    n_rows = src_ref.shape[0]
    n_tok = d1_ref.shape[0]

    def zero(i, _):
        src_ref[i] = 0
        return 0

    lax.fori_loop(0, n_rows, zero, 0, unroll=16)

    def put(t, _):
        src_ref[d1_ref[t]] = t
        src_ref[d2_ref[t]] = t
        return 0

    lax.fori_loop(0, n_tok, put, 0, unroll=8)


def moe_invert(d1, d2, n_rows):
    smem = pl.BlockSpec(memory_space=pltpu.SMEM)
    return pl.pallas_call(
        _invert_kernel,
        in_specs=[smem, smem],
        out_specs=smem,
        out_shape=jax.ShapeDtypeStruct((n_rows,), jnp.int32),
        name="moe_invert",
    )(d1, d2)


GATHER_TM = 256
GATHER_UNROLL = 8


def _gather_norm_kernel(src_ref, nt_ref, h_hbm, w_ref, o_ref, buf_ref, sem):
    i = pl.program_id(0)
    nt = nt_ref[0]

    def row_copy(tile, slot, r):
        return pltpu.make_async_copy(h_hbm.at[pl.ds(src_ref[tile * MOE_TM + r], 1), :],
                                     buf_ref.at[slot, pl.ds(r, 1), :], sem.at[slot])

    def issue(tile, slot):
        def body(r, _):
            row_copy(tile, slot, r).start()
            return 0
        lax.fori_loop(0, MOE_TM, body, 0, unroll=GATHER_UNROLL)

    @pl.when(i == 0)
    def _():
        issue(0, 0)

    @pl.when(i + 1 < nt)
    def _():
        issue(i + 1, (i + 1) % 2)

    @pl.when(i < nt)
    def _():
        slot = i % 2

        def body(r, _):
            row_copy(i, slot, r).wait()
            return 0
        lax.fori_loop(0, MOE_TM, body, 0, unroll=GATHER_UNROLL)
        x = buf_ref[slot]
        ms = jnp.mean(x * x, axis=-1, keepdims=True)
        o_ref[...] = (x * lax.rsqrt(ms + NORM_EPS) * w_ref[...]).astype(o_ref.dtype)

    @pl.when(i >= nt)
    def _():
        o_ref[...] = jnp.zeros(o_ref.shape, o_ref.dtype)


def moe_gather_norm(src, nt, h, w):
    n_rows = src.shape[0]
    d = h.shape[1]
    return pl.pallas_call(
        _gather_norm_kernel,
        grid_spec=pltpu.PrefetchScalarGridSpec(
            num_scalar_prefetch=2,
            grid=(n_rows // MOE_TM,),
            in_specs=[pl.BlockSpec(memory_space=pl.ANY),
                      pl.BlockSpec((1, d), lambda i, src, nt: (0, 0))],
            out_specs=pl.BlockSpec((MOE_TM, d), lambda i, src, nt: (i, 0)),
            scratch_shapes=[pltpu.VMEM((2, MOE_TM, d), F32), pltpu.SemaphoreType.DMA((2,))]),
        out_shape=jax.ShapeDtypeStruct((n_rows, d), BF16),
        compiler_params=_cparams("arbitrary"),
        name="moe_gather_norm",
    )(src, nt, h, w.reshape(1, d))


def _combine_kernel(d1_ref, d2_ref, h_ref, rt_ref, pw_ref, y_hbm, o_ref, ya_ref, yb_ref, sem, *, post_norm):
    i = pl.program_id(0)
    n = pl.num_programs(0)

    def copy_a(blk, slot, r):
        return pltpu.make_async_copy(y_hbm.at[pl.ds(d1_ref[blk * GATHER_TM + r], 1), :],
                                     ya_ref.at[slot, pl.ds(r, 1), :], sem.at[0, slot])

    def copy_b(blk, slot, r):
        return pltpu.make_async_copy(y_hbm.at[pl.ds(d2_ref[blk * GATHER_TM + r], 1), :],
                                     yb_ref.at[slot, pl.ds(r, 1), :], sem.at[1, slot])

    def issue(blk, slot):
        def body(r, _):
            copy_a(blk, slot, r).start()
            copy_b(blk, slot, r).start()
            return 0
        lax.fori_loop(0, GATHER_TM, body, 0, unroll=GATHER_UNROLL)

    @pl.when(i == 0)
    def _():
        issue(0, 0)

    @pl.when(i + 1 < n)
    def _():
        issue(i + 1, (i + 1) % 2)

    slot = i % 2

    def wait(r, _):
        copy_a(i, slot, r).wait()
        copy_b(i, slot, r).wait()
        return 0
    lax.fori_loop(0, GATHER_TM, wait, 0, unroll=GATHER_UNROLL)
    rt = rt_ref[...]
    out = h_ref[...] + rt[:, RT_C1:RT_C1 + 1] * ya_ref[slot] + rt[:, RT_C2:RT_C2 + 1] * yb_ref[slot]
    if post_norm:
        ms = jnp.mean(out * out, axis=-1, keepdims=True)
        out = out * lax.rsqrt(ms + NORM_EPS) * pw_ref[...]
    o_ref[...] = out


def moe_combine(d1, d2, h, rt, y, post_norm_w):
    s, d = h.shape
    post_norm = post_norm_w is not None
    pw = (post_norm_w if post_norm else jnp.ones((d,), F32)).reshape(1, d)
    return pl.pallas_call(
        functools.partial(_combine_kernel, post_norm=post_norm),
        grid_spec=pltpu.PrefetchScalarGridSpec(
            num_scalar_prefetch=2,
            grid=(s // GATHER_TM,),
            in_specs=[pl.BlockSpec((GATHER_TM, d), lambda i, a, b: (i, 0)),
                      pl.BlockSpec((GATHER_TM, LANES), lambda i, a, b: (i, 0)),
                      pl.BlockSpec((1, d), lambda i, a, b: (0, 0)),
                      pl.BlockSpec(memory_space=pl.ANY)],
            out_specs=pl.BlockSpec((GATHER_TM, d), lambda i, a, b: (i, 0)),
            scratch_shapes=[pltpu.VMEM((2, GATHER_TM, d), F32), pltpu.VMEM((2, GATHER_TM, d), F32),
                            pltpu.SemaphoreType.DMA((2, 2))]),
        out_shape=jax.ShapeDtypeStruct((s, d), F32),
        compiler_params=_cparams("arbitrary"),
        name="moe_combine",
    )(d1, d2, h, rt, pw, y)


def _mm_kernel(x_ref, w_ref, o_ref):
    o_ref[...] = jnp.dot(x_ref[...], w_ref[...], preferred_element_type=F32).astype(o_ref.dtype)


def matmul(x, w, out_dtype, tm, tn):
    m, k = x.shape
    n = w.shape[1]
    return pl.pallas_call(
        _mm_kernel,
        grid=(m // tm, n // tn),
        in_specs=[pl.BlockSpec((tm, k), lambda i, j: (i, 0)),
                  pl.BlockSpec((k, tn), lambda i, j: (0, j))],
        out_specs=pl.BlockSpec((tm, tn), lambda i, j: (i, j)),
        out_shape=jax.ShapeDtypeStruct((m, n), out_dtype),
        compiler_params=_cparams("parallel", "arbitrary"),
        name="in_proj",
    )(x, w)


def _out_proj_kernel(ya_ref, yb_ref, wa_ref, wb_ref, r_ref, o_ref):
    acc = jnp.dot(ya_ref[...], wa_ref[...], preferred_element_type=F32)
    acc += jnp.dot(yb_ref[...], wb_ref[...], preferred_element_type=F32)
    o_ref[...] = r_ref[...] + acc


def out_proj(y_ssd, y_ml, w_out, resid, tm=1024, tn=1024):
    m, ka = y_ssd.shape
    n = w_out.shape[1]
    return pl.pallas_call(
        _out_proj_kernel,
        grid=(m // tm, n // tn),
        in_specs=[pl.BlockSpec((tm, ka), lambda i, j: (i, 0)),
                  pl.BlockSpec((tm, ka), lambda i, j: (i, 0)),
                  pl.BlockSpec((ka, tn), lambda i, j: (0, j)),
                  pl.BlockSpec((ka, tn), lambda i, j: (1, j)),
                  pl.BlockSpec((tm, tn), lambda i, j: (i, j))],
        out_specs=pl.BlockSpec((tm, tn), lambda i, j: (i, j)),
        out_shape=jax.ShapeDtypeStruct((m, n), F32),
        compiler_params=_cparams("parallel", "arbitrary"),
        name="out_proj",
    )(y_ssd, y_ml, w_out, w_out, resid)


def _silu(x):
    return x * jax.nn.sigmoid(x)


class UpPlan(NamedTuple):
    counts: jax.Array
    run_of_tile: jax.Array
    last_of_run: jax.Array
    phase_expert: jax.Array
    phase_hidden: jax.Array


def _up_plan(tiles, n_tiles_max, n_hidden_tiles):
    ne = tiles.shape[0]
    tile_end = jnp.cumsum(tiles)
    nt = tile_end[-1]
    tid = jnp.minimum(jnp.arange(n_tiles_max, dtype=jnp.int32), nt - 1)
    eot = jnp.sum((tid[:, None] >= tile_end[None, :]).astype(jnp.int32), axis=1)
    has = tiles > 0
    n_runs = jnp.sum(has.astype(jnp.int32))
    run_of_expert = jnp.cumsum(has.astype(jnp.int32)) - 1
    run_expert = jnp.argsort(jnp.logical_not(has), stable=True).astype(jnp.int32)
    k = jnp.arange(ne * n_hidden_tiles, dtype=jnp.int32)
    plan = UpPlan(counts=jnp.stack([nt, n_runs, n_runs * n_hidden_tiles]).astype(jnp.int32),
                  run_of_tile=run_of_expert[eot],
                  last_of_run=(tid == tile_end[eot] - 1).astype(jnp.int32),
                  phase_expert=run_expert[k % n_runs],
                  phase_hidden=jnp.minimum(k // n_runs, n_hidden_tiles - 1))
    return plan, eot


def _grouped_up_kernel(cnt_ref, run_ref, last_ref, pe_ref, ph_ref, x_ref, wg_hbm, wu_hbm, o_ref,
                       wf_ref, wb_even_ref, wb_odd_ref, sem, *, tf):
    j, t = pl.program_id(0), pl.program_id(1)
    nt, n_runs, n_phases = cnt_ref[0], cnt_ref[1], cnt_ref[2]

    def fetch(p):
        cols = pl.ds(pl.multiple_of(ph_ref[p] * tf, tf), tf)
        return (pltpu.make_async_copy(wg_hbm.at[pe_ref[p], :, cols], wf_ref.at[0], sem.at[0]),
                pltpu.make_async_copy(wu_hbm.at[pe_ref[p], :, cols], wf_ref.at[1], sem.at[1]))

    def start(p):
        for c in fetch(p):
            c.start()

    def stage(p, dst_ref):
        for c in fetch(p):
            c.wait()
        dst_ref[0] = wf_ref[0].astype(BF16)
        dst_ref[1] = wf_ref[1].astype(BF16)

    def compute(w_ref):
        x = x_ref[...]
        g = jnp.dot(x, w_ref[0], preferred_element_type=F32)
        u = jnp.dot(x, w_ref[1], preferred_element_type=F32)
        o_ref[...] = (_silu(g) * u).astype(o_ref.dtype)

    active = t < nt
    p = j * n_runs + run_ref[t]
    hand_over = active & (last_ref[t] == 1) & (p + 1 < n_phases)

    @pl.when(active & (j == 0) & (t == 0))
    def _():
        start(0)
        stage(0, wb_even_ref)

        @pl.when(n_phases > 1)
        def _():
            start(1)

    def step(cur_ref, nxt_ref):
        @pl.when(jnp.logical_not(hand_over))
        def _():
            compute(cur_ref)

        @pl.when(hand_over)
        def _():
            stage(p + 1, nxt_ref)
            compute(cur_ref)

            @pl.when(p + 2 < n_phases)
            def _():
                start(p + 2)

    @pl.when(active & (p % 2 == 0))
    def _():
        step(wb_even_ref, wb_odd_ref)

    @pl.when(active & (p % 2 == 1))
    def _():
        step(wb_odd_ref, wb_even_ref)

    @pl.when(jnp.logical_not(active))
    def _():
        o_ref[...] = jnp.zeros(o_ref.shape, o_ref.dtype)


def grouped_swiglu_up(plan, xs, wg, wu, tf=512):
    r, k = xs.shape
    f = wg.shape[2]
    row = lambda j, t, cnt, *_: (jnp.maximum(jnp.minimum(t, cnt[0] - 1), 0), 0)
    return pl.pallas_call(
        functools.partial(_grouped_up_kernel, tf=tf),
        grid_spec=pltpu.PrefetchScalarGridSpec(
            num_scalar_prefetch=5,
            grid=(f // tf, r // MOE_TM),
            in_specs=[pl.BlockSpec((MOE_TM, k), row),
                      pl.BlockSpec(memory_space=pl.ANY),
                      pl.BlockSpec(memory_space=pl.ANY)],
            out_specs=pl.BlockSpec((MOE_TM, tf), lambda j, t, *_: (t, j)),
            scratch_shapes=[pltpu.VMEM((2, k, tf), F32), pltpu.VMEM((2, k, tf), BF16),
                            pltpu.VMEM((2, k, tf), BF16), pltpu.SemaphoreType.DMA((2,))]),
        out_shape=jax.ShapeDtypeStruct((r, f), BF16),
        compiler_params=_cparams("arbitrary", "arbitrary"),
        name="grouped_swiglu_up",
    )(*plan, xs, wg, wu)


def _moe_down_kernel(eot_ref, nt_ref, a_ref, w_ref, o_ref):
    @pl.when(pl.program_id(1) < nt_ref[0])
    def _():
        o_ref[...] = jnp.dot(a_ref[...], w_ref[...], preferred_element_type=F32)

    @pl.when(pl.program_id(1) >= nt_ref[0])
    def _():
        o_ref[...] = jnp.zeros(o_ref.shape, o_ref.dtype)


def moe_down(eot, nt, a, wd, tn=1024):
    r, f = a.shape
    n = wd.shape[2]
    return pl.pallas_call(
        _moe_down_kernel,
        grid_spec=pltpu.PrefetchScalarGridSpec(
            num_scalar_prefetch=2,
            grid=(n // tn, r // MOE_TM),
            in_specs=[pl.BlockSpec((MOE_TM, f), lambda j, t, eot, nt: (jnp.maximum(jnp.minimum(t, nt[0] - 1), 0), 0)),
                      pl.BlockSpec((None, f, tn), lambda j, t, eot, nt: (eot[t], 0, j))],
            out_specs=pl.BlockSpec((MOE_TM, tn), lambda j, t, eot, nt: (t, j))),
        out_shape=jax.ShapeDtypeStruct((r, n), F32),
        compiler_params=_cparams("arbitrary", "arbitrary"),
        name="moe_down",
    )(eot, nt, a, wd)


def moe_swiglu(h, norm_w, router_w, wg, wu, wd, post_norm_w=None):
    s, d = h.shape
    n_tiles_max = (2 * s) // MOE_TM + N_EXPERTS
    n_rows = n_tiles_max * MOE_TM
    rt = router(h, norm_w, router_w)
    dest, meta = moe_plan(rt)
    d1, d2 = dest[:, 0], dest[:, 1]
    plan, eot = _up_plan(meta[1, :N_EXPERTS], n_tiles_max, wg.shape[2] // UP_TF)
    nt = plan.counts[0:1]
    src = moe_invert(d1, d2, n_rows)
    xs = moe_gather_norm(src, nt, h, norm_w)
    a = grouped_swiglu_up(plan, xs, wg, wu, UP_TF)
    y = moe_down(eot, nt, a, wd)
    return moe_combine(d1, d2, h, rt, y, post_norm_w)


def _down_kernel(a_ref, w_ref, r_ref, o_ref):
    kk = pl.program_id(2)

    @pl.when(kk == 0)
    def _():
        o_ref[...] = r_ref[...]

    o_ref[...] += jnp.dot(a_ref[...], w_ref[...], preferred_element_type=F32)


def down_proj(a, w, resid, tk, tm=1024, tn=1024):
    m, k = a.shape
    n = w.shape[1]
    return pl.pallas_call(
        _down_kernel,
        grid=(m // tm, n // tn, k // tk),
        in_specs=[pl.BlockSpec((tm, tk), lambda i, j, kk: (i, kk)),
                  pl.BlockSpec((tk, tn), lambda i, j, kk: (kk, j)),
                  pl.BlockSpec((tm, tn), lambda i, j, kk: (i, j))],
        out_specs=pl.BlockSpec((tm, tn), lambda i, j, kk: (i, j)),
        out_shape=jax.ShapeDtypeStruct((m, n), F32),
        compiler_params=_cparams("parallel", "arbitrary", "arbitrary"),
        name="down_proj",
    )(a, w, resid)


def _softplus(x):
    return jnp.maximum(x, 0.0) + jnp.log1p(jnp.exp(-jnp.abs(x)))


def _tril_ones(n):
    r = lax.broadcasted_iota(jnp.int32, (n, n), 0)
    c = lax.broadcasted_iota(jnp.int32, (n, n), 1)
    return r >= c


def _ssd_kernel(z_ref, x_ref, b_ref, c_ref, sm_ref, convw_ref, convb_ref, smb_ref, alog_ref,
                dskip_ref, nw_ref, expand_ref, o_ref, tail_ref, state_ref):
    L = SSD_CHUNK
    HI = lax.Precision.HIGHEST

    @pl.when(pl.program_id(0) == 0)
    def _():
        tail_ref[...] = jnp.zeros(tail_ref.shape, F32)
        state_ref[...] = jnp.zeros(state_ref.shape, F32)

    row8 = lax.broadcasted_iota(jnp.int32, (SUBLANES, LANES), 0)

    def conv_silu(u, col0):
        w = u.shape[1]
        cols = slice(col0, col0 + w)
        tail = tail_ref[:, cols]
        acc = convb_ref[:, cols] + u * convw_ref[SSD_CONV - 1:SSD_CONV, cols]
        for k in range(1, SSD_CONV):
            rolled = pltpu.roll(u, k, axis=0)
            fix = jnp.where(jnp.tile(row8, (1, w // LANES)) < k, pltpu.roll(tail, k, axis=0), rolled[0:SUBLANES])
            shifted = jnp.concatenate([fix, rolled[SUBLANES:]], axis=0)
            acc = acc + shifted * convw_ref[SSD_CONV - 1 - k:SSD_CONV - k, cols]
        tail_ref[:, cols] = u[L - SUBLANES:L]
        return _silu(acc)

    xs = conv_silu(x_ref[...], 0)
    bmat = conv_silu(b_ref[...], SSD_INNER)
    cmat = conv_silu(c_ref[...], SSD_INNER + SSD_GROUPS * SSD_STATE)

    lane = lax.broadcasted_iota(jnp.int32, (L, LANES), 1)

    def expand_heads(v):
        hi = v.astype(BF16).astype(F32)
        r1 = v - hi
        mid = r1.astype(BF16).astype(F32)
        lo = r1 - mid
        packed = jnp.where(lane < 32, hi,
                           jnp.where(lane < 64, pltpu.roll(mid, 32, axis=1),
                                     jnp.where(lane < 96, pltpu.roll(lo, 64, axis=1), 0.0)))
        return jnp.dot(packed.astype(BF16), expand_ref[...], preferred_element_type=F32)

    causal = _tril_ones(L)
    dt = jnp.where(lane < SSD_HEADS, _softplus(sm_ref[...] + smb_ref[...]), 0.0)
    da = dt * (-jnp.exp(alog_ref[...]))
    acs = jnp.dot(causal.astype(F32), da, preferred_element_type=F32, precision=HI)
    acs_t = acs.T
    a_last = acs[L - 1:L, :]
    dt_e = expand_heads(dt)
    eacs_e = expand_heads(jnp.exp(acs))
    dst_e = expand_heads(jnp.exp(a_last - acs))
    xdt = xs * dt_e
    xdec = (xdt * dst_e).astype(BF16)
    xdt_b = xdt.astype(BF16)
    chunk_decay_e = eacs_e[L - 1:L, :]

    y = xs * dskip_ref[...]
    zgate = _silu(z_ref[...])
    lo_half = lane < SSD_HEAD_DIM

    def group(g):
        gs = slice(g * SSD_GROUP_WIDTH, (g + 1) * SSD_GROUP_WIDTH)
        bg = bmat[:, g * SSD_STATE:(g + 1) * SSD_STATE]
        cg = cmat[:, g * SSD_STATE:(g + 1) * SSD_STATE]
        cg_b = cg.astype(BF16)
        cb = lax.dot_general(cg_b, bg.astype(BF16), (((1,), (1,)), ((), ())),
                             preferred_element_type=F32)
        h_enter = state_ref[g]
        y_off = jnp.dot(cg_b, h_enter.astype(BF16), preferred_element_type=F32)
        st_new = jnp.dot(bg.T.astype(BF16), xdec[:, gs], preferred_element_type=F32)
        state_ref[g] = chunk_decay_e[:, gs] * h_enter + st_new
        yd_parts = []
        for p in range(4):
            j0 = g * 8 + 2 * p
            ms = []
            for j in (j0, j0 + 1):
                seg = jnp.where(causal, acs[:, j:j + 1] - acs_t[j:j + 1, :], -jnp.inf)
                ms.append((cb * jnp.exp(seg)).astype(BF16))
            xp = xdt_b[:, g * SSD_GROUP_WIDTH + p * LANES:g * SSD_GROUP_WIDTH + (p + 1) * LANES]
            zero = jnp.zeros_like(xp)
            rhs = jnp.concatenate([jnp.where(lo_half, xp, zero), jnp.where(lo_half, zero, xp)], axis=0)
            yd_parts.append(jnp.dot(jnp.concatenate(ms, axis=1), rhs, preferred_element_type=F32))
        y_diag = jnp.concatenate(yd_parts, axis=1)
        yg = (y[:, gs] + y_diag + y_off * eacs_e[:, gs]) * zgate[:, gs]
        msq = jnp.mean(yg * yg, axis=1, keepdims=True)
        o_ref[:, gs] = (yg * lax.rsqrt(msq + NORM_EPS) * nw_ref[:, gs]).astype(o_ref.dtype)

    return group


def _log_sigmoid(x):
    return jnp.minimum(x, 0.0) - jnp.log1p(jnp.exp(-jnp.abs(x)))


def _mlstm_kernel(q_ref, k_ref, v_ref, o_ref, sm_ref, smb_ref, nw_ref, out_ref, c_ref, n_ref, m_ref):
    L = MLSTM_CHUNK
    HI = lax.Precision.HIGHEST

    @pl.when(pl.program_id(0) == 0)
    def _():
        c_ref[...] = jnp.zeros(c_ref.shape, F32)
        n_ref[...] = jnp.zeros(n_ref.shape, F32)
        m_ref[...] = jnp.zeros(m_ref.shape, F32)

    causal = _tril_ones(L)
    sm = sm_ref[...] + smb_ref[...]
    bcs = jnp.dot(causal.astype(F32), _log_sigmoid(sm), preferred_element_type=F32, precision=HI)
    sm_t = sm.T
    bcs_t = bcs.T

    def head(h):
        li_col = sm[:, SMALL_I + h:SMALL_I + h + 1]
        b_col = bcs[:, SMALL_F + h:SMALL_F + h + 1]
        r_row = sm_t[SMALL_I + h:SMALL_I + h + 1, :] - bcs_t[SMALL_F + h:SMALL_F + h + 1, :]
        m_prev = m_ref[h:h + 1, 0:1]
        inter = b_col + m_prev
        dm = jnp.where(causal, b_col + r_row, -jnp.inf)
        m_t = jnp.maximum(inter, jnp.max(dm, axis=1, keepdims=True))
        w_intra = jnp.exp(dm - m_t)
        w_inter = jnp.exp(inter - m_t)
        qf = q_ref[:, h * MLSTM_DQK:(h + 1) * MLSTM_DQK] * (MLSTM_DQK ** -0.5)
        kf = k_ref[:, h * MLSTM_DQK:(h + 1) * MLSTM_DQK]
        qb = qf.astype(BF16)
        vb = v_ref[:, h * MLSTM_DV:(h + 1) * MLSTM_DV].astype(BF16)
        sqk = lax.dot_general(qb, kf.astype(BF16), (((1,), (1,)), ((), ())),
                              preferred_element_type=F32) * w_intra
        c_st = c_ref[h]
        n_row = n_ref[h:h + 1, :]
        num = (w_inter * jnp.dot(qb, c_st.astype(BF16), preferred_element_type=F32)
               + jnp.dot(sqk.astype(BF16), vb, preferred_element_type=F32))
        qn = jnp.sum(qf * n_row, axis=1, keepdims=True)
        den = w_inter * qn + jnp.sum(sqk, axis=1, keepdims=True)
        hout = num / jnp.maximum(jnp.abs(den), jnp.exp(-m_t))
        b_last = b_col[L - 1:L, :]
        g_col = b_last - b_col + li_col
        m_new = jnp.maximum(b_last + m_prev, jnp.max(g_col, axis=0, keepdims=True))
        decay = jnp.exp(b_last + m_prev - m_new)
        kw = kf * jnp.exp(g_col - m_new)
        c_ref[h] = decay * c_st + jnp.dot(kw.T.astype(BF16), vb, preferred_element_type=F32)
        n_ref[h:h + 1, :] = decay * n_row + jnp.sum(kw, axis=0, keepdims=True)
        m_ref[h:h + 1, :] = jnp.broadcast_to(m_new, (1, LANES))
        vs = slice(h * MLSTM_DV, (h + 1) * MLSTM_DV)
        msq = jnp.mean(hout * hout, axis=1, keepdims=True)
        hn = hout * lax.rsqrt(msq + NORM_EPS) * nw_ref[:, vs]
        out_ref[:, vs] = (jax.nn.sigmoid(o_ref[:, vs]) * hn).astype(out_ref.dtype)

    return head


assert MLSTM_CHUNK == SSD_CHUNK


def _mixer_kernel(z_ref, x_ref, b_ref, c_ref, q_ref, k_ref, v_ref, og_ref, sm_ref,
                  convw_ref, convb_ref, smb_ref, alog_ref, dskip_ref, ssd_nw_ref, expand_ref, ml_nw_ref,
                  y_ssd_ref, y_ml_ref, tail_ref, state_ref, cst_ref, nst_ref, mst_ref):
    ssd_group = _ssd_kernel(z_ref, x_ref, b_ref, c_ref, sm_ref, convw_ref, convb_ref, smb_ref, alog_ref,
                            dskip_ref, ssd_nw_ref, expand_ref, y_ssd_ref, tail_ref, state_ref)
    for g in range(SSD_GROUPS):
        ssd_group(g)
    mlstm_head = _mlstm_kernel(q_ref, k_ref, v_ref, og_ref, sm_ref, smb_ref, ml_nw_ref, y_ml_ref,
                               cst_ref, nst_ref, mst_ref)
    for h in range(MLSTM_HEADS):
        mlstm_head(h)


def hybrid_mixer(proj_a, proj_b, proj_s, conv_w, conv_b, small_bias, a_log, d_skip, ssd_norm_w, ml_norm_w):
    s = proj_a.shape[0]
    L = SSD_CHUNK
    conv_dim = conv_w.shape[1]
    qk_w = MLSTM_HEADS * MLSTM_DQK
    alog = jnp.pad(a_log, (0, LANES - SSD_HEADS)).reshape(1, LANES)
    dskip_e = jnp.repeat(d_skip, SSD_HEAD_DIM).reshape(1, SSD_INNER)
    piece_row = jnp.arange(LANES)[:, None]
    expand = ((jnp.arange(SSD_INNER)[None, :] // SSD_HEAD_DIM == piece_row % SSD_HEADS)
              & (piece_row < 3 * SSD_HEADS)).astype(BF16)
    full = lambda shp: pl.BlockSpec(shp, lambda i: (0,) * len(shp))
    blk = lambda w, col: pl.BlockSpec((L, w), lambda i: (i, col // w))
    return pl.pallas_call(
        _mixer_kernel,
        grid=(s // L,),
        in_specs=[blk(SSD_INNER, PA_Z), blk(SSD_INNER, PA_X), blk(512, PA_B), blk(512, PA_C),
                  blk(qk_w, PB_Q), blk(qk_w, PB_K), blk(MLSTM_WIDTH, PB_V), blk(MLSTM_WIDTH, PB_O),
                  blk(LANES, 0),
                  full((SSD_CONV, conv_dim)), full((1, conv_dim)), full((1, LANES)), full((1, LANES)),
                  full((1, SSD_INNER)), full((1, SSD_INNER)), full((LANES, SSD_INNER)),
                  full((1, MLSTM_WIDTH))],
        out_specs=[pl.BlockSpec((L, SSD_INNER), lambda i: (i, 0)),
                   pl.BlockSpec((L, MLSTM_WIDTH), lambda i: (i, 0))],
        out_shape=[jax.ShapeDtypeStruct((s, SSD_INNER), BF16),
                   jax.ShapeDtypeStruct((s, MLSTM_WIDTH), BF16)],
        scratch_shapes=[pltpu.VMEM((SUBLANES, conv_dim), F32),
                        pltpu.VMEM((SSD_GROUPS, SSD_STATE, SSD_GROUP_WIDTH), F32),
                        pltpu.VMEM((MLSTM_HEADS, MLSTM_DQK, MLSTM_DV), F32),
                        pltpu.VMEM((SUBLANES, MLSTM_DQK), F32),
                        pltpu.VMEM((SUBLANES, LANES), F32)],
        compiler_params=_cparams("arbitrary"),
        name="hybrid_mixer",
    )(proj_a, proj_a, proj_a, proj_a, proj_b, proj_b, proj_b, proj_b, proj_s,
      conv_w, conv_b.reshape(1, conv_dim), small_bias, alog, dskip_e,
      ssd_norm_w.reshape(1, SSD_INNER), expand, ml_norm_w.reshape(1, MLSTM_WIDTH))


def _split_w_in(w_in):
    w_a = w_in[:, :PA_WIDTH].astype(BF16)
    w_b = w_in[:, PB_START:PB_START + PB_WIDTH].astype(BF16)
    w_s = jnp.concatenate([w_in[:, IN_DT:IN_DT + SSD_HEADS], w_in[:, IN_I:IN_I + MLSTM_HEADS],
                           w_in[:, IN_F:IN_F + MLSTM_HEADS],
                           jnp.zeros((w_in.shape[0], LANES - SSD_HEADS - 2 * MLSTM_HEADS), w_in.dtype)],
                          axis=1).astype(BF16)
    return w_a, w_b, w_s


def _small_bias(dt_bias, i_bias, f_bias):
    v = jnp.concatenate([dt_bias, i_bias, f_bias, jnp.zeros((LANES - 40,), F32)])
    return v.reshape(1, LANES)


def kernel(x, norm_mix_w, w_in, conv_w, conv_b, dt_bias, a_log, d_skip, ssd_norm_w, mlstm_i_bias,
           mlstm_f_bias, mlstm_norm_w, w_out, norm_ffn_w, ffn_w_gate, ffn_w_up, ffn_w_down, router_w,
           moe_w_gate, moe_w_up, moe_w_down, final_norm_w):
    bsz, s, d = x.shape
    depth = w_in.shape[0]
    outs = []
    for b in range(bsz):
        h = x[b]
        normed = False
        for layer in range(depth):
            w_a, w_b, w_s = _split_w_in(w_in[layer])
            u, proj_s = rmsnorm_small(h, norm_mix_w[layer], w_s)
            proj_a = matmul(u, w_a, F32, tm=1024, tn=PA_TN)
            proj_b = matmul(u, w_b, F32, tm=1024, tn=PB_TN)
            sbias = _small_bias(dt_bias[layer], mlstm_i_bias[layer], mlstm_f_bias[layer])
            y_ssd, y_ml = hybrid_mixer(proj_a, proj_b, proj_s, conv_w[layer], conv_b[layer], sbias,
                                       a_log[layer], d_skip[layer], ssd_norm_w[layer], mlstm_norm_w[layer])
            h = out_proj(y_ssd, y_ml, w_out[layer].astype(BF16), h)
            j = layer // 2
            if layer % 2 == 0:
                u = rmsnorm(h, norm_ffn_w[layer], BF16)
                plan, _ = _up_plan(jnp.full((1,), s // MOE_TM, jnp.int32), s // MOE_TM, FF_DENSE // UP_TF)
                a = grouped_swiglu_up(plan, u, ffn_w_gate[j][None], ffn_w_up[j][None], UP_TF)
                h = down_proj(a, ffn_w_down[j].astype(BF16), h, tk=2048)
            else:
                normed = layer == depth - 1
                h = moe_swiglu(h, norm_ffn_w[layer], router_w[j], moe_w_gate[j], moe_w_up[j],
                               moe_w_down[j].astype(BF16), final_norm_w if normed else None)
        outs.append(h if normed else rmsnorm(h, final_norm_w, F32))
    return jnp.stack(outs, axis=0)
```

```python
import functools
from typing import NamedTuple

import jax
import jax.numpy as jnp
from jax import lax
from jax.experimental import pallas as pl
from jax.experimental.pallas import tpu as pltpu

F32 = jnp.float32
BF16 = jnp.bfloat16

D_MODEL = 4096
SSD_INNER = 2048
SSD_HEAD_DIM = 64
SSD_HEADS = 32
SSD_GROUPS = 4
SSD_STATE = 128
SSD_CONV = 4
SSD_CHUNK = 128
SSD_GROUP_WIDTH = SSD_INNER // SSD_GROUPS
MLSTM_WIDTH = 2048
MLSTM_HEADS = 4
MLSTM_DV = 512
MLSTM_DQK = 256
FF_DENSE = 14336
N_EXPERTS = 8
FF_EXPERT = 5632
NORM_EPS = 1e-6

LANES = 128
SUBLANES = 8
VMEM_LIMIT_BYTES = 56 * 1024 * 1024

PA_WIDTH, PA_TN = 5120, 1280
PA_Z, PA_X, PA_B, PA_C = 0, 2048, 4096, 4608
PB_START, PB_WIDTH, PB_TN = 5152, 6144, 1024
PB_Q, PB_K, PB_V, PB_O = 0, 1024, 2048, 4096
IN_DT, IN_I, IN_F = 5120, 11296, 11300
SMALL_DT, SMALL_I, SMALL_F = 0, 32, 36

MLSTM_CHUNK = 128


def _cparams(*sem):
    return pltpu.CompilerParams(dimension_semantics=sem, vmem_limit_bytes=VMEM_LIMIT_BYTES)


def _rmsnorm_kernel(x_ref, w_ref, o_ref):
    x = x_ref[...]
    ms = jnp.mean(x * x, axis=-1, keepdims=True)
    o_ref[...] = (x * lax.rsqrt(ms + NORM_EPS) * w_ref[...]).astype(o_ref.dtype)


def rmsnorm(x, w, out_dtype, tr=512):
    s, d = x.shape
    return pl.pallas_call(
        _rmsnorm_kernel,
        grid=(s // tr,),
        in_specs=[pl.BlockSpec((tr, d), lambda i: (i, 0)),
                  pl.BlockSpec((1, d), lambda i: (0, 0))],
        out_specs=pl.BlockSpec((tr, d), lambda i: (i, 0)),
        out_shape=jax.ShapeDtypeStruct((s, d), out_dtype),
        compiler_params=_cparams("parallel"),
        name="rmsnorm",
    )(x, w.reshape(1, d))


def _rmsnorm_small_kernel(x_ref, w_ref, ws_ref, u_ref, ps_ref):
    x = x_ref[...]
    ms = jnp.mean(x * x, axis=-1, keepdims=True)
    u = (x * lax.rsqrt(ms + NORM_EPS) * w_ref[...]).astype(BF16)
    u_ref[...] = u
    ps_ref[...] = jnp.dot(u, ws_ref[...], preferred_element_type=F32)


def rmsnorm_small(x, w, w_small, tr=512):
    s, d = x.shape
    return pl.pallas_call(
        _rmsnorm_small_kernel,
        grid=(s // tr,),
        in_specs=[pl.BlockSpec((tr, d), lambda i: (i, 0)),
                  pl.BlockSpec((1, d), lambda i: (0, 0)),
                  pl.BlockSpec((d, LANES), lambda i: (0, 0))],
        out_specs=[pl.BlockSpec((tr, d), lambda i: (i, 0)),
                   pl.BlockSpec((tr, LANES), lambda i: (i, 0))],
        out_shape=[jax.ShapeDtypeStruct((s, d), BF16),
                   jax.ShapeDtypeStruct((s, LANES), F32)],
        compiler_params=_cparams("parallel"),
        name="rmsnorm_small",
    )(x, w.reshape(1, d), w_small)


RT_SEL = 0
RT_C1, RT_C2 = 8, 9
RT_I1, RT_I2 = 10, 11


def _router_kernel(x_ref, w_ref, rw_ref, rt_ref):
    x = x_ref[...]
    ms = jnp.mean(x * x, axis=-1, keepdims=True)
    u = x * lax.rsqrt(ms + NORM_EPS) * w_ref[...]
    logits = jnp.dot(u, rw_ref[...], preferred_element_type=F32, precision=lax.Precision.HIGHEST)
    lane = lax.broadcasted_iota(jnp.int32, logits.shape, 1)
    valid = lane < N_EXPERTS
    lg = jnp.where(valid, logits, -jnp.inf)
    ex = jnp.exp(lg - jnp.max(lg, axis=1, keepdims=True))
    probs = ex / jnp.sum(ex, axis=1, keepdims=True)
    probs = jnp.where(valid, probs, -1.0)
    p1 = jnp.max(probs, axis=1, keepdims=True)
    i1 = jnp.min(jnp.where(probs == p1, lane, LANES), axis=1, keepdims=True)
    rest = jnp.where(lane == i1, -1.0, probs)
    p2 = jnp.max(rest, axis=1, keepdims=True)
    i2 = jnp.min(jnp.where(rest == p2, lane, LANES), axis=1, keepdims=True)
    tot = p1 + p2
    rt = jnp.where((lane == i1) | (lane == i2), 1.0, 0.0)
    rt = jnp.where(lane == RT_C1, p1 / tot, rt)
    rt = jnp.where(lane == RT_C2, p2 / tot, rt)
    rt = jnp.where(lane == RT_I1, i1.astype(F32), rt)
    rt = jnp.where(lane == RT_I2, i2.astype(F32), rt)
    rt_ref[...] = rt


def router(x, w, router_w, tr=512):
    s, d = x.shape
    rw = jnp.pad(router_w, ((0, 0), (0, LANES - N_EXPERTS)))
    return pl.pallas_call(
        _router_kernel,
        grid=(s // tr,),
        in_specs=[pl.BlockSpec((tr, d), lambda i: (i, 0)),
                  pl.BlockSpec((1, d), lambda i: (0, 0)),
                  pl.BlockSpec((d, LANES), lambda i: (0, 0))],
        out_specs=pl.BlockSpec((tr, LANES), lambda i: (i, 0)),
        out_shape=jax.ShapeDtypeStruct((s, LANES), F32),
        compiler_params=_cparams("parallel"),
        name="router",
    )(x, w.reshape(1, d), rw)


MOE_TM = 512
UP_TF = 512
PLAN_BLK = 512


def _plan_kernel(rt_ref, dest_ref, meta_ref):
    s = rt_ref.shape[0]
    nblk = s // PLAN_BLK
    lane = lax.broadcasted_iota(jnp.int32, (PLAN_BLK, LANES), 1)
    is_exp = lane < N_EXPERTS
    r = lax.broadcasted_iota(jnp.int32, (PLAN_BLK, PLAN_BLK), 0)
    c = lax.broadcasted_iota(jnp.int32, (PLAN_BLK, PLAN_BLK), 1)
    strict_tril = (r > c).astype(BF16)

    def rank_body(b, carry):
        rows = pl.ds(pl.multiple_of(b * PLAN_BLK, PLAN_BLK), PLAN_BLK)
        sel = jnp.where(is_exp, rt_ref[rows, :], 0.0)
        rank = jnp.dot(strict_tril, sel.astype(BF16), preferred_element_type=F32) + carry
        dest_ref[rows, :] = rank.astype(jnp.int32)
        return carry + jnp.sum(sel, axis=0, keepdims=True)

    counts = lax.fori_loop(0, nblk, rank_body, jnp.zeros((1, LANES), F32))
    tiles = jnp.floor((counts + (MOE_TM - 1)) * (1.0 / MOE_TM))
    er = lax.broadcasted_iota(jnp.int32, (LANES, LANES), 0)
    ec = lax.broadcasted_iota(jnp.int32, (LANES, LANES), 1)
    before = (er < ec).astype(F32)
    tile_start = jnp.dot(jnp.broadcast_to(tiles, (SUBLANES, LANES)), before, preferred_element_type=F32,
                         precision=lax.Precision.HIGHEST)[0:1, :]
    row_off = tile_start * MOE_TM
    meta_ref[0:1, :] = counts.astype(jnp.int32)
    meta_ref[1:2, :] = tiles.astype(jnp.int32)
    meta_ref[2:3, :] = tile_start.astype(jnp.int32)
    meta_ref[3:SUBLANES, :] = jnp.zeros((SUBLANES - 3, LANES), jnp.int32)

    def dest_body(b, _):
        rows = pl.ds(pl.multiple_of(b * PLAN_BLK, PLAN_BLK), PLAN_BLK)
        rt = rt_ref[rows, :]
        pos = dest_ref[rows, :].astype(F32) + row_off
        i1 = rt[:, RT_I1:RT_I1 + 1].astype(jnp.int32)
        i2 = rt[:, RT_I2:RT_I2 + 1].astype(jnp.int32)
        d1 = jnp.sum(jnp.where(lane == i1, pos, 0.0), axis=1, keepdims=True)
        d2 = jnp.sum(jnp.where(lane == i2, pos, 0.0), axis=1, keepdims=True)
        dest_ref[rows, :] = jnp.where(lane == 0, d1, jnp.where(lane == 1, d2, 0.0)).astype(jnp.int32)
        return 0

    lax.fori_loop(0, nblk, dest_body, 0)


def moe_plan(rt):
    s = rt.shape[0]
    return pl.pallas_call(
        _plan_kernel,
        out_shape=[jax.ShapeDtypeStruct((s, LANES), jnp.int32),
                   jax.ShapeDtypeStruct((SUBLANES, LANES), jnp.int32)],
        compiler_params=pltpu.CompilerParams(vmem_limit_bytes=VMEM_LIMIT_BYTES),
        name="moe_plan",
    )(rt)


def _invert_kernel(d1_ref, d2_ref, src_ref):
    n_rows = src_ref.shape[0]
    n_tok = d1_ref.shape[0]

    def zero(i, _):
        src_ref[i] = 0
        return 0

    lax.fori_loop(0, n_rows, zero, 0, unroll=16)

    def put(t, _):
        src_ref[d1_ref[t]] = t
        src_ref[d2_ref[t]] = t
        return 0

    lax.fori_loop(0, n_tok, put, 0, unroll=8)


def moe_invert(d1, d2, n_rows):
    smem = pl.BlockSpec(memory_space=pltpu.SMEM)
    return pl.pallas_call(
        _invert_kernel,
        in_specs=[smem, smem],
        out_specs=smem,
        out_shape=jax.ShapeDtypeStruct((n_rows,), jnp.int32),
        name="moe_invert",
    )(d1, d2)


GATHER_TM = 256
GATHER_UNROLL = 8


def _gather_norm_kernel(src_ref, nt_ref, h_hbm, w_ref, o_ref, buf_ref, sem):
    i = pl.program_id(0)
    nt = nt_ref[0]

    def row_copy(tile, slot, r):
        return pltpu.make_async_copy(h_hbm.at[pl.ds(src_ref[tile * MOE_TM + r], 1), :],
                                     buf_ref.at[slot, pl.ds(r, 1), :], sem.at[slot])

    def issue(tile, slot):
        def body(r, _):
            row_copy(tile, slot, r).start()
            return 0
        lax.fori_loop(0, MOE_TM, body, 0, unroll=GATHER_UNROLL)

    @pl.when(i == 0)
    def _():
        issue(0, 0)

    @pl.when(i + 1 < nt)
    def _():
        issue(i + 1, (i + 1) % 2)

    @pl.when(i < nt)
    def _():
        slot = i % 2

        pltpu.make_async_copy(h_hbm.at[pl.ds(0, MOE_TM), :], buf_ref.at[slot], sem.at[slot]).wait()
        x = buf_ref[slot]
        ms = jnp.mean(x * x, axis=-1, keepdims=True)
        o_ref[...] = (x * lax.rsqrt(ms + NORM_EPS) * w_ref[...]).astype(o_ref.dtype)

    @pl.when(i >= nt)
    def _():
        o_ref[...] = jnp.zeros(o_ref.shape, o_ref.dtype)


def moe_gather_norm(src, nt, h, w):
    n_rows = src.shape[0]
    d = h.shape[1]
    return pl.pallas_call(
        _gather_norm_kernel,
        grid_spec=pltpu.PrefetchScalarGridSpec(
            num_scalar_prefetch=2,
            grid=(n_rows // MOE_TM,),
            in_specs=[pl.BlockSpec(memory_space=pl.ANY),
                      pl.BlockSpec((1, d), lambda i, src, nt: (0, 0))],
            out_specs=pl.BlockSpec((MOE_TM, d), lambda i, src, nt: (i, 0)),
            scratch_shapes=[pltpu.VMEM((2, MOE_TM, d), F32), pltpu.SemaphoreType.DMA((2,))]),
        out_shape=jax.ShapeDtypeStruct((n_rows, d), BF16),
        compiler_params=_cparams("arbitrary"),
        name="moe_gather_norm",
    )(src, nt, h, w.reshape(1, d))


def _combine_kernel(d1_ref, d2_ref, h_ref, rt_ref, pw_ref, y_hbm, o_ref, ya_ref, yb_ref, sem, *, post_norm):
    i = pl.program_id(0)
    n = pl.num_programs(0)

    def copy_a(blk, slot, r):
        return pltpu.make_async_copy(y_hbm.at[pl.ds(d1_ref[blk * GATHER_TM + r], 1), :],
                                     ya_ref.at[slot, pl.ds(r, 1), :], sem.at[0, slot])

    def copy_b(blk, slot, r):
        return pltpu.make_async_copy(y_hbm.at[pl.ds(d2_ref[blk * GATHER_TM + r], 1), :],
                                     yb_ref.at[slot, pl.ds(r, 1), :], sem.at[1, slot])

    def issue(blk, slot):
        def body(r, _):
            copy_a(blk, slot, r).start()
            copy_b(blk, slot, r).start()
            return 0
        lax.fori_loop(0, GATHER_TM, body, 0, unroll=GATHER_UNROLL)

    @pl.when(i == 0)
    def _():
        issue(0, 0)

    @pl.when(i + 1 < n)
    def _():
        issue(i + 1, (i + 1) % 2)

    slot = i % 2

    pltpu.make_async_copy(y_hbm.at[pl.ds(0, GATHER_TM), :], ya_ref.at[slot], sem.at[0, slot]).wait()
    pltpu.make_async_copy(y_hbm.at[pl.ds(0, GATHER_TM), :], yb_ref.at[slot], sem.at[1, slot]).wait()
    rt = rt_ref[...]
    out = h_ref[...] + rt[:, RT_C1:RT_C1 + 1] * ya_ref[slot] + rt[:, RT_C2:RT_C2 + 1] * yb_ref[slot]
    if post_norm:
        ms = jnp.mean(out * out, axis=-1, keepdims=True)
        out = out * lax.rsqrt(ms + NORM_EPS) * pw_ref[...]
    o_ref[...] = out


def moe_combine(d1, d2, h, rt, y, post_norm_w):
    s, d = h.shape
    post_norm = post_norm_w is not None
    pw = (post_norm_w if post_norm else jnp.ones((d,), F32)).reshape(1, d)
    return pl.pallas_call(
        functools.partial(_combine_kernel, post_norm=post_norm),
        grid_spec=pltpu.PrefetchScalarGridSpec(
            num_scalar_prefetch=2,
            grid=(s // GATHER_TM,),
            in_specs=[pl.BlockSpec((GATHER_TM, d), lambda i, a, b: (i, 0)),
                      pl.BlockSpec((GATHER_TM, LANES), lambda i, a, b: (i, 0)),
                      pl.BlockSpec((1, d), lambda i, a, b: (0, 0)),
                      pl.BlockSpec(memory_space=pl.ANY)],
            out_specs=pl.BlockSpec((GATHER_TM, d), lambda i, a, b: (i, 0)),
            scratch_shapes=[pltpu.VMEM((2, GATHER_TM, d), F32), pltpu.VMEM((2, GATHER_TM, d), F32),
                            pltpu.SemaphoreType.DMA((2, 2))]),
        out_shape=jax.ShapeDtypeStruct((s, d), F32),
        compiler_params=_cparams("arbitrary"),
        name="moe_combine",
    )(d1, d2, h, rt, pw, y)


def _mm_kernel(x_ref, w_ref, o_ref):
    o_ref[...] = jnp.dot(x_ref[...], w_ref[...], preferred_element_type=F32).astype(o_ref.dtype)


def matmul(x, w, out_dtype, tm, tn, n=None):
    m, k = x.shape
    n = w.shape[1] if n is None else n
    return pl.pallas_call(
        _mm_kernel,
        grid=(m // tm, n // tn),
        in_specs=[pl.BlockSpec((tm, k), lambda i, j: (i, 0)),
                  pl.BlockSpec((k, tn), lambda i, j: (0, j))],
        out_specs=pl.BlockSpec((tm, tn), lambda i, j: (i, j)),
        out_shape=jax.ShapeDtypeStruct((m, n), out_dtype),
        compiler_params=_cparams("parallel", "arbitrary"),
        name="in_proj",
    )(x, w)


def _out_proj_kernel(ya_ref, yb_ref, wa_ref, wb_ref, r_ref, o_ref):
    acc = jnp.dot(ya_ref[...], wa_ref[...], preferred_element_type=F32)
    acc += jnp.dot(yb_ref[...], wb_ref[...], preferred_element_type=F32)
    o_ref[...] = r_ref[...] + acc


def out_proj(y_ssd, y_ml, w_out, resid, tm=1024, tn=1024):
    m, ka = y_ssd.shape
    n = w_out.shape[1]
    return pl.pallas_call(
        _out_proj_kernel,
        grid=(m // tm, n // tn),
        in_specs=[pl.BlockSpec((tm, ka), lambda i, j: (i, 0)),
                  pl.BlockSpec((tm, ka), lambda i, j: (i, 0)),
                  pl.BlockSpec((ka, tn), lambda i, j: (0, j)),
                  pl.BlockSpec((ka, tn), lambda i, j: (1, j)),
                  pl.BlockSpec((tm, tn), lambda i, j: (i, j))],
        out_specs=pl.BlockSpec((tm, tn), lambda i, j: (i, j)),
        out_shape=jax.ShapeDtypeStruct((m, n), F32),
        compiler_params=_cparams("parallel", "arbitrary"),
        name="out_proj",
    )(y_ssd, y_ml, w_out, w_out, resid)


def _silu(x):
    return x * jax.nn.sigmoid(x)


class UpPlan(NamedTuple):
    counts: jax.Array
    run_of_tile: jax.Array
    last_of_run: jax.Array
    phase_expert: jax.Array
    phase_hidden: jax.Array


def _up_plan(tiles, n_tiles_max, n_hidden_tiles):
    ne = tiles.shape[0]
    tile_end = jnp.cumsum(tiles)
    nt = tile_end[-1]
    tid = jnp.minimum(jnp.arange(n_tiles_max, dtype=jnp.int32), nt - 1)
    eot = jnp.sum((tid[:, None] >= tile_end[None, :]).astype(jnp.int32), axis=1)
    has = tiles > 0
    n_runs = jnp.sum(has.astype(jnp.int32))
    run_of_expert = jnp.cumsum(has.astype(jnp.int32)) - 1
    run_expert = jnp.argsort(jnp.logical_not(has), stable=True).astype(jnp.int32)
    k = jnp.arange(ne * n_hidden_tiles, dtype=jnp.int32)
    plan = UpPlan(counts=jnp.stack([nt, n_runs, n_runs * n_hidden_tiles]).astype(jnp.int32),
                  run_of_tile=run_of_expert[eot],
                  last_of_run=(tid == tile_end[eot] - 1).astype(jnp.int32),
                  phase_expert=run_expert[k % n_runs],
                  phase_hidden=jnp.minimum(k // n_runs, n_hidden_tiles - 1))
    return plan, eot


def _grouped_up_kernel(cnt_ref, run_ref, last_ref, pe_ref, ph_ref, x_ref, wg_hbm, wu_hbm, o_ref,
                       wf_ref, wb_even_ref, wb_odd_ref, sem, *, tf):
    j, t = pl.program_id(0), pl.program_id(1)
    nt, n_runs, n_phases = cnt_ref[0], cnt_ref[1], cnt_ref[2]

    def fetch(p):
        cols = pl.ds(pl.multiple_of(ph_ref[p] * tf, tf), tf)
        return (pltpu.make_async_copy(wg_hbm.at[pe_ref[p], :, cols], wf_ref.at[0], sem.at[0]),
                pltpu.make_async_copy(wu_hbm.at[pe_ref[p], :, cols], wf_ref.at[1], sem.at[1]))

    def start(p):
        for c in fetch(p):
            c.start()

    def stage(p, dst_ref):
        for c in fetch(p):
            c.wait()
        dst_ref[0] = wf_ref[0].astype(BF16)
        dst_ref[1] = wf_ref[1].astype(BF16)

    def compute(w_ref):
        x = x_ref[...]
        g = jnp.dot(x, w_ref[0], preferred_element_type=F32)
        u = jnp.dot(x, w_ref[1], preferred_element_type=F32)
        o_ref[...] = (_silu(g) * u).astype(o_ref.dtype)

    active = t < nt
    p = j * n_runs + run_ref[t]
    hand_over = active & (last_ref[t] == 1) & (p + 1 < n_phases)

    @pl.when(active & (j == 0) & (t == 0))
    def _():
        start(0)
        stage(0, wb_even_ref)

        @pl.when(n_phases > 1)
        def _():
            start(1)

    def step(cur_ref, nxt_ref):
        @pl.when(jnp.logical_not(hand_over))
        def _():
            compute(cur_ref)

        @pl.when(hand_over)
        def _():
            stage(p + 1, nxt_ref)
            compute(cur_ref)

            @pl.when(p + 2 < n_phases)
            def _():
                start(p + 2)

    @pl.when(active & (p % 2 == 0))
    def _():
        step(wb_even_ref, wb_odd_ref)

    @pl.when(active & (p % 2 == 1))
    def _():
        step(wb_odd_ref, wb_even_ref)

    @pl.when(jnp.logical_not(active))
    def _():
        o_ref[...] = jnp.zeros(o_ref.shape, o_ref.dtype)


def grouped_swiglu_up(plan, xs, wg, wu, tf=512):
    r, k = xs.shape
    f = wg.shape[2]
    row = lambda j, t, cnt, *_: (jnp.maximum(jnp.minimum(t, cnt[0] - 1), 0), 0)
    return pl.pallas_call(
        functools.partial(_grouped_up_kernel, tf=tf),
        grid_spec=pltpu.PrefetchScalarGridSpec(
            num_scalar_prefetch=5,
            grid=(f // tf, r // MOE_TM),
            in_specs=[pl.BlockSpec((MOE_TM, k), row),
                      pl.BlockSpec(memory_space=pl.ANY),
                      pl.BlockSpec(memory_space=pl.ANY)],
            out_specs=pl.BlockSpec((MOE_TM, tf), lambda j, t, *_: (t, j)),
            scratch_shapes=[pltpu.VMEM((2, k, tf), F32), pltpu.VMEM((2, k, tf), BF16),
                            pltpu.VMEM((2, k, tf), BF16), pltpu.SemaphoreType.DMA((2,))]),
        out_shape=jax.ShapeDtypeStruct((r, f), BF16),
        compiler_params=_cparams("arbitrary", "arbitrary"),
        name="grouped_swiglu_up",
    )(*plan, xs, wg, wu)


def _moe_down_kernel(eot_ref, nt_ref, a_ref, w_ref, o_ref):
    @pl.when(pl.program_id(1) < nt_ref[0])
    def _():
        o_ref[...] = jnp.dot(a_ref[...], w_ref[...], preferred_element_type=F32)

    @pl.when(pl.program_id(1) >= nt_ref[0])
    def _():
        o_ref[...] = jnp.zeros(o_ref.shape, o_ref.dtype)


def moe_down(eot, nt, a, wd, tn=1024):
    r, f = a.shape
    n = wd.shape[2]
    return pl.pallas_call(
        _moe_down_kernel,
        grid_spec=pltpu.PrefetchScalarGridSpec(
            num_scalar_prefetch=2,
            grid=(n // tn, r // MOE_TM),
            in_specs=[pl.BlockSpec((MOE_TM, f), lambda j, t, eot, nt: (jnp.maximum(jnp.minimum(t, nt[0] - 1), 0), 0)),
                      pl.BlockSpec((None, f, tn), lambda j, t, eot, nt: (eot[t], 0, j))],
            out_specs=pl.BlockSpec((MOE_TM, tn), lambda j, t, eot, nt: (t, j))),
        out_shape=jax.ShapeDtypeStruct((r, n), F32),
        compiler_params=_cparams("arbitrary", "arbitrary"),
        name="moe_down",
    )(eot, nt, a, wd)


def moe_swiglu(h, norm_w, router_w, wg, wu, wd, post_norm_w=None):
    s, d = h.shape
    n_tiles_max = (2 * s) // MOE_TM + N_EXPERTS
    n_rows = n_tiles_max * MOE_TM
    rt = router(h, norm_w, router_w)
    dest, meta = moe_plan(rt)
    d1, d2 = dest[:, 0], dest[:, 1]
    plan, eot = _up_plan(meta[1, :N_EXPERTS], n_tiles_max, wg.shape[2] // UP_TF)
    nt = plan.counts[0:1]
    src = moe_invert(d1, d2, n_rows)
    xs = moe_gather_norm(src, nt, h, norm_w)
    a = grouped_swiglu_up(plan, xs, wg, wu, UP_TF)
    y = moe_down(eot, nt, a, wd)
    return moe_combine(d1, d2, h, rt, y, post_norm_w)


def _down_kernel(a_ref, w_ref, r_ref, o_ref):
    kk = pl.program_id(2)

    @pl.when(kk == 0)
    def _():
        o_ref[...] = r_ref[...]

    o_ref[...] += jnp.dot(a_ref[...], w_ref[...], preferred_element_type=F32)


def down_proj(a, w, resid, tk, tm=1024, tn=1024):
    m, k = a.shape
    n = w.shape[1]
    return pl.pallas_call(
        _down_kernel,
        grid=(m // tm, n // tn, k // tk),
        in_specs=[pl.BlockSpec((tm, tk), lambda i, j, kk: (i, kk)),
                  pl.BlockSpec((tk, tn), lambda i, j, kk: (kk, j)),
                  pl.BlockSpec((tm, tn), lambda i, j, kk: (i, j))],
        out_specs=pl.BlockSpec((tm, tn), lambda i, j, kk: (i, j)),
        out_shape=jax.ShapeDtypeStruct((m, n), F32),
        compiler_params=_cparams("parallel", "arbitrary", "arbitrary"),
        name="down_proj",
    )(a, w, resid)


def _softplus(x):
    return jnp.maximum(x, 0.0) + jnp.log1p(jnp.exp(-jnp.abs(x)))


def _tril_ones(n):
    r = lax.broadcasted_iota(jnp.int32, (n, n), 0)
    c = lax.broadcasted_iota(jnp.int32, (n, n), 1)
    return r >= c


def _ssd_kernel(z_ref, x_ref, b_ref, c_ref, sm_ref, convw_ref, convb_ref, smb_ref, alog_ref,
                dskip_ref, nw_ref, expand_ref, o_ref, tail_ref, state_ref):
    L = SSD_CHUNK
    HI = lax.Precision.HIGHEST

    @pl.when(pl.program_id(0) == 0)
    def _():
        tail_ref[...] = jnp.zeros(tail_ref.shape, F32)
        state_ref[...] = jnp.zeros(state_ref.shape, F32)

    row8 = lax.broadcasted_iota(jnp.int32, (SUBLANES, LANES), 0)

    def conv_silu(u, col0):
        w = u.shape[1]
        cols = slice(col0, col0 + w)
        tail = tail_ref[:, cols]
        acc = convb_ref[:, cols] + u * convw_ref[SSD_CONV - 1:SSD_CONV, cols]
        for k in range(1, SSD_CONV):
            rolled = pltpu.roll(u, k, axis=0)
            fix = jnp.where(jnp.tile(row8, (1, w // LANES)) < k, pltpu.roll(tail, k, axis=0), rolled[0:SUBLANES])
            shifted = jnp.concatenate([fix, rolled[SUBLANES:]], axis=0)
            acc = acc + shifted * convw_ref[SSD_CONV - 1 - k:SSD_CONV - k, cols]
        tail_ref[:, cols] = u[L - SUBLANES:L]
        return _silu(acc)

    xs = conv_silu(x_ref[...], 0)
    bmat = conv_silu(b_ref[...], SSD_INNER)
    cmat = conv_silu(c_ref[...], SSD_INNER + SSD_GROUPS * SSD_STATE)

    lane = lax.broadcasted_iota(jnp.int32, (L, LANES), 1)

    def expand_heads(v):
        hi = v.astype(BF16).astype(F32)
        r1 = v - hi
        mid = r1.astype(BF16).astype(F32)
        lo = r1 - mid
        packed = jnp.where(lane < 32, hi,
                           jnp.where(lane < 64, pltpu.roll(mid, 32, axis=1),
                                     jnp.where(lane < 96, pltpu.roll(lo, 64, axis=1), 0.0)))
        return jnp.dot(packed.astype(BF16), expand_ref[...], preferred_element_type=F32)

    causal = _tril_ones(L)
    dt = jnp.where(lane < SSD_HEADS, _softplus(sm_ref[...] + smb_ref[...]), 0.0)
    da = dt * (-jnp.exp(alog_ref[...]))
    acs = jnp.dot(causal.astype(F32), da, preferred_element_type=F32, precision=HI)
    acs_t = acs.T
    a_last = acs[L - 1:L, :]
    dt_e = expand_heads(dt)
    eacs_e = expand_heads(jnp.exp(acs))
    dst_e = expand_heads(jnp.exp(a_last - acs))
    xdt = xs * dt_e
    xdec = (xdt * dst_e).astype(BF16)
    xdt_b = xdt.astype(BF16)
    chunk_decay_e = eacs_e[L - 1:L, :]

    y = xs * dskip_ref[...]
    zgate = _silu(z_ref[...])
    lo_half = lane < SSD_HEAD_DIM

    def group(g):
        gs = slice(g * SSD_GROUP_WIDTH, (g + 1) * SSD_GROUP_WIDTH)
        bg = bmat[:, g * SSD_STATE:(g + 1) * SSD_STATE]
        cg = cmat[:, g * SSD_STATE:(g + 1) * SSD_STATE]
        cg_b = cg.astype(BF16)
        cb = lax.dot_general(cg_b, bg.astype(BF16), (((1,), (1,)), ((), ())),
                             preferred_element_type=F32)
        h_enter = state_ref[g]
        y_off = jnp.dot(cg_b, h_enter.astype(BF16), preferred_element_type=F32)
        st_new = jnp.dot(bg.T.astype(BF16), xdec[:, gs], preferred_element_type=F32)
        state_ref[g] = chunk_decay_e[:, gs] * h_enter + st_new
        yd_parts = []
        for p in range(4):
            j0 = g * 8 + 2 * p
            ms = []
            for j in (j0, j0 + 1):
                seg = jnp.where(causal, acs[:, j:j + 1] - acs_t[j:j + 1, :], -jnp.inf)
                ms.append((cb * jnp.exp(seg)).astype(BF16))
            xp = xdt_b[:, g * SSD_GROUP_WIDTH + p * LANES:g * SSD_GROUP_WIDTH + (p + 1) * LANES]
            zero = jnp.zeros_like(xp)
            rhs = jnp.concatenate([jnp.where(lo_half, xp, zero), jnp.where(lo_half, zero, xp)], axis=0)
            yd_parts.append(jnp.dot(jnp.concatenate(ms, axis=1), rhs, preferred_element_type=F32))
        y_diag = jnp.concatenate(yd_parts, axis=1)
        yg = (y[:, gs] + y_diag + y_off * eacs_e[:, gs]) * zgate[:, gs]
        msq = jnp.mean(yg * yg, axis=1, keepdims=True)
        o_ref[:, gs] = (yg * lax.rsqrt(msq + NORM_EPS) * nw_ref[:, gs]).astype(o_ref.dtype)

    return group


def _log_sigmoid(x):
    return jnp.minimum(x, 0.0) - jnp.log1p(jnp.exp(-jnp.abs(x)))


def _mlstm_kernel(q_ref, k_ref, v_ref, o_ref, sm_ref, smb_ref, nw_ref, out_ref, c_ref, n_ref, m_ref):
    L = MLSTM_CHUNK
    HI = lax.Precision.HIGHEST

    @pl.when(pl.program_id(0) == 0)
    def _():
        c_ref[...] = jnp.zeros(c_ref.shape, F32)
        n_ref[...] = jnp.zeros(n_ref.shape, F32)
        m_ref[...] = jnp.zeros(m_ref.shape, F32)

    causal = _tril_ones(L)
    sm = sm_ref[...] + smb_ref[...]
    bcs = jnp.dot(causal.astype(F32), _log_sigmoid(sm), preferred_element_type=F32, precision=HI)
    sm_t = sm.T
    bcs_t = bcs.T

    def head(h):
        li_col = sm[:, SMALL_I + h:SMALL_I + h + 1]
        b_col = bcs[:, SMALL_F + h:SMALL_F + h + 1]
        r_row = sm_t[SMALL_I + h:SMALL_I + h + 1, :] - bcs_t[SMALL_F + h:SMALL_F + h + 1, :]
        m_prev = m_ref[h:h + 1, 0:1]
        inter = b_col + m_prev
        dm = jnp.where(causal, b_col + r_row, -jnp.inf)
        m_t = jnp.maximum(inter, jnp.max(dm, axis=1, keepdims=True))
        w_intra = jnp.exp(dm - m_t)
        w_inter = jnp.exp(inter - m_t)
        qf = q_ref[:, h * MLSTM_DQK:(h + 1) * MLSTM_DQK] * (MLSTM_DQK ** -0.5)
        kf = k_ref[:, h * MLSTM_DQK:(h + 1) * MLSTM_DQK]
        qb = qf.astype(BF16)
        vb = v_ref[:, h * MLSTM_DV:(h + 1) * MLSTM_DV].astype(BF16)
        sqk = lax.dot_general(qb, kf.astype(BF16), (((1,), (1,)), ((), ())),
                              preferred_element_type=F32) * w_intra
        c_st = c_ref[h]
        n_row = n_ref[h:h + 1, :]
        num = (w_inter * jnp.dot(qb, c_st.astype(BF16), preferred_element_type=F32)
               + jnp.dot(sqk.astype(BF16), vb, preferred_element_type=F32))
        qn = jnp.sum(qf * n_row, axis=1, keepdims=True)
        den = w_inter * qn + jnp.sum(sqk, axis=1, keepdims=True)
        hout = num / jnp.maximum(jnp.abs(den), jnp.exp(-m_t))
        b_last = b_col[L - 1:L, :]
        g_col = b_last - b_col + li_col
        m_new = jnp.maximum(b_last + m_prev, jnp.max(g_col, axis=0, keepdims=True))
        decay = jnp.exp(b_last + m_prev - m_new)
        kw = kf * jnp.exp(g_col - m_new)
        c_ref[h] = decay * c_st + jnp.dot(kw.T.astype(BF16), vb, preferred_element_type=F32)
        n_ref[h:h + 1, :] = decay * n_row + jnp.sum(kw, axis=0, keepdims=True)
        m_ref[h:h + 1, :] = jnp.broadcast_to(m_new, (1, LANES))
        vs = slice(h * MLSTM_DV, (h + 1) * MLSTM_DV)
        msq = jnp.mean(hout * hout, axis=1, keepdims=True)
        hn = hout * lax.rsqrt(msq + NORM_EPS) * nw_ref[:, vs]
        out_ref[:, vs] = (jax.nn.sigmoid(o_ref[:, vs]) * hn).astype(out_ref.dtype)

    return head


assert MLSTM_CHUNK == SSD_CHUNK


def _mixer_kernel(z_ref, x_ref, b_ref, c_ref, q_ref, k_ref, v_ref, og_ref, sm_ref,
                  convw_ref, convb_ref, smb_ref, alog_ref, dskip_ref, ssd_nw_ref, expand_ref, ml_nw_ref,
                  y_ssd_ref, y_ml_ref, tail_ref, state_ref, cst_ref, nst_ref, mst_ref):
    mlstm_head = _mlstm_kernel(q_ref, k_ref, v_ref, og_ref, sm_ref, smb_ref, ml_nw_ref, y_ml_ref,
                               cst_ref, nst_ref, mst_ref)
    ssd_group = _ssd_kernel(z_ref, x_ref, b_ref, c_ref, sm_ref, convw_ref, convb_ref, smb_ref, alog_ref,
                            dskip_ref, ssd_nw_ref, expand_ref, y_ssd_ref, tail_ref, state_ref)
    for h in range(MLSTM_HEADS):
        mlstm_head(h)
    for g in range(SSD_GROUPS):
        ssd_group(g)


def hybrid_mixer(proj_a, proj_b, proj_s, conv_w, conv_b, small_bias, a_log, d_skip, ssd_norm_w, ml_norm_w):
    s = proj_a.shape[0]
    L = SSD_CHUNK
    conv_dim = conv_w.shape[1]
    qk_w = MLSTM_HEADS * MLSTM_DQK
    alog = jnp.pad(a_log, (0, LANES - SSD_HEADS)).reshape(1, LANES)
    dskip_e = jnp.repeat(d_skip, SSD_HEAD_DIM).reshape(1, SSD_INNER)
    piece_row = jnp.arange(LANES)[:, None]
    expand = ((jnp.arange(SSD_INNER)[None, :] // SSD_HEAD_DIM == piece_row % SSD_HEADS)
              & (piece_row < 3 * SSD_HEADS)).astype(BF16)
    full = lambda shp: pl.BlockSpec(shp, lambda i: (0,) * len(shp))
    blk = lambda w, col: pl.BlockSpec((L, w), lambda i: (i, col // w))
    return pl.pallas_call(
        _mixer_kernel,
        grid=(s // L,),
        in_specs=[blk(SSD_INNER, PA_Z), blk(SSD_INNER, PA_X), blk(512, PA_B), blk(512, PA_C),
                  blk(qk_w, PB_Q), blk(qk_w, PB_K), blk(MLSTM_WIDTH, PB_V), blk(MLSTM_WIDTH, PB_O),
                  blk(LANES, 0),
                  full((SSD_CONV, conv_dim)), full((1, conv_dim)), full((1, LANES)), full((1, LANES)),
                  full((1, SSD_INNER)), full((1, SSD_INNER)), full((LANES, SSD_INNER)),
                  full((1, MLSTM_WIDTH))],
        out_specs=[pl.BlockSpec((L, SSD_INNER), lambda i: (i, 0)),
                   pl.BlockSpec((L, MLSTM_WIDTH), lambda i: (i, 0))],
        out_shape=[jax.ShapeDtypeStruct((s, SSD_INNER), BF16),
                   jax.ShapeDtypeStruct((s, MLSTM_WIDTH), BF16)],
        scratch_shapes=[pltpu.VMEM((SUBLANES, conv_dim), F32),
                        pltpu.VMEM((SSD_GROUPS, SSD_STATE, SSD_GROUP_WIDTH), F32),
                        pltpu.VMEM((MLSTM_HEADS, MLSTM_DQK, MLSTM_DV), F32),
                        pltpu.VMEM((SUBLANES, MLSTM_DQK), F32),
                        pltpu.VMEM((SUBLANES, LANES), F32)],
        compiler_params=_cparams("arbitrary"),
        name="hybrid_mixer",
    )(proj_a, proj_a, proj_a, proj_a, proj_b, proj_b, proj_b, proj_b, proj_s,
      conv_w, conv_b.reshape(1, conv_dim), small_bias, alog, dskip_e,
      ssd_norm_w.reshape(1, SSD_INNER), expand, ml_norm_w.reshape(1, MLSTM_WIDTH))


def _split_w_in(w_in):
    w_all = w_in.astype(BF16)
    w_b = w_all[:, PB_START:PB_START + PB_WIDTH]
    w_s = jnp.concatenate([w_all[:, IN_DT:IN_DT + SSD_HEADS], w_all[:, IN_I:IN_I + MLSTM_HEADS],
                           w_all[:, IN_F:IN_F + MLSTM_HEADS],
                           jnp.zeros((w_in.shape[0], LANES - SSD_HEADS - 2 * MLSTM_HEADS), BF16)], axis=1)
    return w_all, w_b, w_s


def _small_bias(dt_bias, i_bias, f_bias):
    v = jnp.concatenate([dt_bias, i_bias, f_bias, jnp.zeros((LANES - 40,), F32)])
    return v.reshape(1, LANES)


def kernel(x, norm_mix_w, w_in, conv_w, conv_b, dt_bias, a_log, d_skip, ssd_norm_w, mlstm_i_bias,
           mlstm_f_bias, mlstm_norm_w, w_out, norm_ffn_w, ffn_w_gate, ffn_w_up, ffn_w_down, router_w,
           moe_w_gate, moe_w_up, moe_w_down, final_norm_w):
    bsz, s, d = x.shape
    depth = w_in.shape[0]
    outs = []
    for b in range(bsz):
        h = x[b]
        normed = False
        for layer in range(depth):
            w_all, w_b, w_s = _split_w_in(w_in[layer])
            u, proj_s = rmsnorm_small(h, norm_mix_w[layer], w_s)
            proj_a = matmul(u, w_all, F32, tm=1024, tn=PA_TN, n=PA_WIDTH)
            proj_b = matmul(u, w_b, F32, tm=1024, tn=PB_TN)
            sbias = _small_bias(dt_bias[layer], mlstm_i_bias[layer], mlstm_f_bias[layer])
            y_ssd, y_ml = hybrid_mixer(proj_a, proj_b, proj_s, conv_w[layer], conv_b[layer], sbias,
                                       a_log[layer], d_skip[layer], ssd_norm_w[layer], mlstm_norm_w[layer])
            h = out_proj(y_ssd, y_ml, w_out[layer].astype(BF16), h)
            j = layer // 2
            if layer % 2 == 0:
                u = rmsnorm(h, norm_ffn_w[layer], BF16)
                plan, _ = _up_plan(jnp.full((1,), s // MOE_TM, jnp.int32), s // MOE_TM, FF_DENSE // UP_TF)
                a = grouped_swiglu_up(plan, u, ffn_w_gate[j][None], ffn_w_up[j][None], UP_TF)
                h = down_proj(a, ffn_w_down[j].astype(BF16), h, tk=FF_DENSE // 4)
            else:
                normed = layer == depth - 1
                h = moe_swiglu(h, norm_ffn_w[layer], router_w[j], moe_w_gate[j], moe_w_up[j],
                               moe_w_down[j].astype(BF16), final_norm_w if normed else None)
        outs.append(h if normed else rmsnorm(h, final_norm_w, F32))
    return jnp.stack(outs, axis=0)
```

```python
import functools
from typing import NamedTuple

import jax
import jax.numpy as jnp
from jax import lax
from jax.experimental import pallas as pl
from jax.experimental.pallas import tpu as pltpu

F32 = jnp.float32
BF16 = jnp.bfloat16

D_MODEL = 4096
SSD_INNER = 2048
SSD_HEAD_DIM = 64
SSD_HEADS = 32
SSD_GROUPS = 4
SSD_STATE = 128
SSD_CONV = 4
SSD_CHUNK = 128
SSD_GROUP_WIDTH = SSD_INNER // SSD_GROUPS
MLSTM_WIDTH = 2048
MLSTM_HEADS = 4
MLSTM_DV = 512
MLSTM_DQK = 256
FF_DENSE = 14336
N_EXPERTS = 8
FF_EXPERT = 5632
NORM_EPS = 1e-6

LANES = 128
SUBLANES = 8
VMEM_LIMIT_BYTES = 56 * 1024 * 1024

PA_WIDTH, PA_TN = 5120, 1280
PA_Z, PA_X, PA_B, PA_C = 0, 2048, 4096, 4608
PB_START, PB_WIDTH, PB_TN = 5152, 6144, 1024
PB_Q, PB_K, PB_V, PB_O = 0, 1024, 2048, 4096
IN_DT, IN_I, IN_F = 5120, 11296, 11300
SMALL_DT, SMALL_I, SMALL_F = 0, 32, 36

MLSTM_CHUNK = 128


def _cparams(*sem):
    return pltpu.CompilerParams(dimension_semantics=sem, vmem_limit_bytes=VMEM_LIMIT_BYTES)


def _rmsnorm_kernel(x_ref, w_ref, o_ref):
    x = x_ref[...]
    ms = jnp.mean(x * x, axis=-1, keepdims=True)
    o_ref[...] = (x * lax.rsqrt(ms + NORM_EPS) * w_ref[...]).astype(o_ref.dtype)


def rmsnorm(x, w, out_dtype, tr=512):
    s, d = x.shape
    return pl.pallas_call(
        _rmsnorm_kernel,
        grid=(s // tr,),
        in_specs=[pl.BlockSpec((tr, d), lambda i: (i, 0)),
                  pl.BlockSpec((1, d), lambda i: (0, 0))],
        out_specs=pl.BlockSpec((tr, d), lambda i: (i, 0)),
        out_shape=jax.ShapeDtypeStruct((s, d), out_dtype),
        compiler_params=_cparams("parallel"),
        name="rmsnorm",
    )(x, w.reshape(1, d))


def _rmsnorm_small_kernel(x_ref, w_ref, ws_ref, u_ref, ps_ref):
    x = x_ref[...]
    ms = jnp.mean(x * x, axis=-1, keepdims=True)
    u = (x * lax.rsqrt(ms + NORM_EPS) * w_ref[...]).astype(BF16)
    u_ref[...] = u
    ps_ref[...] = jnp.dot(u, ws_ref[...], preferred_element_type=F32)


def rmsnorm_small(x, w, w_small, tr=512):
    s, d = x.shape
    return pl.pallas_call(
        _rmsnorm_small_kernel,
        grid=(s // tr,),
        in_specs=[pl.BlockSpec((tr, d), lambda i: (i, 0)),
                  pl.BlockSpec((1, d), lambda i: (0, 0)),
                  pl.BlockSpec((d, LANES), lambda i: (0, 0))],
        out_specs=[pl.BlockSpec((tr, d), lambda i: (i, 0)),
                   pl.BlockSpec((tr, LANES), lambda i: (i, 0))],
        out_shape=[jax.ShapeDtypeStruct((s, d), BF16),
                   jax.ShapeDtypeStruct((s, LANES), F32)],
        compiler_params=_cparams("parallel"),
        name="rmsnorm_small",
    )(x, w.reshape(1, d), w_small)


RT_SEL = 0
RT_C1, RT_C2 = 8, 9
RT_I1, RT_I2 = 10, 11


def _router_kernel(x_ref, w_ref, rw_ref, rt_ref):
    x = x_ref[...]
    ms = jnp.mean(x * x, axis=-1, keepdims=True)
    u = x * lax.rsqrt(ms + NORM_EPS) * w_ref[...]
    logits = jnp.dot(u, rw_ref[...], preferred_element_type=F32, precision=lax.Precision.HIGHEST)
    lane = lax.broadcasted_iota(jnp.int32, logits.shape, 1)
    valid = lane < N_EXPERTS
    lg = jnp.where(valid, logits, -jnp.inf)
    ex = jnp.exp(lg - jnp.max(lg, axis=1, keepdims=True))
    probs = ex / jnp.sum(ex, axis=1, keepdims=True)
    probs = jnp.where(valid, probs, -1.0)
    p1 = jnp.max(probs, axis=1, keepdims=True)
    i1 = jnp.min(jnp.where(probs == p1, lane, LANES), axis=1, keepdims=True)
    rest = jnp.where(lane == i1, -1.0, probs)
    p2 = jnp.max(rest, axis=1, keepdims=True)
    i2 = jnp.min(jnp.where(rest == p2, lane, LANES), axis=1, keepdims=True)
    tot = p1 + p2
    rt = jnp.where((lane == i1) | (lane == i2), 1.0, 0.0)
    rt = jnp.where(lane == RT_C1, p1 / tot, rt)
    rt = jnp.where(lane == RT_C2, p2 / tot, rt)
    rt = jnp.where(lane == RT_I1, i1.astype(F32), rt)
    rt = jnp.where(lane == RT_I2, i2.astype(F32), rt)
    rt_ref[...] = rt


def router(x, w, router_w, tr=512):
    s, d = x.shape
    rw = jnp.pad(router_w, ((0, 0), (0, LANES - N_EXPERTS)))
    return pl.pallas_call(
        _router_kernel,
        grid=(s // tr,),
        in_specs=[pl.BlockSpec((tr, d), lambda i: (i, 0)),
                  pl.BlockSpec((1, d), lambda i: (0, 0)),
                  pl.BlockSpec((d, LANES), lambda i: (0, 0))],
        out_specs=pl.BlockSpec((tr, LANES), lambda i: (i, 0)),
        out_shape=jax.ShapeDtypeStruct((s, LANES), F32),
        compiler_params=_cparams("parallel"),
        name="router",
    )(x, w.reshape(1, d), rw)


MOE_TM = 512
UP_TF = 512
PLAN_BLK = 512


def _plan_kernel(rt_ref, dest_ref, meta_ref):
    s = rt_ref.shape[0]
    nblk = s // PLAN_BLK
    lane = lax.broadcasted_iota(jnp.int32, (PLAN_BLK, LANES), 1)
    is_exp = lane < N_EXPERTS
    r = lax.broadcasted_iota(jnp.int32, (PLAN_BLK, PLAN_BLK), 0)
    c = lax.broadcasted_iota(jnp.int32, (PLAN_BLK, PLAN_BLK), 1)
    strict_tril = (r > c).astype(BF16)

    def rank_body(b, carry):
        rows = pl.ds(pl.multiple_of(b * PLAN_BLK, PLAN_BLK), PLAN_BLK)
        sel = jnp.where(is_exp, rt_ref[rows, :], 0.0)
        rank = jnp.dot(strict_tril, sel.astype(BF16), preferred_element_type=F32) + carry
        dest_ref[rows, :] = rank.astype(jnp.int32)
        return carry + jnp.sum(sel, axis=0, keepdims=True)

    counts = lax.fori_loop(0, nblk, rank_body, jnp.zeros((1, LANES), F32))
    tiles = jnp.floor((counts + (MOE_TM - 1)) * (1.0 / MOE_TM))
    er = lax.broadcasted_iota(jnp.int32, (LANES, LANES), 0)
    ec = lax.broadcasted_iota(jnp.int32, (LANES, LANES), 1)
    before = (er < ec).astype(F32)
    tile_start = jnp.dot(jnp.broadcast_to(tiles, (SUBLANES, LANES)), before, preferred_element_type=F32,
                         precision=lax.Precision.HIGHEST)[0:1, :]
    row_off = tile_start * MOE_TM
    meta_ref[0:1, :] = counts.astype(jnp.int32)
    meta_ref[1:2, :] = tiles.astype(jnp.int32)
    meta_ref[2:3, :] = tile_start.astype(jnp.int32)
    meta_ref[3:SUBLANES, :] = jnp.zeros((SUBLANES - 3, LANES), jnp.int32)

    def dest_body(b, _):
        rows = pl.ds(pl.multiple_of(b * PLAN_BLK, PLAN_BLK), PLAN_BLK)
        rt = rt_ref[rows, :]
        pos = dest_ref[rows, :].astype(F32) + row_off
        i1 = rt[:, RT_I1:RT_I1 + 1].astype(jnp.int32)
        i2 = rt[:, RT_I2:RT_I2 + 1].astype(jnp.int32)
        d1 = jnp.sum(jnp.where(lane == i1, pos, 0.0), axis=1, keepdims=True)
        d2 = jnp.sum(jnp.where(lane == i2, pos, 0.0), axis=1, keepdims=True)
        dest_ref[rows, :] = jnp.where(lane == 0, d1, jnp.where(lane == 1, d2, 0.0)).astype(jnp.int32)
        return 0

    lax.fori_loop(0, nblk, dest_body, 0)


def moe_plan(rt):
    s = rt.shape[0]
    return pl.pallas_call(
        _plan_kernel,
        out_shape=[jax.ShapeDtypeStruct((s, LANES), jnp.int32),
                   jax.ShapeDtypeStruct((SUBLANES, LANES), jnp.int32)],
        compiler_params=pltpu.CompilerParams(vmem_limit_bytes=VMEM_LIMIT_BYTES),
        name="moe_plan",
    )(rt)


def _invert_kernel(d1_ref, d2_ref, src_ref):
    n_rows = src_ref.shape[0]
    n_tok = d1_ref.shape[0]

    def zero(i, _):
        src_ref[i] = 0
        return 0

    lax.fori_loop(0, n_rows, zero, 0, unroll=16)

    def put(t, _):
        src_ref[d1_ref[t]] = t
        src_ref[d2_ref[t]] = t
        return 0

    lax.fori_loop(0, n_tok, put, 0, unroll=8)


def moe_invert(d1, d2, n_rows):
    smem = pl.BlockSpec(memory_space=pltpu.SMEM)
    return pl.pallas_call(
        _invert_kernel,
        in_specs=[smem, smem],
        out_specs=smem,
        out_shape=jax.ShapeDtypeStruct((n_rows,), jnp.int32),
        name="moe_invert",
    )(d1, d2)


GATHER_TM = 256
GATHER_UNROLL = 8


def _gather_norm_kernel(src_ref, nt_ref, h_hbm, w_ref, o_ref, buf_ref, sem):
    i = pl.program_id(0)
    nt = nt_ref[0]

    def row_copy(tile, slot, r):
        return pltpu.make_async_copy(h_hbm.at[pl.ds(src_ref[tile * MOE_TM + r], 1), :],
                                     buf_ref.at[slot, pl.ds(r, 1), :], sem.at[slot])

    def issue(tile, slot):
        def body(r, _):
            row_copy(tile, slot, r).start()
            return 0
        lax.fori_loop(0, MOE_TM, body, 0, unroll=GATHER_UNROLL)

    @pl.when(i == 0)
    def _():
        issue(0, 0)

    @pl.when(i + 1 < nt)
    def _():
        issue(i + 1, (i + 1) % 2)

    @pl.when(i < nt)
    def _():
        slot = i % 2

        pltpu.make_async_copy(h_hbm.at[pl.ds(0, MOE_TM), :], buf_ref.at[slot], sem.at[slot]).wait()
        x = buf_ref[slot]
        ms = jnp.mean(x * x, axis=-1, keepdims=True)
        o_ref[...] = (x * lax.rsqrt(ms + NORM_EPS) * w_ref[...]).astype(o_ref.dtype)

    @pl.when(i >= nt)
    def _():
        o_ref[...] = jnp.zeros(o_ref.shape, o_ref.dtype)


def moe_gather_norm(src, nt, h, w):
    n_rows = src.shape[0]
    d = h.shape[1]
    return pl.pallas_call(
        _gather_norm_kernel,
        grid_spec=pltpu.PrefetchScalarGridSpec(
            num_scalar_prefetch=2,
            grid=(n_rows // MOE_TM,),
            in_specs=[pl.BlockSpec(memory_space=pl.ANY),
                      pl.BlockSpec((1, d), lambda i, src, nt: (0, 0))],
            out_specs=pl.BlockSpec((MOE_TM, d), lambda i, src, nt: (i, 0)),
            scratch_shapes=[pltpu.VMEM((2, MOE_TM, d), F32), pltpu.SemaphoreType.DMA((2,))]),
        out_shape=jax.ShapeDtypeStruct((n_rows, d), BF16),
        compiler_params=_cparams("arbitrary"),
        name="moe_gather_norm",
    )(src, nt, h, w.reshape(1, d))


def _combine_kernel(d1_ref, d2_ref, h_ref, rt_ref, pw_ref, y_hbm, o_ref, ya_ref, yb_ref, sem, *, post_norm):
    i = pl.program_id(0)
    n = pl.num_programs(0)

    def copy_a(blk, slot, r):
        return pltpu.make_async_copy(y_hbm.at[pl.ds(d1_ref[blk * GATHER_TM + r], 1), :],
                                     ya_ref.at[slot, pl.ds(r, 1), :], sem.at[0, slot])

    def copy_b(blk, slot, r):
        return pltpu.make_async_copy(y_hbm.at[pl.ds(d2_ref[blk * GATHER_TM + r], 1), :],
                                     yb_ref.at[slot, pl.ds(r, 1), :], sem.at[1, slot])

    def issue(blk, slot):
        def body(r, _):
            copy_a(blk, slot, r).start()
            copy_b(blk, slot, r).start()
            return 0
        lax.fori_loop(0, GATHER_TM, body, 0, unroll=GATHER_UNROLL)

    @pl.when(i == 0)
    def _():
        issue(0, 0)

    @pl.when(i + 1 < n)
    def _():
        issue(i + 1, (i + 1) % 2)

    slot = i % 2

    pltpu.make_async_copy(y_hbm.at[pl.ds(0, GATHER_TM), :], ya_ref.at[slot], sem.at[0, slot]).wait()
    pltpu.make_async_copy(y_hbm.at[pl.ds(0, GATHER_TM), :], yb_ref.at[slot], sem.at[1, slot]).wait()
    rt = rt_ref[...]
    out = h_ref[...] + rt[:, RT_C1:RT_C1 + 1] * ya_ref[slot] + rt[:, RT_C2:RT_C2 + 1] * yb_ref[slot]
    if post_norm:
        ms = jnp.mean(out * out, axis=-1, keepdims=True)
        out = out * lax.rsqrt(ms + NORM_EPS) * pw_ref[...]
    o_ref[...] = out


def moe_combine(d1, d2, h, rt, y, post_norm_w):
    s, d = h.shape
    post_norm = post_norm_w is not None
    pw = (post_norm_w if post_norm else jnp.ones((d,), F32)).reshape(1, d)
    return pl.pallas_call(
        functools.partial(_combine_kernel, post_norm=post_norm),
        grid_spec=pltpu.PrefetchScalarGridSpec(
            num_scalar_prefetch=2,
            grid=(s // GATHER_TM,),
            in_specs=[pl.BlockSpec((GATHER_TM, d), lambda i, a, b: (i, 0)),
                      pl.BlockSpec((GATHER_TM, LANES), lambda i, a, b: (i, 0)),
                      pl.BlockSpec((1, d), lambda i, a, b: (0, 0)),
                      pl.BlockSpec(memory_space=pl.ANY)],
            out_specs=pl.BlockSpec((GATHER_TM, d), lambda i, a, b: (i, 0)),
            scratch_shapes=[pltpu.VMEM((2, GATHER_TM, d), F32), pltpu.VMEM((2, GATHER_TM, d), F32),
                            pltpu.SemaphoreType.DMA((2, 2))]),
        out_shape=jax.ShapeDtypeStruct((s, d), F32),
        compiler_params=_cparams("arbitrary"),
        name="moe_combine",
    )(d1, d2, h, rt, pw, y)


def _mm_kernel(x_ref, w_ref, o_ref):
    o_ref[...] = jnp.dot(x_ref[...], w_ref[...], preferred_element_type=F32).astype(o_ref.dtype)


def matmul(x, w, out_dtype, tm, tn):
    m, k = x.shape
    n = w.shape[1]
    return pl.pallas_call(
        _mm_kernel,
        grid=(m // tm, n // tn),
        in_specs=[pl.BlockSpec((tm, k), lambda i, j: (i, 0)),
                  pl.BlockSpec((k, tn), lambda i, j: (0, j))],
        out_specs=pl.BlockSpec((tm, tn), lambda i, j: (i, j)),
        out_shape=jax.ShapeDtypeStruct((m, n), out_dtype),
        compiler_params=_cparams("parallel", "arbitrary"),
        name="in_proj",
    )(x, w)


def _out_proj_kernel(ya_ref, yb_ref, wa_ref, wb_ref, r_ref, o_ref):
    acc = jnp.dot(ya_ref[...], wa_ref[...], preferred_element_type=F32)
    acc += jnp.dot(yb_ref[...], wb_ref[...], preferred_element_type=F32)
    o_ref[...] = r_ref[...] + acc


def out_proj(y_ssd, y_ml, w_out, resid, tm=1024, tn=1024):
    m, ka = y_ssd.shape
    n = w_out.shape[1]
    return pl.pallas_call(
        _out_proj_kernel,
        grid=(m // tm, n // tn),
        in_specs=[pl.BlockSpec((tm, ka), lambda i, j: (i, 0)),
                  pl.BlockSpec((tm, ka), lambda i, j: (i, 0)),
                  pl.BlockSpec((ka, tn), lambda i, j: (0, j)),
                  pl.BlockSpec((ka, tn), lambda i, j: (1, j)),
                  pl.BlockSpec((tm, tn), lambda i, j: (i, j))],
        out_specs=pl.BlockSpec((tm, tn), lambda i, j: (i, j)),
        out_shape=jax.ShapeDtypeStruct((m, n), F32),
        compiler_params=_cparams("parallel", "arbitrary"),
        name="out_proj",
    )(y_ssd, y_ml, w_out, w_out, resid)


def _silu(x):
    return x * jax.nn.sigmoid(x)


class UpPlan(NamedTuple):
    counts: jax.Array
    run_of_tile: jax.Array
    last_of_run: jax.Array
    phase_expert: jax.Array
    phase_hidden: jax.Array


def _up_plan(tiles, n_tiles_max, n_hidden_tiles):
    ne = tiles.shape[0]
    tile_end = jnp.cumsum(tiles)
    nt = tile_end[-1]
    tid = jnp.minimum(jnp.arange(n_tiles_max, dtype=jnp.int32), nt - 1)
    eot = jnp.sum((tid[:, None] >= tile_end[None, :]).astype(jnp.int32), axis=1)
    has = tiles > 0
    n_runs = jnp.sum(has.astype(jnp.int32))
    run_of_expert = jnp.cumsum(has.astype(jnp.int32)) - 1
    run_expert = jnp.argsort(jnp.logical_not(has), stable=True).astype(jnp.int32)
    k = jnp.arange(ne * n_hidden_tiles, dtype=jnp.int32)
    plan = UpPlan(counts=jnp.stack([nt, n_runs, n_runs * n_hidden_tiles]).astype(jnp.int32),
                  run_of_tile=run_of_expert[eot],
                  last_of_run=(tid == tile_end[eot] - 1).astype(jnp.int32),
                  phase_expert=run_expert[k % n_runs],
                  phase_hidden=jnp.minimum(k // n_runs, n_hidden_tiles - 1))
    return plan, eot


def _grouped_up_kernel(cnt_ref, run_ref, last_ref, pe_ref, ph_ref, x_ref, wg_hbm, wu_hbm, o_ref,
                       wf_ref, wb_even_ref, wb_odd_ref, sem, *, tf):
    j, t = pl.program_id(0), pl.program_id(1)
    nt, n_runs, n_phases = cnt_ref[0], cnt_ref[1], cnt_ref[2]

    def fetch(p):
        cols = pl.ds(pl.multiple_of(ph_ref[p] * tf, tf), tf)
        return (pltpu.make_async_copy(wg_hbm.at[pe_ref[p], :, cols], wf_ref.at[0], sem.at[0]),
                pltpu.make_async_copy(wu_hbm.at[pe_ref[p], :, cols], wf_ref.at[1], sem.at[1]))

    def start(p):
        for c in fetch(p):
            c.start()

    def stage(p, dst_ref):
        for c in fetch(p):
            c.wait()
        dst_ref[0] = wf_ref[0].astype(BF16)
        dst_ref[1] = wf_ref[1].astype(BF16)

    def compute(w_ref):
        x = x_ref[...]
        g = jnp.dot(x, w_ref[0], preferred_element_type=F32)
        u = jnp.dot(x, w_ref[1], preferred_element_type=F32)
        o_ref[...] = (_silu(g) * u).astype(o_ref.dtype)

    active = t < nt
    p = j * n_runs + run_ref[t]
    hand_over = active & (last_ref[t] == 1) & (p + 1 < n_phases)

    @pl.when(active & (j == 0) & (t == 0))
    def _():
        start(0)
        stage(0, wb_even_ref)

        @pl.when(n_phases > 1)
        def _():
            start(1)

    def step(cur_ref, nxt_ref):
        @pl.when(jnp.logical_not(hand_over))
        def _():
            compute(cur_ref)

        @pl.when(hand_over)
        def _():
            stage(p + 1, nxt_ref)
            compute(cur_ref)

            @pl.when(p + 2 < n_phases)
            def _():
                start(p + 2)

    @pl.when(active & (p % 2 == 0))
    def _():
        step(wb_even_ref, wb_odd_ref)

    @pl.when(active & (p % 2 == 1))
    def _():
        step(wb_odd_ref, wb_even_ref)

    @pl.when(jnp.logical_not(active))
    def _():
        o_ref[...] = jnp.zeros(o_ref.shape, o_ref.dtype)


def grouped_swiglu_up(plan, xs, wg, wu, tf=512):
    r, k = xs.shape
    f = wg.shape[2]
    row = lambda j, t, cnt, *_: (jnp.maximum(jnp.minimum(t, cnt[0] - 1), 0), 0)
    return pl.pallas_call(
        functools.partial(_grouped_up_kernel, tf=tf),
        grid_spec=pltpu.PrefetchScalarGridSpec(
            num_scalar_prefetch=5,
            grid=(f // tf, r // MOE_TM),
            in_specs=[pl.BlockSpec((MOE_TM, k), row),
                      pl.BlockSpec(memory_space=pl.ANY),
                      pl.BlockSpec(memory_space=pl.ANY)],
            out_specs=pl.BlockSpec((MOE_TM, tf), lambda j, t, *_: (t, j)),
            scratch_shapes=[pltpu.VMEM((2, k, tf), F32), pltpu.VMEM((2, k, tf), BF16),
                            pltpu.VMEM((2, k, tf), BF16), pltpu.SemaphoreType.DMA((2,))]),
        out_shape=jax.ShapeDtypeStruct((r, f), BF16),
        compiler_params=_cparams("arbitrary", "arbitrary"),
        name="grouped_swiglu_up",
    )(*plan, xs, wg, wu)


def _moe_down_kernel(eot_ref, nt_ref, a_ref, w_ref, o_ref):
    @pl.when(pl.program_id(1) < nt_ref[0])
    def _():
        o_ref[...] = jnp.dot(a_ref[...], w_ref[...], preferred_element_type=F32)

    @pl.when(pl.program_id(1) >= nt_ref[0])
    def _():
        o_ref[...] = jnp.zeros(o_ref.shape, o_ref.dtype)


def moe_down(eot, nt, a, wd, tn=1024):
    r, f = a.shape
    n = wd.shape[2]
    return pl.pallas_call(
        _moe_down_kernel,
        grid_spec=pltpu.PrefetchScalarGridSpec(
            num_scalar_prefetch=2,
            grid=(n // tn, r // MOE_TM),
            in_specs=[pl.BlockSpec((MOE_TM, f), lambda j, t, eot, nt: (jnp.maximum(jnp.minimum(t, nt[0] - 1), 0), 0)),
                      pl.BlockSpec((None, f, tn), lambda j, t, eot, nt: (eot[t], 0, j))],
            out_specs=pl.BlockSpec((MOE_TM, tn), lambda j, t, eot, nt: (t, j))),
        out_shape=jax.ShapeDtypeStruct((r, n), F32),
        compiler_params=_cparams("arbitrary", "arbitrary"),
        name="moe_down",
    )(eot, nt, a, wd)


def moe_swiglu(h, norm_w, router_w, wg, wu, wd, post_norm_w=None):
    s, d = h.shape
    n_tiles_max = (2 * s) // MOE_TM + N_EXPERTS
    n_rows = n_tiles_max * MOE_TM
    rt = router(h, norm_w, router_w)
    dest, meta = moe_plan(rt)
    d1, d2 = dest[:, 0], dest[:, 1]
    plan, eot = _up_plan(meta[1, :N_EXPERTS], n_tiles_max, wg.shape[2] // UP_TF)
    nt = plan.counts[0:1]
    src = moe_invert(d1, d2, n_rows)
    xs = moe_gather_norm(src, nt, h, norm_w)
    a = grouped_swiglu_up(plan, xs, wg, wu, UP_TF)
    y = moe_down(eot, nt, a, wd)
    return moe_combine(d1, d2, h, rt, y, post_norm_w)


def _down_kernel(a_ref, w_ref, r_ref, o_ref):
    kk = pl.program_id(2)

    @pl.when(kk == 0)
    def _():
        o_ref[...] = r_ref[...]

    o_ref[...] += jnp.dot(a_ref[...], w_ref[...], preferred_element_type=F32)


def down_proj(a, w, resid, tk, tm=1024, tn=1024):
    m, k = a.shape
    n = w.shape[1]
    return pl.pallas_call(
        _down_kernel,
        grid=(m // tm, n // tn, k // tk),
        in_specs=[pl.BlockSpec((tm, tk), lambda i, j, kk: (i, kk)),
                  pl.BlockSpec((tk, tn), lambda i, j, kk: (kk, j)),
                  pl.BlockSpec((tm, tn), lambda i, j, kk: (i, j))],
        out_specs=pl.BlockSpec((tm, tn), lambda i, j, kk: (i, j)),
        out_shape=jax.ShapeDtypeStruct((m, n), F32),
        compiler_params=_cparams("parallel", "arbitrary", "arbitrary"),
        name="down_proj",
    )(a, w, resid)


def _softplus(x):
    return jnp.maximum(x, 0.0) + jnp.log1p(jnp.exp(-jnp.abs(x)))


def _tril_ones(n):
    r = lax.broadcasted_iota(jnp.int32, (n, n), 0)
    c = lax.broadcasted_iota(jnp.int32, (n, n), 1)
    return r >= c


def _ssd_kernel(z_ref, x_ref, b_ref, c_ref, sm_ref, convw_ref, convb_ref, smb_ref, alog_ref,
                dskip_ref, nw_ref, expand_ref, o_ref, tail_ref, state_ref):
    L = SSD_CHUNK
    HI = lax.Precision.HIGHEST

    @pl.when(pl.program_id(0) == 0)
    def _():
        tail_ref[...] = jnp.zeros(tail_ref.shape, F32)
        state_ref[...] = jnp.zeros(state_ref.shape, F32)

    row8 = lax.broadcasted_iota(jnp.int32, (SUBLANES, LANES), 0)

    def conv_silu(u, col0):
        w = u.shape[1]
        cols = slice(col0, col0 + w)
        tail = tail_ref[:, cols]
        acc = convb_ref[:, cols] + u * convw_ref[SSD_CONV - 1:SSD_CONV, cols]
        for k in range(1, SSD_CONV):
            rolled = pltpu.roll(u, k, axis=0)
            fix = jnp.where(jnp.tile(row8, (1, w // LANES)) < k, pltpu.roll(tail, k, axis=0), rolled[0:SUBLANES])
            shifted = jnp.concatenate([fix, rolled[SUBLANES:]], axis=0)
            acc = acc + shifted * convw_ref[SSD_CONV - 1 - k:SSD_CONV - k, cols]
        tail_ref[:, cols] = u[L - SUBLANES:L]
        return _silu(acc)

    xs = conv_silu(x_ref[...], 0)
    bmat = conv_silu(b_ref[...], SSD_INNER)
    cmat = conv_silu(c_ref[...], SSD_INNER + SSD_GROUPS * SSD_STATE)

    lane = lax.broadcasted_iota(jnp.int32, (L, LANES), 1)

    def expand_heads(v):
        hi = v.astype(BF16).astype(F32)
        r1 = v - hi
        mid = r1.astype(BF16).astype(F32)
        lo = r1 - mid
        packed = jnp.where(lane < 32, hi,
                           jnp.where(lane < 64, pltpu.roll(mid, 32, axis=1),
                                     jnp.where(lane < 96, pltpu.roll(lo, 64, axis=1), 0.0)))
        return jnp.dot(packed.astype(BF16), expand_ref[...], preferred_element_type=F32)

    causal = _tril_ones(L)
    dt = jnp.where(lane < SSD_HEADS, _softplus(sm_ref[...] + smb_ref[...]), 0.0)
    da = dt * (-jnp.exp(alog_ref[...]))
    acs = jnp.dot(causal.astype(F32), da, preferred_element_type=F32, precision=HI)
    acs_t = acs.T
    a_last = acs[L - 1:L, :]
    dt_e = expand_heads(dt)
    eacs_e = expand_heads(jnp.exp(acs))
    dst_e = expand_heads(jnp.exp(a_last - acs))
    xdt = xs * dt_e
    xdec = (xdt * dst_e).astype(BF16)
    xdt_b = xdt.astype(BF16)
    chunk_decay_e = eacs_e[L - 1:L, :]

    y = xs * dskip_ref[...]
    zgate = _silu(z_ref[...])
    lo_half = lane < SSD_HEAD_DIM

    def group(g):
        gs = slice(g * SSD_GROUP_WIDTH, (g + 1) * SSD_GROUP_WIDTH)
        bg = bmat[:, g * SSD_STATE:(g + 1) * SSD_STATE]
        cg = cmat[:, g * SSD_STATE:(g + 1) * SSD_STATE]
        cg_b = cg.astype(BF16)
        cb = lax.dot_general(cg_b, bg.astype(BF16), (((1,), (1,)), ((), ())),
                             preferred_element_type=F32)
        h_enter = state_ref[g]
        y_off = jnp.dot(cg_b, h_enter.astype(BF16), preferred_element_type=F32)
        st_new = jnp.dot(bg.T.astype(BF16), xdec[:, gs], preferred_element_type=F32)
        state_ref[g] = chunk_decay_e[:, gs] * h_enter + st_new
        yd_parts = []
        for p in range(4):
            j0 = g * 8 + 2 * p
            ms = []
            for j in (j0, j0 + 1):
                seg = jnp.where(causal, acs[:, j:j + 1] - acs_t[j:j + 1, :], -jnp.inf)
                ms.append((cb * jnp.exp(seg)).astype(BF16))
            xp = xdt_b[:, g * SSD_GROUP_WIDTH + p * LANES:g * SSD_GROUP_WIDTH + (p + 1) * LANES]
            zero = jnp.zeros_like(xp)
            rhs = jnp.concatenate([jnp.where(lo_half, xp, zero), jnp.where(lo_half, zero, xp)], axis=0)
            yd_parts.append(jnp.dot(jnp.concatenate(ms, axis=1), rhs, preferred_element_type=F32))
        y_diag = jnp.concatenate(yd_parts, axis=1)
        yg = (y[:, gs] + y_diag + y_off * eacs_e[:, gs]) * zgate[:, gs]
        msq = jnp.mean(yg * yg, axis=1, keepdims=True)
        o_ref[:, gs] = (yg * lax.rsqrt(msq + NORM_EPS) * nw_ref[:, gs]).astype(o_ref.dtype)

    return group


def _log_sigmoid(x):
    return jnp.minimum(x, 0.0) - jnp.log1p(jnp.exp(-jnp.abs(x)))


def _mlstm_kernel(q_ref, k_ref, v_ref, o_ref, sm_ref, smb_ref, nw_ref, out_ref, c_ref, n_ref, m_ref):
    L = MLSTM_CHUNK
    HI = lax.Precision.HIGHEST

    @pl.when(pl.program_id(0) == 0)
    def _():
        c_ref[...] = jnp.zeros(c_ref.shape, F32)
        n_ref[...] = jnp.zeros(n_ref.shape, F32)
        m_ref[...] = jnp.zeros(m_ref.shape, F32)

    causal = _tril_ones(L)
    sm = sm_ref[...] + smb_ref[...]
    bcs = jnp.dot(causal.astype(F32), _log_sigmoid(sm), preferred_element_type=F32, precision=HI)
    sm_t = sm.T
    bcs_t = bcs.T

    def head(h):
        li_col = sm[:, SMALL_I + h:SMALL_I + h + 1]
        b_col = bcs[:, SMALL_F + h:SMALL_F + h + 1]
        r_row = sm_t[SMALL_I + h:SMALL_I + h + 1, :] - bcs_t[SMALL_F + h:SMALL_F + h + 1, :]
        m_prev = m_ref[h:h + 1, 0:1]
        inter = b_col + m_prev
        dm = jnp.where(causal, b_col + r_row, -jnp.inf)
        m_t = jnp.maximum(inter, jnp.max(dm, axis=1, keepdims=True))
        w_intra = jnp.exp(dm - m_t)
        w_inter = jnp.exp(inter - m_t)
        qf = q_ref[:, h * MLSTM_DQK:(h + 1) * MLSTM_DQK] * (MLSTM_DQK ** -0.5)
        kf = k_ref[:, h * MLSTM_DQK:(h + 1) * MLSTM_DQK]
        qb = qf.astype(BF16)
        vb = v_ref[:, h * MLSTM_DV:(h + 1) * MLSTM_DV].astype(BF16)
        sqk = lax.dot_general(qb, kf.astype(BF16), (((1,), (1,)), ((), ())),
                              preferred_element_type=F32) * w_intra
        c_st = c_ref[h]
        n_row = n_ref[h:h + 1, :]
        num = (w_inter * jnp.dot(qb, c_st.astype(BF16), preferred_element_type=F32)
               + jnp.dot(sqk.astype(BF16), vb, preferred_element_type=F32))
        qn = jnp.sum(qf * n_row, axis=1, keepdims=True)
        den = w_inter * qn + jnp.sum(sqk, axis=1, keepdims=True)
        hout = num / jnp.maximum(jnp.abs(den), jnp.exp(-m_t))
        b_last = b_col[L - 1:L, :]
        g_col = b_last - b_col + li_col
        m_new = jnp.maximum(b_last + m_prev, jnp.max(g_col, axis=0, keepdims=True))
        decay = jnp.exp(b_last + m_prev - m_new)
        kw = kf * jnp.exp(g_col - m_new)
        c_ref[h] = decay * c_st + jnp.dot(kw.T.astype(BF16), vb, preferred_element_type=F32)
        n_ref[h:h + 1, :] = decay * n_row + jnp.sum(kw, axis=0, keepdims=True)
        m_ref[h:h + 1, :] = jnp.broadcast_to(m_new, (1, LANES))
        vs = slice(h * MLSTM_DV, (h + 1) * MLSTM_DV)
        msq = jnp.mean(hout * hout, axis=1, keepdims=True)
        hn = hout * lax.rsqrt(msq + NORM_EPS) * nw_ref[:, vs]
        out_ref[:, vs] = (jax.nn.sigmoid(o_ref[:, vs]) * hn).astype(out_ref.dtype)

    return head


assert MLSTM_CHUNK == SSD_CHUNK


def _mixer_kernel(z_ref, x_ref, b_ref, c_ref, q_ref, k_ref, v_ref, og_ref, sm_ref,
                  convw_ref, convb_ref, smb_ref, alog_ref, dskip_ref, ssd_nw_ref, expand_ref, ml_nw_ref,
                  y_ssd_ref, y_ml_ref, tail_ref, state_ref, cst_ref, nst_ref, mst_ref):
    mlstm_head = _mlstm_kernel(q_ref, k_ref, v_ref, og_ref, sm_ref, smb_ref, ml_nw_ref, y_ml_ref,
                               cst_ref, nst_ref, mst_ref)
    ssd_group = _ssd_kernel(z_ref, x_ref, b_ref, c_ref, sm_ref, convw_ref, convb_ref, smb_ref, alog_ref,
                            dskip_ref, ssd_nw_ref, expand_ref, y_ssd_ref, tail_ref, state_ref)
    for h in range(MLSTM_HEADS):
        mlstm_head(h)
    for g in range(SSD_GROUPS):
        ssd_group(g)


def hybrid_mixer(proj_a, proj_b, proj_s, conv_w, conv_b, small_bias, a_log, d_skip, ssd_norm_w, ml_norm_w):
    s = proj_a.shape[0]
    L = SSD_CHUNK
    conv_dim = conv_w.shape[1]
    qk_w = MLSTM_HEADS * MLSTM_DQK
    alog = jnp.pad(a_log, (0, LANES - SSD_HEADS)).reshape(1, LANES)
    dskip_e = jnp.repeat(d_skip, SSD_HEAD_DIM).reshape(1, SSD_INNER)
    piece_row = jnp.arange(LANES)[:, None]
    expand = ((jnp.arange(SSD_INNER)[None, :] // SSD_HEAD_DIM == piece_row % SSD_HEADS)
              & (piece_row < 3 * SSD_HEADS)).astype(BF16)
    full = lambda shp: pl.BlockSpec(shp, lambda i: (0,) * len(shp))
    blk = lambda w, col: pl.BlockSpec((L, w), lambda i: (i, col // w))
    return pl.pallas_call(
        _mixer_kernel,
        grid=(s // L,),
        in_specs=[blk(SSD_INNER, PA_Z), blk(SSD_INNER, PA_X), blk(512, PA_B), blk(512, PA_C),
                  blk(qk_w, PB_Q), blk(qk_w, PB_K), blk(MLSTM_WIDTH, PB_V), blk(MLSTM_WIDTH, PB_O),
                  blk(LANES, 0),
                  full((SSD_CONV, conv_dim)), full((1, conv_dim)), full((1, LANES)), full((1, LANES)),
                  full((1, SSD_INNER)), full((1, SSD_INNER)), full((LANES, SSD_INNER)),
                  full((1, MLSTM_WIDTH))],
        out_specs=[pl.BlockSpec((L, SSD_INNER), lambda i: (i, 0)),
                   pl.BlockSpec((L, MLSTM_WIDTH), lambda i: (i, 0))],
        out_shape=[jax.ShapeDtypeStruct((s, SSD_INNER), BF16),
                   jax.ShapeDtypeStruct((s, MLSTM_WIDTH), BF16)],
        scratch_shapes=[pltpu.VMEM((SUBLANES, conv_dim), F32),
                        pltpu.VMEM((SSD_GROUPS, SSD_STATE, SSD_GROUP_WIDTH), F32),
                        pltpu.VMEM((MLSTM_HEADS, MLSTM_DQK, MLSTM_DV), F32),
                        pltpu.VMEM((SUBLANES, MLSTM_DQK), F32),
                        pltpu.VMEM((SUBLANES, LANES), F32)],
        compiler_params=_cparams("arbitrary"),
        name="hybrid_mixer",
    )(proj_a, proj_a, proj_a, proj_a, proj_b, proj_b, proj_b, proj_b, proj_s,
      conv_w, conv_b.reshape(1, conv_dim), small_bias, alog, dskip_e,
      ssd_norm_w.reshape(1, SSD_INNER), expand, ml_norm_w.reshape(1, MLSTM_WIDTH))


def _split_w_in(w_in):
    w_a = w_in[:, :PA_WIDTH].astype(BF16)
    w_b = w_in[:, PB_START:PB_START + PB_WIDTH].astype(BF16)
    w_s = jnp.concatenate([w_in[:, IN_DT:IN_DT + SSD_HEADS], w_in[:, IN_I:IN_I + MLSTM_HEADS],
                           w_in[:, IN_F:IN_F + MLSTM_HEADS],
                           jnp.zeros((w_in.shape[0], LANES - SSD_HEADS - 2 * MLSTM_HEADS), w_in.dtype)],
                          axis=1).astype(BF16)
    return w_a, w_b, w_s


def _small_bias(dt_bias, i_bias, f_bias):
    v = jnp.concatenate([dt_bias, i_bias, f_bias, jnp.zeros((LANES - 40,), F32)])
    return v.reshape(1, LANES)


def kernel(x, norm_mix_w, w_in, conv_w, conv_b, dt_bias, a_log, d_skip, ssd_norm_w, mlstm_i_bias,
           mlstm_f_bias, mlstm_norm_w, w_out, norm_ffn_w, ffn_w_gate, ffn_w_up, ffn_w_down, router_w,
           moe_w_gate, moe_w_up, moe_w_down, final_norm_w):
    bsz, s, d = x.shape
    depth = w_in.shape[0]
    outs = []
    for b in range(bsz):
        h = x[b]
        normed = False
        for layer in range(depth):
            w_a, w_b, w_s = _split_w_in(w_in[layer])
            u, proj_s = rmsnorm_small(h, norm_mix_w[layer], w_s)
            proj_a = matmul(u, w_a, F32, tm=1024, tn=PA_TN)
            proj_b = matmul(u, w_b, F32, tm=1024, tn=PB_TN)
            sbias = _small_bias(dt_bias[layer], mlstm_i_bias[layer], mlstm_f_bias[layer])
            y_ssd, y_ml = hybrid_mixer(proj_a, proj_b, proj_s, conv_w[layer], conv_b[layer], sbias,
                                       a_log[layer], d_skip[layer], ssd_norm_w[layer], mlstm_norm_w[layer])
            h = out_proj(y_ssd, y_ml, w_out[layer].astype(BF16), h)
            j = layer // 2
            if layer % 2 == 0:
                u = rmsnorm(h, norm_ffn_w[layer], BF16)
                plan, _ = _up_plan(jnp.full((1,), s // MOE_TM, jnp.int32), s // MOE_TM, FF_DENSE // UP_TF)
                a = grouped_swiglu_up(plan, u, ffn_w_gate[j][None], ffn_w_up[j][None], UP_TF)
                h = down_proj(a, ffn_w_down[j].astype(BF16), h, tk=FF_DENSE // 4)
            else:
                normed = layer == depth - 1
                h = moe_swiglu(h, norm_ffn_w[layer], router_w[j], moe_w_gate[j], moe_w_up[j],
                               moe_w_down[j].astype(BF16), final_norm_w if normed else None)
        outs.append(h if normed else rmsnorm(h, final_norm_w, F32))
    return jnp.stack(outs, axis=0)
```

```python
import functools
from typing import NamedTuple

import jax
import jax.numpy as jnp
from jax import lax
from jax.experimental import pallas as pl
from jax.experimental.pallas import tpu as pltpu

F32 = jnp.float32
BF16 = jnp.bfloat16

D_MODEL = 4096
SSD_INNER = 2048
SSD_HEAD_DIM = 64
SSD_HEADS = 32
SSD_GROUPS = 4
SSD_STATE = 128
SSD_CONV = 4
SSD_CHUNK = 128
SSD_GROUP_WIDTH = SSD_INNER // SSD_GROUPS
MLSTM_WIDTH = 2048
MLSTM_HEADS = 4
MLSTM_DV = 512
MLSTM_DQK = 256
FF_DENSE = 14336
N_EXPERTS = 8
FF_EXPERT = 5632
NORM_EPS = 1e-6

LANES = 128
SUBLANES = 8
VMEM_LIMIT_BYTES = 56 * 1024 * 1024

PA_WIDTH, PA_TN = 5120, 1280
PA_Z, PA_X, PA_B, PA_C = 0, 2048, 4096, 4608
PB_START, PB_WIDTH, PB_TN = 5152, 6144, 1024
PB_Q, PB_K, PB_V, PB_O = 0, 1024, 2048, 4096
IN_DT, IN_I, IN_F = 5120, 11296, 11300
SMALL_DT, SMALL_I, SMALL_F = 0, 32, 36

MLSTM_CHUNK = 128


def _cparams(*sem):
    return pltpu.CompilerParams(dimension_semantics=sem, vmem_limit_bytes=VMEM_LIMIT_BYTES)


def _rmsnorm_kernel(x_ref, w_ref, o_ref):
    x = x_ref[...]
    ms = jnp.mean(x * x, axis=-1, keepdims=True)
    o_ref[...] = (x * lax.rsqrt(ms + NORM_EPS) * w_ref[...]).astype(o_ref.dtype)


def rmsnorm(x, w, out_dtype, tr=512):
    s, d = x.shape
    return pl.pallas_call(
        _rmsnorm_kernel,
        grid=(s // tr,),
        in_specs=[pl.BlockSpec((tr, d), lambda i: (i, 0)),
                  pl.BlockSpec((1, d), lambda i: (0, 0))],
        out_specs=pl.BlockSpec((tr, d), lambda i: (i, 0)),
        out_shape=jax.ShapeDtypeStruct((s, d), out_dtype),
        compiler_params=_cparams("parallel"),
        name="rmsnorm",
    )(x, w.reshape(1, d))


def _rmsnorm_small_kernel(x_ref, w_ref, ws_ref, u_ref, ps_ref):
    x = x_ref[...]
    ms = jnp.mean(x * x, axis=-1, keepdims=True)
    u = (x * lax.rsqrt(ms + NORM_EPS) * w_ref[...]).astype(BF16)
    u_ref[...] = u
    ps_ref[...] = jnp.dot(u, ws_ref[...], preferred_element_type=F32)


def rmsnorm_small(x, w, w_small, tr=512):
    s, d = x.shape
    return pl.pallas_call(
        _rmsnorm_small_kernel,
        grid=(s // tr,),
        in_specs=[pl.BlockSpec((tr, d), lambda i: (i, 0)),
                  pl.BlockSpec((1, d), lambda i: (0, 0)),
                  pl.BlockSpec((d, LANES), lambda i: (0, 0))],
        out_specs=[pl.BlockSpec((tr, d), lambda i: (i, 0)),
                   pl.BlockSpec((tr, LANES), lambda i: (i, 0))],
        out_shape=[jax.ShapeDtypeStruct((s, d), BF16),
                   jax.ShapeDtypeStruct((s, LANES), F32)],
        compiler_params=_cparams("parallel"),
        name="rmsnorm_small",
    )(x, w.reshape(1, d), w_small)


RT_SEL = 0
RT_C1, RT_C2 = 8, 9
RT_I1, RT_I2 = 10, 11


def _router_kernel(x_ref, w_ref, rw_ref, rt_ref):
    x = x_ref[...]
    ms = jnp.mean(x * x, axis=-1, keepdims=True)
    u = x * lax.rsqrt(ms + NORM_EPS) * w_ref[...]
    logits = jnp.dot(u, rw_ref[...], preferred_element_type=F32, precision=lax.Precision.HIGHEST)
    lane = lax.broadcasted_iota(jnp.int32, logits.shape, 1)
    valid = lane < N_EXPERTS
    lg = jnp.where(valid, logits, -jnp.inf)
    ex = jnp.exp(lg - jnp.max(lg, axis=1, keepdims=True))
    probs = ex / jnp.sum(ex, axis=1, keepdims=True)
    probs = jnp.where(valid, probs, -1.0)
    p1 = jnp.max(probs, axis=1, keepdims=True)
    i1 = jnp.min(jnp.where(probs == p1, lane, LANES), axis=1, keepdims=True)
    rest = jnp.where(lane == i1, -1.0, probs)
    p2 = jnp.max(rest, axis=1, keepdims=True)
    i2 = jnp.min(jnp.where(rest == p2, lane, LANES), axis=1, keepdims=True)
    tot = p1 + p2
    rt = jnp.where((lane == i1) | (lane == i2), 1.0, 0.0)
    rt = jnp.where(lane == RT_C1, p1 / tot, rt)
    rt = jnp.where(lane == RT_C2, p2 / tot, rt)
    rt = jnp.where(lane == RT_I1, i1.astype(F32), rt)
    rt = jnp.where(lane == RT_I2, i2.astype(F32), rt)
    rt_ref[...] = rt


def router(x, w, router_w, tr=512):
    s, d = x.shape
    rw = jnp.pad(router_w, ((0, 0), (0, LANES - N_EXPERTS)))
    return pl.pallas_call(
        _router_kernel,
        grid=(s // tr,),
        in_specs=[pl.BlockSpec((tr, d), lambda i: (i, 0)),
                  pl.BlockSpec((1, d), lambda i: (0, 0)),
                  pl.BlockSpec((d, LANES), lambda i: (0, 0))],
        out_specs=pl.BlockSpec((tr, LANES), lambda i: (i, 0)),
        out_shape=jax.ShapeDtypeStruct((s, LANES), F32),
        compiler_params=_cparams("parallel"),
        name="router",
    )(x, w.reshape(1, d), rw)


MOE_TM = 512
UP_TF = 512
PLAN_BLK = 512


def _plan_kernel(rt_ref, dest_ref, meta_ref):
    s = rt_ref.shape[0]
    nblk = s // PLAN_BLK
    lane = lax.broadcasted_iota(jnp.int32, (PLAN_BLK, LANES), 1)
    is_exp = lane < N_EXPERTS
    r = lax.broadcasted_iota(jnp.int32, (PLAN_BLK, PLAN_BLK), 0)
    c = lax.broadcasted_iota(jnp.int32, (PLAN_BLK, PLAN_BLK), 1)
    strict_tril = (r > c).astype(BF16)

    def rank_body(b, carry):
        rows = pl.ds(pl.multiple_of(b * PLAN_BLK, PLAN_BLK), PLAN_BLK)
        sel = jnp.where(is_exp, rt_ref[rows, :], 0.0)
        rank = jnp.dot(strict_tril, sel.astype(BF16), preferred_element_type=F32) + carry
        dest_ref[rows, :] = rank.astype(jnp.int32)
        return carry + jnp.sum(sel, axis=0, keepdims=True)

    counts = lax.fori_loop(0, nblk, rank_body, jnp.zeros((1, LANES), F32))
    tiles = jnp.floor((counts + (MOE_TM - 1)) * (1.0 / MOE_TM))
    er = lax.broadcasted_iota(jnp.int32, (LANES, LANES), 0)
    ec = lax.broadcasted_iota(jnp.int32, (LANES, LANES), 1)
    before = (er < ec).astype(F32)
    tile_start = jnp.dot(jnp.broadcast_to(tiles, (SUBLANES, LANES)), before, preferred_element_type=F32,
                         precision=lax.Precision.HIGHEST)[0:1, :]
    row_off = tile_start * MOE_TM
    meta_ref[0:1, :] = counts.astype(jnp.int32)
    meta_ref[1:2, :] = tiles.astype(jnp.int32)
    meta_ref[2:3, :] = tile_start.astype(jnp.int32)
    meta_ref[3:SUBLANES, :] = jnp.zeros((SUBLANES - 3, LANES), jnp.int32)

    def dest_body(b, _):
        rows = pl.ds(pl.multiple_of(b * PLAN_BLK, PLAN_BLK), PLAN_BLK)
        rt = rt_ref[rows, :]
        pos = dest_ref[rows, :].astype(F32) + row_off
        i1 = rt[:, RT_I1:RT_I1 + 1].astype(jnp.int32)
        i2 = rt[:, RT_I2:RT_I2 + 1].astype(jnp.int32)
        d1 = jnp.sum(jnp.where(lane == i1, pos, 0.0), axis=1, keepdims=True)
        d2 = jnp.sum(jnp.where(lane == i2, pos, 0.0), axis=1, keepdims=True)
        dest_ref[rows, :] = jnp.where(lane == 0, d1, jnp.where(lane == 1, d2, 0.0)).astype(jnp.int32)
        return 0

    lax.fori_loop(0, nblk, dest_body, 0)


def moe_plan(rt):
    s = rt.shape[0]
    return pl.pallas_call(
        _plan_kernel,
        out_shape=[jax.ShapeDtypeStruct((s, LANES), jnp.int32),
                   jax.ShapeDtypeStruct((SUBLANES, LANES), jnp.int32)],
        compiler_params=pltpu.CompilerParams(vmem_limit_bytes=VMEM_LIMIT_BYTES),
        name="moe_plan",
    )(rt)


def _invert_kernel(d1_ref, d2_ref, src_ref):
    n_rows = src_ref.shape[0]
    n_tok = d1_ref.shape[0]

    def zero(i, _):
        src_ref[i] = 0
        return 0

    lax.fori_loop(0, n_rows, zero, 0, unroll=16)

    def put(t, _):
        src_ref[d1_ref[t]] = t
        src_ref[d2_ref[t]] = t
        return 0

    lax.fori_loop(0, n_tok, put, 0, unroll=8)


def moe_invert(d1, d2, n_rows):
    smem = pl.BlockSpec(memory_space=pltpu.SMEM)
    return pl.pallas_call(
        _invert_kernel,
        in_specs=[smem, smem],
        out_specs=smem,
        out_shape=jax.ShapeDtypeStruct((n_rows,), jnp.int32),
        name="moe_invert",
    )(d1, d2)


GATHER_TM = 256
GATHER_UNROLL = 8
GATHER_STREAMS = 4


def _gather_norm_kernel(src_ref, nt_ref, h_hbm, w_ref, o_ref, buf_ref, sem):
    i = pl.program_id(0)
    nt = nt_ref[0]

    per_stream = MOE_TM // GATHER_STREAMS

    def row_copy(tile, slot, q, r):
        row = q * per_stream + r
        return pltpu.make_async_copy(h_hbm.at[pl.ds(src_ref[tile * MOE_TM + row], 1), :],
                                     buf_ref.at[slot, pl.ds(row, 1), :], sem.at[slot, q])

    def issue(tile, slot):
        def body(r, _):
            for q in range(GATHER_STREAMS):
                row_copy(tile, slot, q, r).start()
            return 0
        lax.fori_loop(0, per_stream, body, 0, unroll=GATHER_UNROLL // GATHER_STREAMS)

    @pl.when(i == 0)
    def _():
        issue(0, 0)

    @pl.when(i + 1 < nt)
    def _():
        issue(i + 1, (i + 1) % 2)

    @pl.when(i < nt)
    def _():
        slot = i % 2

        for q in range(GATHER_STREAMS):
            pltpu.make_async_copy(h_hbm.at[pl.ds(0, per_stream), :],
                                  buf_ref.at[slot, pl.ds(q * per_stream, per_stream), :], sem.at[slot, q]).wait()
        x = buf_ref[slot]
        ms = jnp.mean(x * x, axis=-1, keepdims=True)
        o_ref[...] = (x * lax.rsqrt(ms + NORM_EPS) * w_ref[...]).astype(o_ref.dtype)

    @pl.when(i >= nt)
    def _():
        o_ref[...] = jnp.zeros(o_ref.shape, o_ref.dtype)


def moe_gather_norm(src, nt, h, w):
    n_rows = src.shape[0]
    d = h.shape[1]
    return pl.pallas_call(
        _gather_norm_kernel,
        grid_spec=pltpu.PrefetchScalarGridSpec(
            num_scalar_prefetch=2,
            grid=(n_rows // MOE_TM,),
            in_specs=[pl.BlockSpec(memory_space=pl.ANY),
                      pl.BlockSpec((1, d), lambda i, src, nt: (0, 0))],
            out_specs=pl.BlockSpec((MOE_TM, d), lambda i, src, nt: (i, 0)),
            scratch_shapes=[pltpu.VMEM((2, MOE_TM, d), F32), pltpu.SemaphoreType.DMA((2, GATHER_STREAMS))]),
        out_shape=jax.ShapeDtypeStruct((n_rows, d), BF16),
        compiler_params=_cparams("arbitrary"),
        name="moe_gather_norm",
    )(src, nt, h, w.reshape(1, d))


def _combine_kernel(d1_ref, d2_ref, h_ref, rt_ref, pw_ref, y_hbm, o_ref, ya_ref, yb_ref, sem, *, post_norm):
    i = pl.program_id(0)
    n = pl.num_programs(0)

    def copy_a(blk, slot, r):
        return pltpu.make_async_copy(y_hbm.at[pl.ds(d1_ref[blk * GATHER_TM + r], 1), :],
                                     ya_ref.at[slot, pl.ds(r, 1), :], sem.at[0, slot])

    def copy_b(blk, slot, r):
        return pltpu.make_async_copy(y_hbm.at[pl.ds(d2_ref[blk * GATHER_TM + r], 1), :],
                                     yb_ref.at[slot, pl.ds(r, 1), :], sem.at[1, slot])

    def issue(blk, slot):
        def body(r, _):
            copy_a(blk, slot, r).start()
            copy_b(blk, slot, r).start()
            return 0
        lax.fori_loop(0, GATHER_TM, body, 0, unroll=GATHER_UNROLL)

    @pl.when(i == 0)
    def _():
        issue(0, 0)

    @pl.when(i + 1 < n)
    def _():
        issue(i + 1, (i + 1) % 2)

    slot = i % 2

    pltpu.make_async_copy(y_hbm.at[pl.ds(0, GATHER_TM), :], ya_ref.at[slot], sem.at[0, slot]).wait()
    pltpu.make_async_copy(y_hbm.at[pl.ds(0, GATHER_TM), :], yb_ref.at[slot], sem.at[1, slot]).wait()
    rt = rt_ref[...]
    out = h_ref[...] + rt[:, RT_C1:RT_C1 + 1] * ya_ref[slot] + rt[:, RT_C2:RT_C2 + 1] * yb_ref[slot]
    if post_norm:
        ms = jnp.mean(out * out, axis=-1, keepdims=True)
        out = out * lax.rsqrt(ms + NORM_EPS) * pw_ref[...]
    o_ref[...] = out


def moe_combine(d1, d2, h, rt, y, post_norm_w):
    s, d = h.shape
    post_norm = post_norm_w is not None
    pw = (post_norm_w if post_norm else jnp.ones((d,), F32)).reshape(1, d)
    return pl.pallas_call(
        functools.partial(_combine_kernel, post_norm=post_norm),
        grid_spec=pltpu.PrefetchScalarGridSpec(
            num_scalar_prefetch=2,
            grid=(s // GATHER_TM,),
            in_specs=[pl.BlockSpec((GATHER_TM, d), lambda i, a, b: (i, 0)),
                      pl.BlockSpec((GATHER_TM, LANES), lambda i, a, b: (i, 0)),
                      pl.BlockSpec((1, d), lambda i, a, b: (0, 0)),
                      pl.BlockSpec(memory_space=pl.ANY)],
            out_specs=pl.BlockSpec((GATHER_TM, d), lambda i, a, b: (i, 0)),
            scratch_shapes=[pltpu.VMEM((2, GATHER_TM, d), F32), pltpu.VMEM((2, GATHER_TM, d), F32),
                            pltpu.SemaphoreType.DMA((2, 2))]),
        out_shape=jax.ShapeDtypeStruct((s, d), F32),
        compiler_params=_cparams("arbitrary"),
        name="moe_combine",
    )(d1, d2, h, rt, pw, y)


def _mm_kernel(x_ref, w_ref, o_ref):
    o_ref[...] = jnp.dot(x_ref[...], w_ref[...], preferred_element_type=F32).astype(o_ref.dtype)


def matmul(x, w, out_dtype, tm, tn):
    m, k = x.shape
    n = w.shape[1]
    return pl.pallas_call(
        _mm_kernel,
        grid=(m // tm, n // tn),
        in_specs=[pl.BlockSpec((tm, k), lambda i, j: (i, 0)),
                  pl.BlockSpec((k, tn), lambda i, j: (0, j))],
        out_specs=pl.BlockSpec((tm, tn), lambda i, j: (i, j)),
        out_shape=jax.ShapeDtypeStruct((m, n), out_dtype),
        compiler_params=_cparams("parallel", "arbitrary"),
        name="in_proj",
    )(x, w)


def _out_proj_kernel(ya_ref, yb_ref, wa_ref, wb_ref, r_ref, o_ref):
    acc = jnp.dot(ya_ref[...], wa_ref[...], preferred_element_type=F32)
    acc += jnp.dot(yb_ref[...], wb_ref[...], preferred_element_type=F32)
    o_ref[...] = r_ref[...] + acc


def out_proj(y_ssd, y_ml, w_out, resid, tm=1024, tn=1024):
    m, ka = y_ssd.shape
    n = w_out.shape[1]
    return pl.pallas_call(
        _out_proj_kernel,
        grid=(m // tm, n // tn),
        in_specs=[pl.BlockSpec((tm, ka), lambda i, j: (i, 0)),
                  pl.BlockSpec((tm, ka), lambda i, j: (i, 0)),
                  pl.BlockSpec((ka, tn), lambda i, j: (0, j)),
                  pl.BlockSpec((ka, tn), lambda i, j: (1, j)),
                  pl.BlockSpec((tm, tn), lambda i, j: (i, j))],
        out_specs=pl.BlockSpec((tm, tn), lambda i, j: (i, j)),
        out_shape=jax.ShapeDtypeStruct((m, n), F32),
        compiler_params=_cparams("parallel", "arbitrary"),
        name="out_proj",
    )(y_ssd, y_ml, w_out, w_out, resid)


def _silu(x):
    return x * jax.nn.sigmoid(x)


class UpPlan(NamedTuple):
    counts: jax.Array
    run_of_tile: jax.Array
    last_of_run: jax.Array
    phase_expert: jax.Array
    phase_hidden: jax.Array


def _up_plan(tiles, n_tiles_max, n_hidden_tiles):
    ne = tiles.shape[0]
    tile_end = jnp.cumsum(tiles)
    nt = tile_end[-1]
    tid = jnp.minimum(jnp.arange(n_tiles_max, dtype=jnp.int32), nt - 1)
    eot = jnp.sum((tid[:, None] >= tile_end[None, :]).astype(jnp.int32), axis=1)
    has = tiles > 0
    n_runs = jnp.sum(has.astype(jnp.int32))
    run_of_expert = jnp.cumsum(has.astype(jnp.int32)) - 1
    run_expert = jnp.argsort(jnp.logical_not(has), stable=True).astype(jnp.int32)
    k = jnp.arange(ne * n_hidden_tiles, dtype=jnp.int32)
    plan = UpPlan(counts=jnp.stack([nt, n_runs, n_runs * n_hidden_tiles]).astype(jnp.int32),
                  run_of_tile=run_of_expert[eot],
                  last_of_run=(tid == tile_end[eot] - 1).astype(jnp.int32),
                  phase_expert=run_expert[k % n_runs],
                  phase_hidden=jnp.minimum(k // n_runs, n_hidden_tiles - 1))
    return plan, eot


def _grouped_up_kernel(cnt_ref, run_ref, last_ref, pe_ref, ph_ref, x_ref, wg_hbm, wu_hbm, o_ref,
                       wf_ref, wb_even_ref, wb_odd_ref, sem, *, tf):
    j, t = pl.program_id(0), pl.program_id(1)
    nt, n_runs, n_phases = cnt_ref[0], cnt_ref[1], cnt_ref[2]

    def fetch(p):
        cols = pl.ds(pl.multiple_of(ph_ref[p] * tf, tf), tf)
        return (pltpu.make_async_copy(wg_hbm.at[pe_ref[p], :, cols], wf_ref.at[0], sem.at[0]),
                pltpu.make_async_copy(wu_hbm.at[pe_ref[p], :, cols], wf_ref.at[1], sem.at[1]))

    def start(p):
        for c in fetch(p):
            c.start()

    def stage(p, dst_ref):
        for c in fetch(p):
            c.wait()
        dst_ref[0] = wf_ref[0].astype(BF16)
        dst_ref[1] = wf_ref[1].astype(BF16)

    def compute(w_ref):
        x = x_ref[...]
        g = jnp.dot(x, w_ref[0], preferred_element_type=F32)
        u = jnp.dot(x, w_ref[1], preferred_element_type=F32)
        o_ref[...] = (_silu(g) * u).astype(o_ref.dtype)

    active = t < nt
    p = j * n_runs + run_ref[t]
    hand_over = active & (last_ref[t] == 1) & (p + 1 < n_phases)

    @pl.when(active & (j == 0) & (t == 0))
    def _():
        start(0)
        stage(0, wb_even_ref)

        @pl.when(n_phases > 1)
        def _():
            start(1)

    def step(cur_ref, nxt_ref):
        @pl.when(jnp.logical_not(hand_over))
        def _():
            compute(cur_ref)

        @pl.when(hand_over)
        def _():
            stage(p + 1, nxt_ref)
            compute(cur_ref)

            @pl.when(p + 2 < n_phases)
            def _():
                start(p + 2)

    @pl.when(active & (p % 2 == 0))
    def _():
        step(wb_even_ref, wb_odd_ref)

    @pl.when(active & (p % 2 == 1))
    def _():
        step(wb_odd_ref, wb_even_ref)

    @pl.when(jnp.logical_not(active))
    def _():
        o_ref[...] = jnp.zeros(o_ref.shape, o_ref.dtype)


def grouped_swiglu_up(plan, xs, wg, wu, tf=512):
    r, k = xs.shape
    f = wg.shape[2]
    row = lambda j, t, cnt, *_: (jnp.maximum(jnp.minimum(t, cnt[0] - 1), 0), 0)
    return pl.pallas_call(
        functools.partial(_grouped_up_kernel, tf=tf),
        grid_spec=pltpu.PrefetchScalarGridSpec(
            num_scalar_prefetch=5,
            grid=(f // tf, r // MOE_TM),
            in_specs=[pl.BlockSpec((MOE_TM, k), row),
                      pl.BlockSpec(memory_space=pl.ANY),
                      pl.BlockSpec(memory_space=pl.ANY)],
            out_specs=pl.BlockSpec((MOE_TM, tf), lambda j, t, *_: (t, j)),
            scratch_shapes=[pltpu.VMEM((2, k, tf), F32), pltpu.VMEM((2, k, tf), BF16),
                            pltpu.VMEM((2, k, tf), BF16), pltpu.SemaphoreType.DMA((2,))]),
        out_shape=jax.ShapeDtypeStruct((r, f), BF16),
        compiler_params=_cparams("arbitrary", "arbitrary"),
        name="grouped_swiglu_up",
    )(*plan, xs, wg, wu)


def _moe_down_kernel(eot_ref, nt_ref, a_ref, w_ref, o_ref):
    @pl.when(pl.program_id(1) < nt_ref[0])
    def _():
        o_ref[...] = jnp.dot(a_ref[...], w_ref[...], preferred_element_type=F32)

    @pl.when(pl.program_id(1) >= nt_ref[0])
    def _():
        o_ref[...] = jnp.zeros(o_ref.shape, o_ref.dtype)


def moe_down(eot, nt, a, wd, tn=1024):
    r, f = a.shape
    n = wd.shape[2]
    return pl.pallas_call(
        _moe_down_kernel,
        grid_spec=pltpu.PrefetchScalarGridSpec(
            num_scalar_prefetch=2,
            grid=(n // tn, r // MOE_TM),
            in_specs=[pl.BlockSpec((MOE_TM, f), lambda j, t, eot, nt: (jnp.maximum(jnp.minimum(t, nt[0] - 1), 0), 0)),
                      pl.BlockSpec((None, f, tn), lambda j, t, eot, nt: (eot[t], 0, j))],
            out_specs=pl.BlockSpec((MOE_TM, tn), lambda j, t, eot, nt: (t, j))),
        out_shape=jax.ShapeDtypeStruct((r, n), F32),
        compiler_params=_cparams("arbitrary", "arbitrary"),
        name="moe_down",
    )(eot, nt, a, wd)


def moe_swiglu(h, norm_w, router_w, wg, wu, wd, post_norm_w=None):
    s, d = h.shape
    n_tiles_max = (2 * s) // MOE_TM + N_EXPERTS
    n_rows = n_tiles_max * MOE_TM
    rt = router(h, norm_w, router_w)
    dest, meta = moe_plan(rt)
    d1, d2 = dest[:, 0], dest[:, 1]
    plan, eot = _up_plan(meta[1, :N_EXPERTS], n_tiles_max, wg.shape[2] // UP_TF)
    nt = plan.counts[0:1]
    src = moe_invert(d1, d2, n_rows)
    xs = moe_gather_norm(src, nt, h, norm_w)
    a = grouped_swiglu_up(plan, xs, wg, wu, UP_TF)
    y = moe_down(eot, nt, a, wd)
    return moe_combine(d1, d2, h, rt, y, post_norm_w)


def _down_kernel(a_ref, w_ref, r_ref, o_ref):
    kk = pl.program_id(2)

    @pl.when(kk == 0)
    def _():
        o_ref[...] = r_ref[...]

    o_ref[...] += jnp.dot(a_ref[...], w_ref[...], preferred_element_type=F32)


def down_proj(a, w, resid, tk, tm=1024, tn=1024):
    m, k = a.shape
    n = w.shape[1]
    return pl.pallas_call(
        _down_kernel,
        grid=(m // tm, n // tn, k // tk),
        in_specs=[pl.BlockSpec((tm, tk), lambda i, j, kk: (i, kk)),
                  pl.BlockSpec((tk, tn), lambda i, j, kk: (kk, j)),
                  pl.BlockSpec((tm, tn), lambda i, j, kk: (i, j))],
        out_specs=pl.BlockSpec((tm, tn), lambda i, j, kk: (i, j)),
        out_shape=jax.ShapeDtypeStruct((m, n), F32),
        compiler_params=_cparams("parallel", "arbitrary", "arbitrary"),
        name="down_proj",
    )(a, w, resid)


def _softplus(x):
    return jnp.maximum(x, 0.0) + jnp.log1p(jnp.exp(-jnp.abs(x)))


def _tril_ones(n):
    r = lax.broadcasted_iota(jnp.int32, (n, n), 0)
    c = lax.broadcasted_iota(jnp.int32, (n, n), 1)
    return r >= c


def _ssd_kernel(z_ref, x_ref, b_ref, c_ref, sm_ref, convw_ref, convb_ref, smb_ref, alog_ref,
                dskip_ref, nw_ref, expand_ref, o_ref, tail_ref, state_ref):
    L = SSD_CHUNK
    HI = lax.Precision.HIGHEST

    @pl.when(pl.program_id(0) == 0)
    def _():
        tail_ref[...] = jnp.zeros(tail_ref.shape, F32)
        state_ref[...] = jnp.zeros(state_ref.shape, F32)

    row8 = lax.broadcasted_iota(jnp.int32, (SUBLANES, LANES), 0)

    def conv_silu(u, col0):
        w = u.shape[1]
        cols = slice(col0, col0 + w)
        tail = tail_ref[:, cols]
        acc = convb_ref[:, cols] + u * convw_ref[SSD_CONV - 1:SSD_CONV, cols]
        for k in range(1, SSD_CONV):
            rolled = pltpu.roll(u, k, axis=0)
            fix = jnp.where(jnp.tile(row8, (1, w // LANES)) < k, pltpu.roll(tail, k, axis=0), rolled[0:SUBLANES])
            shifted = jnp.concatenate([fix, rolled[SUBLANES:]], axis=0)
            acc = acc + shifted * convw_ref[SSD_CONV - 1 - k:SSD_CONV - k, cols]
        tail_ref[:, cols] = u[L - SUBLANES:L]
        return _silu(acc)

    xs = conv_silu(x_ref[...], 0)
    bmat = conv_silu(b_ref[...], SSD_INNER)
    cmat = conv_silu(c_ref[...], SSD_INNER + SSD_GROUPS * SSD_STATE)

    lane = lax.broadcasted_iota(jnp.int32, (L, LANES), 1)

    def expand_heads(v):
        hi = v.astype(BF16).astype(F32)
        r1 = v - hi
        mid = r1.astype(BF16).astype(F32)
        lo = r1 - mid
        packed = jnp.where(lane < 32, hi,
                           jnp.where(lane < 64, pltpu.roll(mid, 32, axis=1),
                                     jnp.where(lane < 96, pltpu.roll(lo, 64, axis=1), 0.0)))
        return jnp.dot(packed.astype(BF16), expand_ref[...], preferred_element_type=F32)

    causal = _tril_ones(L)
    dt = jnp.where(lane < SSD_HEADS, _softplus(sm_ref[...] + smb_ref[...]), 0.0)
    da = dt * (-jnp.exp(alog_ref[...]))
    acs = jnp.dot(causal.astype(F32), da, preferred_element_type=F32, precision=HI)
    acs_t = acs.T
    a_last = acs[L - 1:L, :]
    dt_e = expand_heads(dt)
    eacs_e = expand_heads(jnp.exp(acs))
    dst_e = expand_heads(jnp.exp(a_last - acs))
    xdt = xs * dt_e
    xdec = (xdt * dst_e).astype(BF16)
    xdt_b = xdt.astype(BF16)
    chunk_decay_e = eacs_e[L - 1:L, :]

    y = xs * dskip_ref[...]
    zgate = _silu(z_ref[...])
    lo_half = lane < SSD_HEAD_DIM

    def group(g):
        gs = slice(g * SSD_GROUP_WIDTH, (g + 1) * SSD_GROUP_WIDTH)
        bg = bmat[:, g * SSD_STATE:(g + 1) * SSD_STATE]
        cg = cmat[:, g * SSD_STATE:(g + 1) * SSD_STATE]
        cg_b = cg.astype(BF16)
        cb = lax.dot_general(cg_b, bg.astype(BF16), (((1,), (1,)), ((), ())),
                             preferred_element_type=F32)
        h_enter = state_ref[g]
        y_off = jnp.dot(cg_b, h_enter.astype(BF16), preferred_element_type=F32)
        st_new = jnp.dot(bg.T.astype(BF16), xdec[:, gs], preferred_element_type=F32)
        state_ref[g] = chunk_decay_e[:, gs] * h_enter + st_new
        yd_parts = []
        for p in range(4):
            j0 = g * 8 + 2 * p
            ms = []
            for j in (j0, j0 + 1):
                seg = jnp.where(causal, acs[:, j:j + 1] - acs_t[j:j + 1, :], -jnp.inf)
                ms.append((cb * jnp.exp(seg)).astype(BF16))
            xp = xdt_b[:, g * SSD_GROUP_WIDTH + p * LANES:g * SSD_GROUP_WIDTH + (p + 1) * LANES]
            zero = jnp.zeros_like(xp)
            rhs = jnp.concatenate([jnp.where(lo_half, xp, zero), jnp.where(lo_half, zero, xp)], axis=0)
            yd_parts.append(jnp.dot(jnp.concatenate(ms, axis=1), rhs, preferred_element_type=F32))
        y_diag = jnp.concatenate(yd_parts, axis=1)
        yg = (y[:, gs] + y_diag + y_off * eacs_e[:, gs]) * zgate[:, gs]
        msq = jnp.mean(yg * yg, axis=1, keepdims=True)
        o_ref[:, gs] = (yg * lax.rsqrt(msq + NORM_EPS) * nw_ref[:, gs]).astype(o_ref.dtype)

    return group


def _log_sigmoid(x):
    return jnp.minimum(x, 0.0) - jnp.log1p(jnp.exp(-jnp.abs(x)))


def _mlstm_kernel(q_ref, k_ref, v_ref, o_ref, sm_ref, smb_ref, nw_ref, out_ref, c_ref, n_ref, m_ref):
    L = MLSTM_CHUNK
    HI = lax.Precision.HIGHEST

    @pl.when(pl.program_id(0) == 0)
    def _():
        c_ref[...] = jnp.zeros(c_ref.shape, F32)
        n_ref[...] = jnp.zeros(n_ref.shape, F32)
        m_ref[...] = jnp.zeros(m_ref.shape, F32)

    causal = _tril_ones(L)
    sm = sm_ref[...] + smb_ref[...]
    bcs = jnp.dot(causal.astype(F32), _log_sigmoid(sm), preferred_element_type=F32, precision=HI)
    sm_t = sm.T
    bcs_t = bcs.T

    def head(h):
        li_col = sm[:, SMALL_I + h:SMALL_I + h + 1]
        b_col = bcs[:, SMALL_F + h:SMALL_F + h + 1]
        r_row = sm_t[SMALL_I + h:SMALL_I + h + 1, :] - bcs_t[SMALL_F + h:SMALL_F + h + 1, :]
        m_prev = m_ref[h:h + 1, 0:1]
        inter = b_col + m_prev
        dm = jnp.where(causal, b_col + r_row, -jnp.inf)
        m_t = jnp.maximum(inter, jnp.max(dm, axis=1, keepdims=True))
        w_intra = jnp.exp(dm - m_t)
        w_inter = jnp.exp(inter - m_t)
        qf = q_ref[:, h * MLSTM_DQK:(h + 1) * MLSTM_DQK] * (MLSTM_DQK ** -0.5)
        kf = k_ref[:, h * MLSTM_DQK:(h + 1) * MLSTM_DQK]
        qb = qf.astype(BF16)
        vb = v_ref[:, h * MLSTM_DV:(h + 1) * MLSTM_DV].astype(BF16)
        sqk = lax.dot_general(qb, kf.astype(BF16), (((1,), (1,)), ((), ())),
                              preferred_element_type=F32) * w_intra
        c_st = c_ref[h]
        n_row = n_ref[h:h + 1, :]
        num = (w_inter * jnp.dot(qb, c_st.astype(BF16), preferred_element_type=F32)
               + jnp.dot(sqk.astype(BF16), vb, preferred_element_type=F32))
        qn = jnp.sum(qf * n_row, axis=1, keepdims=True)
        den = w_inter * qn + jnp.sum(sqk, axis=1, keepdims=True)
        hout = num / jnp.maximum(jnp.abs(den), jnp.exp(-m_t))
        b_last = b_col[L - 1:L, :]
        g_col = b_last - b_col + li_col
        m_new = jnp.maximum(b_last + m_prev, jnp.max(g_col, axis=0, keepdims=True))
        decay = jnp.exp(b_last + m_prev - m_new)
        kw = kf * jnp.exp(g_col - m_new)
        c_ref[h] = decay * c_st + jnp.dot(kw.T.astype(BF16), vb, preferred_element_type=F32)
        n_ref[h:h + 1, :] = decay * n_row + jnp.sum(kw, axis=0, keepdims=True)
        m_ref[h:h + 1, :] = jnp.broadcast_to(m_new, (1, LANES))
        vs = slice(h * MLSTM_DV, (h + 1) * MLSTM_DV)
        msq = jnp.mean(hout * hout, axis=1, keepdims=True)
        hn = hout * lax.rsqrt(msq + NORM_EPS) * nw_ref[:, vs]
        out_ref[:, vs] = (jax.nn.sigmoid(o_ref[:, vs]) * hn).astype(out_ref.dtype)

    return head


assert MLSTM_CHUNK == SSD_CHUNK


def _mixer_kernel(z_ref, x_ref, b_ref, c_ref, q_ref, k_ref, v_ref, og_ref, sm_ref,
                  convw_ref, convb_ref, smb_ref, alog_ref, dskip_ref, ssd_nw_ref, expand_ref, ml_nw_ref,
                  y_ssd_ref, y_ml_ref, tail_ref, state_ref, cst_ref, nst_ref, mst_ref):
    mlstm_head = _mlstm_kernel(q_ref, k_ref, v_ref, og_ref, sm_ref, smb_ref, ml_nw_ref, y_ml_ref,
                               cst_ref, nst_ref, mst_ref)
    ssd_group = _ssd_kernel(z_ref, x_ref, b_ref, c_ref, sm_ref, convw_ref, convb_ref, smb_ref, alog_ref,
                            dskip_ref, ssd_nw_ref, expand_ref, y_ssd_ref, tail_ref, state_ref)
    for h in range(MLSTM_HEADS):
        mlstm_head(h)
    for g in range(SSD_GROUPS):
        ssd_group(g)


def hybrid_mixer(proj_a, proj_b, proj_s, conv_w, conv_b, small_bias, a_log, d_skip, ssd_norm_w, ml_norm_w):
    s = proj_a.shape[0]
    L = SSD_CHUNK
    conv_dim = conv_w.shape[1]
    qk_w = MLSTM_HEADS * MLSTM_DQK
    alog = jnp.pad(a_log, (0, LANES - SSD_HEADS)).reshape(1, LANES)
    dskip_e = jnp.repeat(d_skip, SSD_HEAD_DIM).reshape(1, SSD_INNER)
    piece_row = jnp.arange(LANES)[:, None]
    expand = ((jnp.arange(SSD_INNER)[None, :] // SSD_HEAD_DIM == piece_row % SSD_HEADS)
              & (piece_row < 3 * SSD_HEADS)).astype(BF16)
    full = lambda shp: pl.BlockSpec(shp, lambda i: (0,) * len(shp))
    blk = lambda w, col: pl.BlockSpec((L, w), lambda i: (i, col // w))
    return pl.pallas_call(
        _mixer_kernel,
        grid=(s // L,),
        in_specs=[blk(SSD_INNER, PA_Z), blk(SSD_INNER, PA_X), blk(512, PA_B), blk(512, PA_C),
                  blk(qk_w, PB_Q), blk(qk_w, PB_K), blk(MLSTM_WIDTH, PB_V), blk(MLSTM_WIDTH, PB_O),
                  blk(LANES, 0),
                  full((SSD_CONV, conv_dim)), full((1, conv_dim)), full((1, LANES)), full((1, LANES)),
                  full((1, SSD_INNER)), full((1, SSD_INNER)), full((LANES, SSD_INNER)),
                  full((1, MLSTM_WIDTH))],
        out_specs=[pl.BlockSpec((L, SSD_INNER), lambda i: (i, 0)),
                   pl.BlockSpec((L, MLSTM_WIDTH), lambda i: (i, 0))],
        out_shape=[jax.ShapeDtypeStruct((s, SSD_INNER), BF16),
                   jax.ShapeDtypeStruct((s, MLSTM_WIDTH), BF16)],
        scratch_shapes=[pltpu.VMEM((SUBLANES, conv_dim), F32),
                        pltpu.VMEM((SSD_GROUPS, SSD_STATE, SSD_GROUP_WIDTH), F32),
                        pltpu.VMEM((MLSTM_HEADS, MLSTM_DQK, MLSTM_DV), F32),
                        pltpu.VMEM((SUBLANES, MLSTM_DQK), F32),
                        pltpu.VMEM((SUBLANES, LANES), F32)],
        compiler_params=_cparams("arbitrary"),
        name="hybrid_mixer",
    )(proj_a, proj_a, proj_a, proj_a, proj_b, proj_b, proj_b, proj_b, proj_s,
      conv_w, conv_b.reshape(1, conv_dim), small_bias, alog, dskip_e,
      ssd_norm_w.reshape(1, SSD_INNER), expand, ml_norm_w.reshape(1, MLSTM_WIDTH))


def _split_w_in(w_in):
    w_a = w_in[:, :PA_WIDTH].astype(BF16)
    w_b = w_in[:, PB_START:PB_START + PB_WIDTH].astype(BF16)
    w_s = jnp.concatenate([w_in[:, IN_DT:IN_DT + SSD_HEADS], w_in[:, IN_I:IN_I + MLSTM_HEADS],
                           w_in[:, IN_F:IN_F + MLSTM_HEADS],
                           jnp.zeros((w_in.shape[0], LANES - SSD_HEADS - 2 * MLSTM_HEADS), w_in.dtype)],
                          axis=1).astype(BF16)
    return w_a, w_b, w_s


def _small_bias(dt_bias, i_bias, f_bias):
    v = jnp.concatenate([dt_bias, i_bias, f_bias, jnp.zeros((LANES - 40,), F32)])
    return v.reshape(1, LANES)


def kernel(x, norm_mix_w, w_in, conv_w, conv_b, dt_bias, a_log, d_skip, ssd_norm_w, mlstm_i_bias,
           mlstm_f_bias, mlstm_norm_w, w_out, norm_ffn_w, ffn_w_gate, ffn_w_up, ffn_w_down, router_w,
           moe_w_gate, moe_w_up, moe_w_down, final_norm_w):
    bsz, s, d = x.shape
    depth = w_in.shape[0]
    outs = []
    for b in range(bsz):
        h = x[b]
        normed = False
        for layer in range(depth):
            w_a, w_b, w_s = _split_w_in(w_in[layer])
            u, proj_s = rmsnorm_small(h, norm_mix_w[layer], w_s)
            proj_a = matmul(u, w_a, F32, tm=1024, tn=PA_TN)
            proj_b = matmul(u, w_b, F32, tm=1024, tn=PB_TN)
            sbias = _small_bias(dt_bias[layer], mlstm_i_bias[layer], mlstm_f_bias[layer])
            y_ssd, y_ml = hybrid_mixer(proj_a, proj_b, proj_s, conv_w[layer], conv_b[layer], sbias,
                                       a_log[layer], d_skip[layer], ssd_norm_w[layer], mlstm_norm_w[layer])
            h = out_proj(y_ssd, y_ml, w_out[layer].astype(BF16), h)
            j = layer // 2
            if layer % 2 == 0:
                u = rmsnorm(h, norm_ffn_w[layer], BF16)
                plan, _ = _up_plan(jnp.full((1,), s // MOE_TM, jnp.int32), s // MOE_TM, FF_DENSE // UP_TF)
                a = grouped_swiglu_up(plan, u, ffn_w_gate[j][None], ffn_w_up[j][None], UP_TF)
                h = down_proj(a, ffn_w_down[j].astype(BF16), h, tk=FF_DENSE // 4)
            else:
                normed = layer == depth - 1
                h = moe_swiglu(h, norm_ffn_w[layer], router_w[j], moe_w_gate[j], moe_w_up[j],
                               moe_w_down[j].astype(BF16), final_norm_w if normed else None)
        outs.append(h if normed else rmsnorm(h, final_norm_w, F32))
    return jnp.stack(outs, axis=0)
```

```python
import functools
from typing import NamedTuple

import jax
import jax.numpy as jnp
from jax import lax
from jax.experimental import pallas as pl
from jax.experimental.pallas import tpu as pltpu

F32 = jnp.float32
BF16 = jnp.bfloat16

D_MODEL = 4096
SSD_INNER = 2048
SSD_HEAD_DIM = 64
SSD_HEADS = 32
SSD_GROUPS = 4
SSD_STATE = 128
SSD_CONV = 4
SSD_CHUNK = 128
SSD_GROUP_WIDTH = SSD_INNER // SSD_GROUPS
MLSTM_WIDTH = 2048
MLSTM_HEADS = 4
MLSTM_DV = 512
MLSTM_DQK = 256
FF_DENSE = 14336
N_EXPERTS = 8
FF_EXPERT = 5632
NORM_EPS = 1e-6

LANES = 128
SUBLANES = 8
VMEM_LIMIT_BYTES = 56 * 1024 * 1024

PA_WIDTH, PA_TN = 5120, 1280
PA_Z, PA_X, PA_B, PA_C = 0, 2048, 4096, 4608
PB_START, PB_WIDTH, PB_TN = 5152, 6144, 1024
PB_Q, PB_K, PB_V, PB_O = 0, 1024, 2048, 4096
IN_DT, IN_I, IN_F = 5120, 11296, 11300
SMALL_DT, SMALL_I, SMALL_F = 0, 32, 36

MLSTM_CHUNK = 128


def _cparams(*sem):
    return pltpu.CompilerParams(dimension_semantics=sem, vmem_limit_bytes=VMEM_LIMIT_BYTES)


def _rmsnorm_kernel(x_ref, w_ref, o_ref):
    x = x_ref[...]
    ms = jnp.mean(x * x, axis=-1, keepdims=True)
    o_ref[...] = (x * lax.rsqrt(ms + NORM_EPS) * w_ref[...]).astype(o_ref.dtype)


def rmsnorm(x, w, out_dtype, tr=512):
    s, d = x.shape
    return pl.pallas_call(
        _rmsnorm_kernel,
        grid=(s // tr,),
        in_specs=[pl.BlockSpec((tr, d), lambda i: (i, 0)),
                  pl.BlockSpec((1, d), lambda i: (0, 0))],
        out_specs=pl.BlockSpec((tr, d), lambda i: (i, 0)),
        out_shape=jax.ShapeDtypeStruct((s, d), out_dtype),
        compiler_params=_cparams("parallel"),
        name="rmsnorm",
    )(x, w.reshape(1, d))


def _rmsnorm_small_kernel(x_ref, w_ref, ws_ref, u_ref, ps_ref):
    x = x_ref[...]
    ms = jnp.mean(x * x, axis=-1, keepdims=True)
    u = (x * lax.rsqrt(ms + NORM_EPS) * w_ref[...]).astype(BF16)
    u_ref[...] = u
    ps_ref[...] = jnp.dot(u, ws_ref[...], preferred_element_type=F32)


def rmsnorm_small(x, w, w_small, tr=512):
    s, d = x.shape
    return pl.pallas_call(
        _rmsnorm_small_kernel,
        grid=(s // tr,),
        in_specs=[pl.BlockSpec((tr, d), lambda i: (i, 0)),
                  pl.BlockSpec((1, d), lambda i: (0, 0)),
                  pl.BlockSpec((d, LANES), lambda i: (0, 0))],
        out_specs=[pl.BlockSpec((tr, d), lambda i: (i, 0)),
                   pl.BlockSpec((tr, LANES), lambda i: (i, 0))],
        out_shape=[jax.ShapeDtypeStruct((s, d), BF16),
                   jax.ShapeDtypeStruct((s, LANES), F32)],
        compiler_params=_cparams("parallel"),
        name="rmsnorm_small",
    )(x, w.reshape(1, d), w_small)


RT_SEL = 0
RT_C1, RT_C2 = 8, 9
RT_I1, RT_I2 = 10, 11


def _router_kernel(x_ref, w_ref, rw_ref, rt_ref):
    x = x_ref[...]
    ms = jnp.mean(x * x, axis=-1, keepdims=True)
    u = x * lax.rsqrt(ms + NORM_EPS) * w_ref[...]
    logits = jnp.dot(u, rw_ref[...], preferred_element_type=F32, precision=lax.Precision.HIGHEST)
    lane = lax.broadcasted_iota(jnp.int32, logits.shape, 1)
    valid = lane < N_EXPERTS
    lg = jnp.where(valid, logits, -jnp.inf)
    ex = jnp.exp(lg - jnp.max(lg, axis=1, keepdims=True))
    probs = ex / jnp.sum(ex, axis=1, keepdims=True)
    probs = jnp.where(valid, probs, -1.0)
    p1 = jnp.max(probs, axis=1, keepdims=True)
    i1 = jnp.min(jnp.where(probs == p1, lane, LANES), axis=1, keepdims=True)
    rest = jnp.where(lane == i1, -1.0, probs)
    p2 = jnp.max(rest, axis=1, keepdims=True)
    i2 = jnp.min(jnp.where(rest == p2, lane, LANES), axis=1, keepdims=True)
    tot = p1 + p2
    rt = jnp.where((lane == i1) | (lane == i2), 1.0, 0.0)
    rt = jnp.where(lane == RT_C1, p1 / tot, rt)
    rt = jnp.where(lane == RT_C2, p2 / tot, rt)
    rt = jnp.where(lane == RT_I1, i1.astype(F32), rt)
    rt = jnp.where(lane == RT_I2, i2.astype(F32), rt)
    rt_ref[...] = rt


def router(x, w, router_w, tr=512):
    s, d = x.shape
    rw = jnp.pad(router_w, ((0, 0), (0, LANES - N_EXPERTS)))
    return pl.pallas_call(
        _router_kernel,
        grid=(s // tr,),
        in_specs=[pl.BlockSpec((tr, d), lambda i: (i, 0)),
                  pl.BlockSpec((1, d), lambda i: (0, 0)),
                  pl.BlockSpec((d, LANES), lambda i: (0, 0))],
        out_specs=pl.BlockSpec((tr, LANES), lambda i: (i, 0)),
        out_shape=jax.ShapeDtypeStruct((s, LANES), F32),
        compiler_params=_cparams("parallel"),
        name="router",
    )(x, w.reshape(1, d), rw)


MOE_TM = 512
DENSE_TM = 1024
UP_TF = 512
PLAN_BLK = 512


def _plan_kernel(rt_ref, dest_ref, meta_ref):
    s = rt_ref.shape[0]
    nblk = s // PLAN_BLK
    lane = lax.broadcasted_iota(jnp.int32, (PLAN_BLK, LANES), 1)
    is_exp = lane < N_EXPERTS
    r = lax.broadcasted_iota(jnp.int32, (PLAN_BLK, PLAN_BLK), 0)
    c = lax.broadcasted_iota(jnp.int32, (PLAN_BLK, PLAN_BLK), 1)
    strict_tril = (r > c).astype(BF16)

    def rank_body(b, carry):
        rows = pl.ds(pl.multiple_of(b * PLAN_BLK, PLAN_BLK), PLAN_BLK)
        sel = jnp.where(is_exp, rt_ref[rows, :], 0.0)
        rank = jnp.dot(strict_tril, sel.astype(BF16), preferred_element_type=F32) + carry
        dest_ref[rows, :] = rank.astype(jnp.int32)
        return carry + jnp.sum(sel, axis=0, keepdims=True)

    counts = lax.fori_loop(0, nblk, rank_body, jnp.zeros((1, LANES), F32))
    tiles = jnp.floor((counts + (MOE_TM - 1)) * (1.0 / MOE_TM))
    er = lax.broadcasted_iota(jnp.int32, (LANES, LANES), 0)
    ec = lax.broadcasted_iota(jnp.int32, (LANES, LANES), 1)
    before = (er < ec).astype(F32)
    tile_start = jnp.dot(jnp.broadcast_to(tiles, (SUBLANES, LANES)), before, preferred_element_type=F32,
                         precision=lax.Precision.HIGHEST)[0:1, :]
    row_off = tile_start * MOE_TM
    meta_ref[0:1, :] = counts.astype(jnp.int32)
    meta_ref[1:2, :] = tiles.astype(jnp.int32)
    meta_ref[2:3, :] = tile_start.astype(jnp.int32)
    meta_ref[3:SUBLANES, :] = jnp.zeros((SUBLANES - 3, LANES), jnp.int32)

    def dest_body(b, _):
        rows = pl.ds(pl.multiple_of(b * PLAN_BLK, PLAN_BLK), PLAN_BLK)
        rt = rt_ref[rows, :]
        pos = dest_ref[rows, :].astype(F32) + row_off
        i1 = rt[:, RT_I1:RT_I1 + 1].astype(jnp.int32)
        i2 = rt[:, RT_I2:RT_I2 + 1].astype(jnp.int32)
        d1 = jnp.sum(jnp.where(lane == i1, pos, 0.0), axis=1, keepdims=True)
        d2 = jnp.sum(jnp.where(lane == i2, pos, 0.0), axis=1, keepdims=True)
        dest_ref[rows, :] = jnp.where(lane == 0, d1, jnp.where(lane == 1, d2, 0.0)).astype(jnp.int32)
        return 0

    lax.fori_loop(0, nblk, dest_body, 0)


def moe_plan(rt):
    s = rt.shape[0]
    return pl.pallas_call(
        _plan_kernel,
        out_shape=[jax.ShapeDtypeStruct((s, LANES), jnp.int32),
                   jax.ShapeDtypeStruct((SUBLANES, LANES), jnp.int32)],
        compiler_params=pltpu.CompilerParams(vmem_limit_bytes=VMEM_LIMIT_BYTES),
        name="moe_plan",
    )(rt)


def _invert_kernel(d1_ref, d2_ref, src_ref):
    n_rows = src_ref.shape[0]
    n_tok = d1_ref.shape[0]

    def zero(i, _):
        src_ref[i] = 0
        return 0

    lax.fori_loop(0, n_rows, zero, 0, unroll=16)

    def put(t, _):
        src_ref[d1_ref[t]] = t
        src_ref[d2_ref[t]] = t
        return 0

    lax.fori_loop(0, n_tok, put, 0, unroll=8)


def moe_invert(d1, d2, n_rows):
    smem = pl.BlockSpec(memory_space=pltpu.SMEM)
    return pl.pallas_call(
        _invert_kernel,
        in_specs=[smem, smem],
        out_specs=smem,
        out_shape=jax.ShapeDtypeStruct((n_rows,), jnp.int32),
        name="moe_invert",
    )(d1, d2)


GATHER_TM = 256
GATHER_UNROLL = 8


def _gather_norm_kernel(src_ref, nt_ref, h_hbm, w_ref, o_ref, buf_ref, sem):
    i = pl.program_id(0)
    nt = nt_ref[0]

    def row_copy(tile, slot, r):
        return pltpu.make_async_copy(h_hbm.at[pl.ds(src_ref[tile * MOE_TM + r], 1), :],
                                     buf_ref.at[slot, pl.ds(r, 1), :], sem.at[slot])

    def issue(tile, slot):
        def body(r, _):
            row_copy(tile, slot, r).start()
            return 0
        lax.fori_loop(0, MOE_TM, body, 0, unroll=GATHER_UNROLL)

    @pl.when(i == 0)
    def _():
        issue(0, 0)

    @pl.when(i + 1 < nt)
    def _():
        issue(i + 1, (i + 1) % 2)

    @pl.when(i < nt)
    def _():
        slot = i % 2

        pltpu.make_async_copy(h_hbm.at[pl.ds(0, MOE_TM), :], buf_ref.at[slot], sem.at[slot]).wait()
        x = buf_ref[slot]
        ms = jnp.mean(x * x, axis=-1, keepdims=True)
        o_ref[...] = (x * lax.rsqrt(ms + NORM_EPS) * w_ref[...]).astype(o_ref.dtype)

    @pl.when(i >= nt)
    def _():
        o_ref[...] = jnp.zeros(o_ref.shape, o_ref.dtype)


def moe_gather_norm(src, nt, h, w):
    n_rows = src.shape[0]
    d = h.shape[1]
    return pl.pallas_call(
        _gather_norm_kernel,
        grid_spec=pltpu.PrefetchScalarGridSpec(
            num_scalar_prefetch=2,
            grid=(n_rows // MOE_TM,),
            in_specs=[pl.BlockSpec(memory_space=pl.ANY),
                      pl.BlockSpec((1, d), lambda i, src, nt: (0, 0))],
            out_specs=pl.BlockSpec((MOE_TM, d), lambda i, src, nt: (i, 0)),
            scratch_shapes=[pltpu.VMEM((2, MOE_TM, d), F32), pltpu.SemaphoreType.DMA((2,))]),
        out_shape=jax.ShapeDtypeStruct((n_rows, d), BF16),
        compiler_params=_cparams("arbitrary"),
        name="moe_gather_norm",
    )(src, nt, h, w.reshape(1, d))


def _combine_kernel(d1_ref, d2_ref, h_ref, rt_ref, pw_ref, y_hbm, o_ref, ya_ref, yb_ref, sem, *, post_norm):
    i = pl.program_id(0)
    n = pl.num_programs(0)

    def copy_a(blk, slot, r):
        return pltpu.make_async_copy(y_hbm.at[pl.ds(d1_ref[blk * GATHER_TM + r], 1), :],
                                     ya_ref.at[slot, pl.ds(r, 1), :], sem.at[0, slot])

    def copy_b(blk, slot, r):
        return pltpu.make_async_copy(y_hbm.at[pl.ds(d2_ref[blk * GATHER_TM + r], 1), :],
                                     yb_ref.at[slot, pl.ds(r, 1), :], sem.at[1, slot])

    def issue(blk, slot):
        def body(r, _):
            copy_a(blk, slot, r).start()
            copy_b(blk, slot, r).start()
            return 0
        lax.fori_loop(0, GATHER_TM, body, 0, unroll=GATHER_UNROLL)

    @pl.when(i == 0)
    def _():
        issue(0, 0)

    @pl.when(i + 1 < n)
    def _():
        issue(i + 1, (i + 1) % 2)

    slot = i % 2

    pltpu.make_async_copy(y_hbm.at[pl.ds(0, GATHER_TM), :], ya_ref.at[slot], sem.at[0, slot]).wait()
    pltpu.make_async_copy(y_hbm.at[pl.ds(0, GATHER_TM), :], yb_ref.at[slot], sem.at[1, slot]).wait()
    rt = rt_ref[...]
    out = h_ref[...] + rt[:, RT_C1:RT_C1 + 1] * ya_ref[slot] + rt[:, RT_C2:RT_C2 + 1] * yb_ref[slot]
    if post_norm:
        ms = jnp.mean(out * out, axis=-1, keepdims=True)
        out = out * lax.rsqrt(ms + NORM_EPS) * pw_ref[...]
    o_ref[...] = out


def moe_combine(d1, d2, h, rt, y, post_norm_w):
    s, d = h.shape
    post_norm = post_norm_w is not None
    pw = (post_norm_w if post_norm else jnp.ones((d,), F32)).reshape(1, d)
    return pl.pallas_call(
        functools.partial(_combine_kernel, post_norm=post_norm),
        grid_spec=pltpu.PrefetchScalarGridSpec(
            num_scalar_prefetch=2,
            grid=(s // GATHER_TM,),
            in_specs=[pl.BlockSpec((GATHER_TM, d), lambda i, a, b: (i, 0)),
                      pl.BlockSpec((GATHER_TM, LANES), lambda i, a, b: (i, 0)),
                      pl.BlockSpec((1, d), lambda i, a, b: (0, 0)),
                      pl.BlockSpec(memory_space=pl.ANY)],
            out_specs=pl.BlockSpec((GATHER_TM, d), lambda i, a, b: (i, 0)),
            scratch_shapes=[pltpu.VMEM((2, GATHER_TM, d), F32), pltpu.VMEM((2, GATHER_TM, d), F32),
                            pltpu.SemaphoreType.DMA((2, 2))]),
        out_shape=jax.ShapeDtypeStruct((s, d), F32),
        compiler_params=_cparams("arbitrary"),
        name="moe_combine",
    )(d1, d2, h, rt, pw, y)


def _mm_kernel(x_ref, w_ref, o_ref):
    o_ref[...] = jnp.dot(x_ref[...], w_ref[...], preferred_element_type=F32).astype(o_ref.dtype)


def matmul(x, w, out_dtype, tm, tn):
    m, k = x.shape
    n = w.shape[1]
    return pl.pallas_call(
        _mm_kernel,
        grid=(m // tm, n // tn),
        in_specs=[pl.BlockSpec((tm, k), lambda i, j: (i, 0)),
                  pl.BlockSpec((k, tn), lambda i, j: (0, j))],
        out_specs=pl.BlockSpec((tm, tn), lambda i, j: (i, j)),
        out_shape=jax.ShapeDtypeStruct((m, n), out_dtype),
        compiler_params=_cparams("parallel", "arbitrary"),
        name="in_proj",
    )(x, w)


def _out_proj_kernel(ya_ref, yb_ref, wa_ref, wb_ref, r_ref, o_ref):
    acc = jnp.dot(ya_ref[...], wa_ref[...], preferred_element_type=F32)
    acc += jnp.dot(yb_ref[...], wb_ref[...], preferred_element_type=F32)
    o_ref[...] = r_ref[...] + acc


def out_proj(y_ssd, y_ml, w_out, resid, tm=1024, tn=1024):
    m, ka = y_ssd.shape
    n = w_out.shape[1]
    return pl.pallas_call(
        _out_proj_kernel,
        grid=(m // tm, n // tn),
        in_specs=[pl.BlockSpec((tm, ka), lambda i, j: (i, 0)),
                  pl.BlockSpec((tm, ka), lambda i, j: (i, 0)),
                  pl.BlockSpec((ka, tn), lambda i, j: (0, j)),
                  pl.BlockSpec((ka, tn), lambda i, j: (1, j)),
                  pl.BlockSpec((tm, tn), lambda i, j: (i, j))],
        out_specs=pl.BlockSpec((tm, tn), lambda i, j: (i, j)),
        out_shape=jax.ShapeDtypeStruct((m, n), F32),
        compiler_params=_cparams("parallel", "arbitrary"),
        name="out_proj",
    )(y_ssd, y_ml, w_out, w_out, resid)


def _silu(x):
    return x * jax.nn.sigmoid(x)


class UpPlan(NamedTuple):
    counts: jax.Array
    run_of_tile: jax.Array
    first_of_run: jax.Array
    phase_expert: jax.Array
    phase_hidden: jax.Array


def _up_plan(tiles, n_tiles_max, n_hidden_tiles):
    ne = tiles.shape[0]
    tile_end = jnp.cumsum(tiles)
    nt = tile_end[-1]
    tid = jnp.minimum(jnp.arange(n_tiles_max, dtype=jnp.int32), nt - 1)
    eot = jnp.sum((tid[:, None] >= tile_end[None, :]).astype(jnp.int32), axis=1)
    has = tiles > 0
    n_runs = jnp.sum(has.astype(jnp.int32))
    run_of_expert = jnp.cumsum(has.astype(jnp.int32)) - 1
    run_expert = jnp.argsort(jnp.logical_not(has), stable=True).astype(jnp.int32)
    k = jnp.arange(ne * n_hidden_tiles, dtype=jnp.int32)
    plan = UpPlan(counts=jnp.stack([nt, n_runs, n_runs * n_hidden_tiles]).astype(jnp.int32),
                  run_of_tile=run_of_expert[eot],
                  first_of_run=(tid == (tile_end - tiles)[eot]).astype(jnp.int32),
                  phase_expert=run_expert[k % n_runs],
                  phase_hidden=jnp.minimum(k // n_runs, n_hidden_tiles - 1))
    return plan, eot


def _grouped_up_kernel(cnt_ref, run_ref, first_ref, pe_ref, ph_ref, x_ref, wg_hbm, wu_hbm, o_ref,
                       wf_ref, wb_ref, sem, *, tf):
    j, t = pl.program_id(0), pl.program_id(1)
    nt, n_runs, n_phases = cnt_ref[0], cnt_ref[1], cnt_ref[2]

    def fetch(p):
        cols = pl.ds(pl.multiple_of(ph_ref[p] * tf, tf), tf)
        return (pltpu.make_async_copy(wg_hbm.at[pe_ref[p], :, cols], wf_ref.at[0], sem.at[0]),
                pltpu.make_async_copy(wu_hbm.at[pe_ref[p], :, cols], wf_ref.at[1], sem.at[1]))

    def start(p):
        for c in fetch(p):
            c.start()

    active = t < nt
    p = j * n_runs + run_ref[t]

    @pl.when(active & (j == 0) & (t == 0))
    def _():
        start(0)

    @pl.when(active & (first_ref[t] == 1))
    def _():
        for c in fetch(p):
            c.wait()
        wb_ref[0] = wf_ref[0].astype(BF16)
        wb_ref[1] = wf_ref[1].astype(BF16)

        @pl.when(p + 1 < n_phases)
        def _():
            start(p + 1)

    @pl.when(active)
    def _():
        x = x_ref[...]
        g = jnp.dot(x, wb_ref[0], preferred_element_type=F32)
        u = jnp.dot(x, wb_ref[1], preferred_element_type=F32)
        o_ref[...] = (_silu(g) * u).astype(o_ref.dtype)

    @pl.when(jnp.logical_not(active))
    def _():
        o_ref[...] = jnp.zeros(o_ref.shape, o_ref.dtype)


def grouped_swiglu_up(plan, xs, wg, wu, tm, tf):
    r, k = xs.shape
    f = wg.shape[2]
    row = lambda j, t, cnt, *_: (jnp.maximum(jnp.minimum(t, cnt[0] - 1), 0), 0)
    return pl.pallas_call(
        functools.partial(_grouped_up_kernel, tf=tf),
        grid_spec=pltpu.PrefetchScalarGridSpec(
            num_scalar_prefetch=5,
            grid=(f // tf, r // tm),
            in_specs=[pl.BlockSpec((tm, k), row),
                      pl.BlockSpec(memory_space=pl.ANY),
                      pl.BlockSpec(memory_space=pl.ANY)],
            out_specs=pl.BlockSpec((tm, tf), lambda j, t, *_: (t, j)),
            scratch_shapes=[pltpu.VMEM((2, k, tf), F32), pltpu.VMEM((2, k, tf), BF16),
                            pltpu.SemaphoreType.DMA((2,))]),
        out_shape=jax.ShapeDtypeStruct((r, f), BF16),
        compiler_params=_cparams("arbitrary", "arbitrary"),
        name="grouped_swiglu_up",
    )(*plan, xs, wg, wu)


def _moe_down_kernel(eot_ref, nt_ref, a_ref, w_ref, o_ref):
    @pl.when(pl.program_id(1) < nt_ref[0])
    def _():
        o_ref[...] = jnp.dot(a_ref[...], w_ref[...], preferred_element_type=F32)

    @pl.when(pl.program_id(1) >= nt_ref[0])
    def _():
        o_ref[...] = jnp.zeros(o_ref.shape, o_ref.dtype)


def moe_down(eot, nt, a, wd, tn=1024):
    r, f = a.shape
    n = wd.shape[2]
    return pl.pallas_call(
        _moe_down_kernel,
        grid_spec=pltpu.PrefetchScalarGridSpec(
            num_scalar_prefetch=2,
            grid=(n // tn, r // MOE_TM),
            in_specs=[pl.BlockSpec((MOE_TM, f), lambda j, t, eot, nt: (jnp.maximum(jnp.minimum(t, nt[0] - 1), 0), 0)),
                      pl.BlockSpec((None, f, tn), lambda j, t, eot, nt: (eot[t], 0, j))],
            out_specs=pl.BlockSpec((MOE_TM, tn), lambda j, t, eot, nt: (t, j))),
        out_shape=jax.ShapeDtypeStruct((r, n), F32),
        compiler_params=_cparams("arbitrary", "arbitrary"),
        name="moe_down",
    )(eot, nt, a, wd)


def moe_swiglu(h, norm_w, router_w, wg, wu, wd, post_norm_w=None):
    s, d = h.shape
    n_tiles_max = (2 * s) // MOE_TM + N_EXPERTS
    n_rows = n_tiles_max * MOE_TM
    rt = router(h, norm_w, router_w)
    dest, meta = moe_plan(rt)
    d1, d2 = dest[:, 0], dest[:, 1]
    plan, eot = _up_plan(meta[1, :N_EXPERTS], n_tiles_max, wg.shape[2] // UP_TF)
    nt = plan.counts[0:1]
    src = moe_invert(d1, d2, n_rows)
    xs = moe_gather_norm(src, nt, h, norm_w)
    a = grouped_swiglu_up(plan, xs, wg, wu, MOE_TM, UP_TF)
    y = moe_down(eot, nt, a, wd)
    return moe_combine(d1, d2, h, rt, y, post_norm_w)


def _down_kernel(a_ref, w_ref, r_ref, o_ref):
    kk = pl.program_id(2)

    @pl.when(kk == 0)
    def _():
        o_ref[...] = r_ref[...]

    o_ref[...] += jnp.dot(a_ref[...], w_ref[...], preferred_element_type=F32)


def down_proj(a, w, resid, tk, tm=1024, tn=1024):
    m, k = a.shape
    n = w.shape[1]
    return pl.pallas_call(
        _down_kernel,
        grid=(m // tm, n // tn, k // tk),
        in_specs=[pl.BlockSpec((tm, tk), lambda i, j, kk: (i, kk)),
                  pl.BlockSpec((tk, tn), lambda i, j, kk: (kk, j)),
                  pl.BlockSpec((tm, tn), lambda i, j, kk: (i, j))],
        out_specs=pl.BlockSpec((tm, tn), lambda i, j, kk: (i, j)),
        out_shape=jax.ShapeDtypeStruct((m, n), F32),
        compiler_params=_cparams("parallel", "arbitrary", "arbitrary"),
        name="down_proj",
    )(a, w, resid)


def _softplus(x):
    return jnp.maximum(x, 0.0) + jnp.log1p(jnp.exp(-jnp.abs(x)))


def _tril_ones(n):
    r = lax.broadcasted_iota(jnp.int32, (n, n), 0)
    c = lax.broadcasted_iota(jnp.int32, (n, n), 1)
    return r >= c


def _ssd_kernel(z_ref, x_ref, b_ref, c_ref, sm_ref, convw_ref, convb_ref, smb_ref, alog_ref,
                dskip_ref, nw_ref, expand_ref, o_ref, tail_ref, state_ref):
    L = SSD_CHUNK
    HI = lax.Precision.HIGHEST

    @pl.when(pl.program_id(0) == 0)
    def _():
        tail_ref[...] = jnp.zeros(tail_ref.shape, F32)
        state_ref[...] = jnp.zeros(state_ref.shape, F32)

    row8 = lax.broadcasted_iota(jnp.int32, (SUBLANES, LANES), 0)

    def conv_silu(u, col0):
        w = u.shape[1]
        cols = slice(col0, col0 + w)
        tail = tail_ref[:, cols]
        acc = convb_ref[:, cols] + u * convw_ref[SSD_CONV - 1:SSD_CONV, cols]
        for k in range(1, SSD_CONV):
            rolled = pltpu.roll(u, k, axis=0)
            fix = jnp.where(jnp.tile(row8, (1, w // LANES)) < k, pltpu.roll(tail, k, axis=0), rolled[0:SUBLANES])
            shifted = jnp.concatenate([fix, rolled[SUBLANES:]], axis=0)
            acc = acc + shifted * convw_ref[SSD_CONV - 1 - k:SSD_CONV - k, cols]
        tail_ref[:, cols] = u[L - SUBLANES:L]
        return _silu(acc)

    xs = conv_silu(x_ref[...], 0)
    bmat = conv_silu(b_ref[...], SSD_INNER)
    cmat = conv_silu(c_ref[...], SSD_INNER + SSD_GROUPS * SSD_STATE)

    lane = lax.broadcasted_iota(jnp.int32, (L, LANES), 1)

    def expand_heads(v):
        hi = v.astype(BF16).astype(F32)
        r1 = v - hi
        mid = r1.astype(BF16).astype(F32)
        lo = r1 - mid
        packed = jnp.where(lane < 32, hi,
                           jnp.where(lane < 64, pltpu.roll(mid, 32, axis=1),
                                     jnp.where(lane < 96, pltpu.roll(lo, 64, axis=1), 0.0)))
        return jnp.dot(packed.astype(BF16), expand_ref[...], preferred_element_type=F32)

    causal = _tril_ones(L)
    dt = jnp.where(lane < SSD_HEADS, _softplus(sm_ref[...] + smb_ref[...]), 0.0)
    da = dt * (-jnp.exp(alog_ref[...]))
    acs = jnp.dot(causal.astype(F32), da, preferred_element_type=F32, precision=HI)
    acs_t = acs.T
    a_last = acs[L - 1:L, :]
    dt_e = expand_heads(dt)
    eacs_e = expand_heads(jnp.exp(acs))
    dst_e = expand_heads(jnp.exp(a_last - acs))
    xdt = xs * dt_e
    xdec = (xdt * dst_e).astype(BF16)
    xdt_b = xdt.astype(BF16)
    chunk_decay_e = eacs_e[L - 1:L, :]

    y = xs * dskip_ref[...]
    zgate = _silu(z_ref[...])
    lo_half = lane < SSD_HEAD_DIM

    def group(g):
        gs = slice(g * SSD_GROUP_WIDTH, (g + 1) * SSD_GROUP_WIDTH)
        bg = bmat[:, g * SSD_STATE:(g + 1) * SSD_STATE]
        cg = cmat[:, g * SSD_STATE:(g + 1) * SSD_STATE]
        cg_b = cg.astype(BF16)
        cb = lax.dot_general(cg_b, bg.astype(BF16), (((1,), (1,)), ((), ())),
                             preferred_element_type=F32)
        h_enter = state_ref[g]
        y_off = jnp.dot(cg_b, h_enter.astype(BF16), preferred_element_type=F32)
        st_new = jnp.dot(bg.T.astype(BF16), xdec[:, gs], preferred_element_type=F32)
        state_ref[g] = chunk_decay_e[:, gs] * h_enter + st_new
        yd_parts = []
        for p in range(4):
            j0 = g * 8 + 2 * p
            ms = []
            for j in (j0, j0 + 1):
                seg = jnp.where(causal, acs[:, j:j + 1] - acs_t[j:j + 1, :], -jnp.inf)
                ms.append((cb * jnp.exp(seg)).astype(BF16))
            xp = xdt_b[:, g * SSD_GROUP_WIDTH + p * LANES:g * SSD_GROUP_WIDTH + (p + 1) * LANES]
            zero = jnp.zeros_like(xp)
            rhs = jnp.concatenate([jnp.where(lo_half, xp, zero), jnp.where(lo_half, zero, xp)], axis=0)
            yd_parts.append(jnp.dot(jnp.concatenate(ms, axis=1), rhs, preferred_element_type=F32))
        y_diag = jnp.concatenate(yd_parts, axis=1)
        yg = (y[:, gs] + y_diag + y_off * eacs_e[:, gs]) * zgate[:, gs]
        msq = jnp.mean(yg * yg, axis=1, keepdims=True)
        o_ref[:, gs] = (yg * lax.rsqrt(msq + NORM_EPS) * nw_ref[:, gs]).astype(o_ref.dtype)

    return group


def _log_sigmoid(x):
    return jnp.minimum(x, 0.0) - jnp.log1p(jnp.exp(-jnp.abs(x)))


def _mlstm_kernel(q_ref, k_ref, v_ref, o_ref, sm_ref, smb_ref, nw_ref, out_ref, c_ref, n_ref, m_ref):
    L = MLSTM_CHUNK
    HI = lax.Precision.HIGHEST

    @pl.when(pl.program_id(0) == 0)
    def _():
        c_ref[...] = jnp.zeros(c_ref.shape, F32)
        n_ref[...] = jnp.zeros(n_ref.shape, F32)
        m_ref[...] = jnp.zeros(m_ref.shape, F32)

    causal = _tril_ones(L)
    sm = sm_ref[...] + smb_ref[...]
    bcs = jnp.dot(causal.astype(F32), _log_sigmoid(sm), preferred_element_type=F32, precision=HI)
    sm_t = sm.T
    bcs_t = bcs.T

    def head(h):
        li_col = sm[:, SMALL_I + h:SMALL_I + h + 1]
        b_col = bcs[:, SMALL_F + h:SMALL_F + h + 1]
        r_row = sm_t[SMALL_I + h:SMALL_I + h + 1, :] - bcs_t[SMALL_F + h:SMALL_F + h + 1, :]
        m_prev = m_ref[h:h + 1, 0:1]
        inter = b_col + m_prev
        dm = jnp.where(causal, b_col + r_row, -jnp.inf)
        m_t = jnp.maximum(inter, jnp.max(dm, axis=1, keepdims=True))
        w_intra = jnp.exp(dm - m_t)
        w_inter = jnp.exp(inter - m_t)
        qf = q_ref[:, h * MLSTM_DQK:(h + 1) * MLSTM_DQK] * (MLSTM_DQK ** -0.5)
        kf = k_ref[:, h * MLSTM_DQK:(h + 1) * MLSTM_DQK]
        qb = qf.astype(BF16)
        vb = v_ref[:, h * MLSTM_DV:(h + 1) * MLSTM_DV].astype(BF16)
        sqk = lax.dot_general(qb, kf.astype(BF16), (((1,), (1,)), ((), ())),
                              preferred_element_type=F32) * w_intra
        c_st = c_ref[h]
        n_row = n_ref[h:h + 1, :]
        num = (w_inter * jnp.dot(qb, c_st.astype(BF16), preferred_element_type=F32)
               + jnp.dot(sqk.astype(BF16), vb, preferred_element_type=F32))
        qn = jnp.sum(qf * n_row, axis=1, keepdims=True)
        den = w_inter * qn + jnp.sum(sqk, axis=1, keepdims=True)
        hout = num / jnp.maximum(jnp.abs(den), jnp.exp(-m_t))
        b_last = b_col[L - 1:L, :]
        g_col = b_last - b_col + li_col
        m_new = jnp.maximum(b_last + m_prev, jnp.max(g_col, axis=0, keepdims=True))
        decay = jnp.exp(b_last + m_prev - m_new)
        kw = kf * jnp.exp(g_col - m_new)
        c_ref[h] = decay * c_st + jnp.dot(kw.T.astype(BF16), vb, preferred_element_type=F32)
        n_ref[h:h + 1, :] = decay * n_row + jnp.sum(kw, axis=0, keepdims=True)
        m_ref[h:h + 1, :] = jnp.broadcast_to(m_new, (1, LANES))
        vs = slice(h * MLSTM_DV, (h + 1) * MLSTM_DV)
        msq = jnp.mean(hout * hout, axis=1, keepdims=True)
        hn = hout * lax.rsqrt(msq + NORM_EPS) * nw_ref[:, vs]
        out_ref[:, vs] = (jax.nn.sigmoid(o_ref[:, vs]) * hn).astype(out_ref.dtype)

    return head


assert MLSTM_CHUNK == SSD_CHUNK


def _mixer_kernel(z_ref, x_ref, b_ref, c_ref, q_ref, k_ref, v_ref, og_ref, sm_ref,
                  convw_ref, convb_ref, smb_ref, alog_ref, dskip_ref, ssd_nw_ref, expand_ref, ml_nw_ref,
                  y_ssd_ref, y_ml_ref, tail_ref, state_ref, cst_ref, nst_ref, mst_ref):
    mlstm_head = _mlstm_kernel(q_ref, k_ref, v_ref, og_ref, sm_ref, smb_ref, ml_nw_ref, y_ml_ref,
                               cst_ref, nst_ref, mst_ref)
    ssd_group = _ssd_kernel(z_ref, x_ref, b_ref, c_ref, sm_ref, convw_ref, convb_ref, smb_ref, alog_ref,
                            dskip_ref, ssd_nw_ref, expand_ref, y_ssd_ref, tail_ref, state_ref)
    for h in range(MLSTM_HEADS):
        mlstm_head(h)
    for g in range(SSD_GROUPS):
        ssd_group(g)


def hybrid_mixer(proj_a, proj_b, proj_s, conv_w, conv_b, small_bias, a_log, d_skip, ssd_norm_w, ml_norm_w):
    s = proj_a.shape[0]
    L = SSD_CHUNK
    conv_dim = conv_w.shape[1]
    qk_w = MLSTM_HEADS * MLSTM_DQK
    alog = jnp.pad(a_log, (0, LANES - SSD_HEADS)).reshape(1, LANES)
    dskip_e = jnp.repeat(d_skip, SSD_HEAD_DIM).reshape(1, SSD_INNER)
    piece_row = jnp.arange(LANES)[:, None]
    expand = ((jnp.arange(SSD_INNER)[None, :] // SSD_HEAD_DIM == piece_row % SSD_HEADS)
              & (piece_row < 3 * SSD_HEADS)).astype(BF16)
    full = lambda shp: pl.BlockSpec(shp, lambda i: (0,) * len(shp))
    blk = lambda w, col: pl.BlockSpec((L, w), lambda i: (i, col // w))
    return pl.pallas_call(
        _mixer_kernel,
        grid=(s // L,),
        in_specs=[blk(SSD_INNER, PA_Z), blk(SSD_INNER, PA_X), blk(512, PA_B), blk(512, PA_C),
                  blk(qk_w, PB_Q), blk(qk_w, PB_K), blk(MLSTM_WIDTH, PB_V), blk(MLSTM_WIDTH, PB_O),
                  blk(LANES, 0),
                  full((SSD_CONV, conv_dim)), full((1, conv_dim)), full((1, LANES)), full((1, LANES)),
                  full((1, SSD_INNER)), full((1, SSD_INNER)), full((LANES, SSD_INNER)),
                  full((1, MLSTM_WIDTH))],
        out_specs=[pl.BlockSpec((L, SSD_INNER), lambda i: (i, 0)),
                   pl.BlockSpec((L, MLSTM_WIDTH), lambda i: (i, 0))],
        out_shape=[jax.ShapeDtypeStruct((s, SSD_INNER), BF16),
                   jax.ShapeDtypeStruct((s, MLSTM_WIDTH), BF16)],
        scratch_shapes=[pltpu.VMEM((SUBLANES, conv_dim), F32),
                        pltpu.VMEM((SSD_GROUPS, SSD_STATE, SSD_GROUP_WIDTH), F32),
                        pltpu.VMEM((MLSTM_HEADS, MLSTM_DQK, MLSTM_DV), F32),
                        pltpu.VMEM((SUBLANES, MLSTM_DQK), F32),
                        pltpu.VMEM((SUBLANES, LANES), F32)],
        compiler_params=_cparams("arbitrary"),
        name="hybrid_mixer",
    )(proj_a, proj_a, proj_a, proj_a, proj_b, proj_b, proj_b, proj_b, proj_s,
      conv_w, conv_b.reshape(1, conv_dim), small_bias, alog, dskip_e,
      ssd_norm_w.reshape(1, SSD_INNER), expand, ml_norm_w.reshape(1, MLSTM_WIDTH))


def _split_w_in(w_in):
    w_a = w_in[:, :PA_WIDTH].astype(BF16)
    w_b = w_in[:, PB_START:PB_START + PB_WIDTH].astype(BF16)
    w_s = jnp.concatenate([w_in[:, IN_DT:IN_DT + SSD_HEADS], w_in[:, IN_I:IN_I + MLSTM_HEADS],
                           w_in[:, IN_F:IN_F + MLSTM_HEADS],
                           jnp.zeros((w_in.shape[0], LANES - SSD_HEADS - 2 * MLSTM_HEADS), w_in.dtype)],
                          axis=1).astype(BF16)
    return w_a, w_b, w_s


def _small_bias(dt_bias, i_bias, f_bias):
    v = jnp.concatenate([dt_bias, i_bias, f_bias, jnp.zeros((LANES - 40,), F32)])
    return v.reshape(1, LANES)


def kernel(x, norm_mix_w, w_in, conv_w, conv_b, dt_bias, a_log, d_skip, ssd_norm_w, mlstm_i_bias,
           mlstm_f_bias, mlstm_norm_w, w_out, norm_ffn_w, ffn_w_gate, ffn_w_up, ffn_w_down, router_w,
           moe_w_gate, moe_w_up, moe_w_down, final_norm_w):
    bsz, s, d = x.shape
    depth = w_in.shape[0]
    outs = []
    for b in range(bsz):
        h = x[b]
        normed = False
        for layer in range(depth):
            w_a, w_b, w_s = _split_w_in(w_in[layer])
            u, proj_s = rmsnorm_small(h, norm_mix_w[layer], w_s)
            proj_a = matmul(u, w_a, F32, tm=1024, tn=PA_TN)
            proj_b = matmul(u, w_b, F32, tm=1024, tn=PB_TN)
            sbias = _small_bias(dt_bias[layer], mlstm_i_bias[layer], mlstm_f_bias[layer])
            y_ssd, y_ml = hybrid_mixer(proj_a, proj_b, proj_s, conv_w[layer], conv_b[layer], sbias,
                                       a_log[layer], d_skip[layer], ssd_norm_w[layer], mlstm_norm_w[layer])
            h = out_proj(y_ssd, y_ml, w_out[layer].astype(BF16), h)
            j = layer // 2
            if layer % 2 == 0:
                u = rmsnorm(h, norm_ffn_w[layer], BF16)
                plan, _ = _up_plan(jnp.full((1,), s // DENSE_TM, jnp.int32), s // DENSE_TM, FF_DENSE // UP_TF)
                a = grouped_swiglu_up(plan, u, ffn_w_gate[j][None], ffn_w_up[j][None], DENSE_TM, UP_TF)
                h = down_proj(a, ffn_w_down[j].astype(BF16), h, tk=FF_DENSE // 4)
            else:
                normed = layer == depth - 1
                h = moe_swiglu(h, norm_ffn_w[layer], router_w[j], moe_w_gate[j], moe_w_up[j],
                               moe_w_down[j].astype(BF16), final_norm_w if normed else None)
        outs.append(h if normed else rmsnorm(h, final_norm_w, F32))
    return jnp.stack(outs, axis=0)
```

```python
import functools
from typing import NamedTuple

import jax
import jax.numpy as jnp
from jax import lax
from jax.experimental import pallas as pl
from jax.experimental.pallas import tpu as pltpu

F32 = jnp.float32
BF16 = jnp.bfloat16

D_MODEL = 4096
SSD_INNER = 2048
SSD_HEAD_DIM = 64
SSD_HEADS = 32
SSD_GROUPS = 4
SSD_STATE = 128
SSD_CONV = 4
SSD_CHUNK = 128
SSD_GROUP_WIDTH = SSD_INNER // SSD_GROUPS
MLSTM_WIDTH = 2048
MLSTM_HEADS = 4
MLSTM_DV = 512
MLSTM_DQK = 256
FF_DENSE = 14336
N_EXPERTS = 8
FF_EXPERT = 5632
NORM_EPS = 1e-6

LANES = 128
SUBLANES = 8
VMEM_LIMIT_BYTES = 56 * 1024 * 1024

PA_WIDTH, PA_TN = 5120, 1280
PA_Z, PA_X, PA_B, PA_C = 0, 2048, 4096, 4608
PB_START, PB_WIDTH, PB_TN = 5152, 6144, 1024
PB_Q, PB_K, PB_V, PB_O = 0, 1024, 2048, 4096
IN_DT, IN_I, IN_F = 5120, 11296, 11300
SMALL_DT, SMALL_I, SMALL_F = 0, 32, 36

MLSTM_CHUNK = 128


def _cparams(*sem):
    return pltpu.CompilerParams(dimension_semantics=sem, vmem_limit_bytes=VMEM_LIMIT_BYTES)


def _rmsnorm_kernel(x_ref, w_ref, o_ref):
    x = x_ref[...]
    ms = jnp.mean(x * x, axis=-1, keepdims=True)
    o_ref[...] = (x * lax.rsqrt(ms + NORM_EPS) * w_ref[...]).astype(o_ref.dtype)


def rmsnorm(x, w, out_dtype, tr=512):
    s, d = x.shape
    return pl.pallas_call(
        _rmsnorm_kernel,
        grid=(s // tr,),
        in_specs=[pl.BlockSpec((tr, d), lambda i: (i, 0)),
                  pl.BlockSpec((1, d), lambda i: (0, 0))],
        out_specs=pl.BlockSpec((tr, d), lambda i: (i, 0)),
        out_shape=jax.ShapeDtypeStruct((s, d), out_dtype),
        compiler_params=_cparams("parallel"),
        name="rmsnorm",
    )(x, w.reshape(1, d))


def _rmsnorm_small_kernel(x_ref, w_ref, ws_ref, u_ref, ps_ref):
    x = x_ref[...]
    ms = jnp.mean(x * x, axis=-1, keepdims=True)
    u = (x * lax.rsqrt(ms + NORM_EPS) * w_ref[...]).astype(BF16)
    u_ref[...] = u
    ps_ref[...] = jnp.dot(u, ws_ref[...], preferred_element_type=F32)


def rmsnorm_small(x, w, w_small, tr=512):
    s, d = x.shape
    return pl.pallas_call(
        _rmsnorm_small_kernel,
        grid=(s // tr,),
        in_specs=[pl.BlockSpec((tr, d), lambda i: (i, 0)),
                  pl.BlockSpec((1, d), lambda i: (0, 0)),
                  pl.BlockSpec((d, LANES), lambda i: (0, 0))],
        out_specs=[pl.BlockSpec((tr, d), lambda i: (i, 0)),
                   pl.BlockSpec((tr, LANES), lambda i: (i, 0))],
        out_shape=[jax.ShapeDtypeStruct((s, d), BF16),
                   jax.ShapeDtypeStruct((s, LANES), F32)],
        compiler_params=_cparams("parallel"),
        name="rmsnorm_small",
    )(x, w.reshape(1, d), w_small)


RT_SEL = 0
RT_C1, RT_C2 = 8, 9
RT_I1, RT_I2 = 10, 11


def _router_kernel(x_ref, w_ref, rw_ref, rt_ref):
    x = x_ref[...]
    ms = jnp.mean(x * x, axis=-1, keepdims=True)
    u = x * lax.rsqrt(ms + NORM_EPS) * w_ref[...]
    logits = jnp.dot(u, rw_ref[...], preferred_element_type=F32, precision=lax.Precision.HIGHEST)
    lane = lax.broadcasted_iota(jnp.int32, logits.shape, 1)
    valid = lane < N_EXPERTS
    lg = jnp.where(valid, logits, -jnp.inf)
    ex = jnp.exp(lg - jnp.max(lg, axis=1, keepdims=True))
    probs = ex / jnp.sum(ex, axis=1, keepdims=True)
    probs = jnp.where(valid, probs, -1.0)
    p1 = jnp.max(probs, axis=1, keepdims=True)
    i1 = jnp.min(jnp.where(probs == p1, lane, LANES), axis=1, keepdims=True)
    rest = jnp.where(lane == i1, -1.0, probs)
    p2 = jnp.max(rest, axis=1, keepdims=True)
    i2 = jnp.min(jnp.where(rest == p2, lane, LANES), axis=1, keepdims=True)
    tot = p1 + p2
    rt = jnp.where((lane == i1) | (lane == i2), 1.0, 0.0)
    rt = jnp.where(lane == RT_C1, p1 / tot, rt)
    rt = jnp.where(lane == RT_C2, p2 / tot, rt)
    rt = jnp.where(lane == RT_I1, i1.astype(F32), rt)
    rt = jnp.where(lane == RT_I2, i2.astype(F32), rt)
    rt_ref[...] = rt


def router(x, w, router_w, tr=512):
    s, d = x.shape
    rw = jnp.pad(router_w, ((0, 0), (0, LANES - N_EXPERTS)))
    return pl.pallas_call(
        _router_kernel,
        grid=(s // tr,),
        in_specs=[pl.BlockSpec((tr, d), lambda i: (i, 0)),
                  pl.BlockSpec((1, d), lambda i: (0, 0)),
                  pl.BlockSpec((d, LANES), lambda i: (0, 0))],
        out_specs=pl.BlockSpec((tr, LANES), lambda i: (i, 0)),
        out_shape=jax.ShapeDtypeStruct((s, LANES), F32),
        compiler_params=_cparams("parallel"),
        name="router",
    )(x, w.reshape(1, d), rw)


MOE_TM = 512
DENSE_TM = 1024
UP_TF = 512
DOWN_TN = 1024
PLAN_BLK = 512


def _plan_kernel(rt_ref, dest_ref, meta_ref):
    s = rt_ref.shape[0]
    nblk = s // PLAN_BLK
    lane = lax.broadcasted_iota(jnp.int32, (PLAN_BLK, LANES), 1)
    is_exp = lane < N_EXPERTS
    r = lax.broadcasted_iota(jnp.int32, (PLAN_BLK, PLAN_BLK), 0)
    c = lax.broadcasted_iota(jnp.int32, (PLAN_BLK, PLAN_BLK), 1)
    strict_tril = (r > c).astype(BF16)

    def rank_body(b, carry):
        rows = pl.ds(pl.multiple_of(b * PLAN_BLK, PLAN_BLK), PLAN_BLK)
        sel = jnp.where(is_exp, rt_ref[rows, :], 0.0)
        rank = jnp.dot(strict_tril, sel.astype(BF16), preferred_element_type=F32) + carry
        dest_ref[rows, :] = rank.astype(jnp.int32)
        return carry + jnp.sum(sel, axis=0, keepdims=True)

    counts = lax.fori_loop(0, nblk, rank_body, jnp.zeros((1, LANES), F32))
    tiles = jnp.floor((counts + (MOE_TM - 1)) * (1.0 / MOE_TM))
    er = lax.broadcasted_iota(jnp.int32, (LANES, LANES), 0)
    ec = lax.broadcasted_iota(jnp.int32, (LANES, LANES), 1)
    before = (er < ec).astype(F32)
    tile_start = jnp.dot(jnp.broadcast_to(tiles, (SUBLANES, LANES)), before, preferred_element_type=F32,
                         precision=lax.Precision.HIGHEST)[0:1, :]
    row_off = tile_start * MOE_TM
    meta_ref[0:1, :] = counts.astype(jnp.int32)
    meta_ref[1:2, :] = tiles.astype(jnp.int32)
    meta_ref[2:3, :] = tile_start.astype(jnp.int32)
    meta_ref[3:SUBLANES, :] = jnp.zeros((SUBLANES - 3, LANES), jnp.int32)

    def dest_body(b, _):
        rows = pl.ds(pl.multiple_of(b * PLAN_BLK, PLAN_BLK), PLAN_BLK)
        rt = rt_ref[rows, :]
        pos = dest_ref[rows, :].astype(F32) + row_off
        i1 = rt[:, RT_I1:RT_I1 + 1].astype(jnp.int32)
        i2 = rt[:, RT_I2:RT_I2 + 1].astype(jnp.int32)
        d1 = jnp.sum(jnp.where(lane == i1, pos, 0.0), axis=1, keepdims=True)
        d2 = jnp.sum(jnp.where(lane == i2, pos, 0.0), axis=1, keepdims=True)
        dest_ref[rows, :] = jnp.where(lane == 0, d1, jnp.where(lane == 1, d2, 0.0)).astype(jnp.int32)
        return 0

    lax.fori_loop(0, nblk, dest_body, 0)


def moe_plan(rt):
    s = rt.shape[0]
    return pl.pallas_call(
        _plan_kernel,
        out_shape=[jax.ShapeDtypeStruct((s, LANES), jnp.int32),
                   jax.ShapeDtypeStruct((SUBLANES, LANES), jnp.int32)],
        compiler_params=pltpu.CompilerParams(vmem_limit_bytes=VMEM_LIMIT_BYTES),
        name="moe_plan",
    )(rt)


def _invert_kernel(d1_ref, d2_ref, src_ref):
    n_rows = src_ref.shape[0]
    n_tok = d1_ref.shape[0]

    def zero(i, _):
        src_ref[i] = 0
        return 0

    lax.fori_loop(0, n_rows, zero, 0, unroll=16)

    def put(t, _):
        src_ref[d1_ref[t]] = t
        src_ref[d2_ref[t]] = t
        return 0

    lax.fori_loop(0, n_tok, put, 0, unroll=8)


def moe_invert(d1, d2, n_rows):
    smem = pl.BlockSpec(memory_space=pltpu.SMEM)
    return pl.pallas_call(
        _invert_kernel,
        in_specs=[smem, smem],
        out_specs=smem,
        out_shape=jax.ShapeDtypeStruct((n_rows,), jnp.int32),
        name="moe_invert",
    )(d1, d2)


GATHER_TM = 256
GATHER_UNROLL = 8


def _gather_norm_kernel(src_ref, nt_ref, h_hbm, w_ref, o_ref, buf_ref, sem):
    i = pl.program_id(0)
    nt = nt_ref[0]

    def row_copy(tile, slot, r):
        return pltpu.make_async_copy(h_hbm.at[pl.ds(src_ref[tile * MOE_TM + r], 1), :],
                                     buf_ref.at[slot, pl.ds(r, 1), :], sem.at[slot])

    def issue(tile, slot):
        def body(r, _):
            row_copy(tile, slot, r).start()
            return 0
        lax.fori_loop(0, MOE_TM, body, 0, unroll=GATHER_UNROLL)

    @pl.when(i == 0)
    def _():
        issue(0, 0)

    @pl.when(i + 1 < nt)
    def _():
        issue(i + 1, (i + 1) % 2)

    @pl.when(i < nt)
    def _():
        slot = i % 2

        pltpu.make_async_copy(h_hbm.at[pl.ds(0, MOE_TM), :], buf_ref.at[slot], sem.at[slot]).wait()
        x = buf_ref[slot]
        ms = jnp.mean(x * x, axis=-1, keepdims=True)
        o_ref[...] = (x * lax.rsqrt(ms + NORM_EPS) * w_ref[...]).astype(o_ref.dtype)

    @pl.when(i >= nt)
    def _():
        o_ref[...] = jnp.zeros(o_ref.shape, o_ref.dtype)


def moe_gather_norm(src, nt, h, w):
    n_rows = src.shape[0]
    d = h.shape[1]
    return pl.pallas_call(
        _gather_norm_kernel,
        grid_spec=pltpu.PrefetchScalarGridSpec(
            num_scalar_prefetch=2,
            grid=(n_rows // MOE_TM,),
            in_specs=[pl.BlockSpec(memory_space=pl.ANY),
                      pl.BlockSpec((1, d), lambda i, src, nt: (0, 0))],
            out_specs=pl.BlockSpec((MOE_TM, d), lambda i, src, nt: (i, 0)),
            scratch_shapes=[pltpu.VMEM((2, MOE_TM, d), F32), pltpu.SemaphoreType.DMA((2,))]),
        out_shape=jax.ShapeDtypeStruct((n_rows, d), BF16),
        compiler_params=_cparams("arbitrary"),
        name="moe_gather_norm",
    )(src, nt, h, w.reshape(1, d))


def _combine_kernel(d1_ref, d2_ref, h_ref, rt_ref, pw_ref, y_hbm, o_ref, ya_ref, yb_ref, sem, *, post_norm):
    i = pl.program_id(0)
    n = pl.num_programs(0)

    def copy_a(blk, slot, r):
        return pltpu.make_async_copy(y_hbm.at[pl.ds(d1_ref[blk * GATHER_TM + r], 1), :],
                                     ya_ref.at[slot, pl.ds(r, 1), :], sem.at[0, slot])

    def copy_b(blk, slot, r):
        return pltpu.make_async_copy(y_hbm.at[pl.ds(d2_ref[blk * GATHER_TM + r], 1), :],
                                     yb_ref.at[slot, pl.ds(r, 1), :], sem.at[1, slot])

    def issue(blk, slot):
        def body(r, _):
            copy_a(blk, slot, r).start()
            copy_b(blk, slot, r).start()
            return 0
        lax.fori_loop(0, GATHER_TM, body, 0, unroll=GATHER_UNROLL)

    @pl.when(i == 0)
    def _():
        issue(0, 0)

    @pl.when(i + 1 < n)
    def _():
        issue(i + 1, (i + 1) % 2)

    slot = i % 2

    pltpu.make_async_copy(y_hbm.at[pl.ds(0, GATHER_TM), :], ya_ref.at[slot], sem.at[0, slot]).wait()
    pltpu.make_async_copy(y_hbm.at[pl.ds(0, GATHER_TM), :], yb_ref.at[slot], sem.at[1, slot]).wait()
    rt = rt_ref[...]
    out = h_ref[...] + rt[:, RT_C1:RT_C1 + 1] * ya_ref[slot] + rt[:, RT_C2:RT_C2 + 1] * yb_ref[slot]
    if post_norm:
        ms = jnp.mean(out * out, axis=-1, keepdims=True)
        out = out * lax.rsqrt(ms + NORM_EPS) * pw_ref[...]
    o_ref[...] = out


def moe_combine(d1, d2, h, rt, y, post_norm_w):
    s, d = h.shape
    post_norm = post_norm_w is not None
    pw = (post_norm_w if post_norm else jnp.ones((d,), F32)).reshape(1, d)
    return pl.pallas_call(
        functools.partial(_combine_kernel, post_norm=post_norm),
        grid_spec=pltpu.PrefetchScalarGridSpec(
            num_scalar_prefetch=2,
            grid=(s // GATHER_TM,),
            in_specs=[pl.BlockSpec((GATHER_TM, d), lambda i, a, b: (i, 0)),
                      pl.BlockSpec((GATHER_TM, LANES), lambda i, a, b: (i, 0)),
                      pl.BlockSpec((1, d), lambda i, a, b: (0, 0)),
                      pl.BlockSpec(memory_space=pl.ANY)],
            out_specs=pl.BlockSpec((GATHER_TM, d), lambda i, a, b: (i, 0)),
            scratch_shapes=[pltpu.VMEM((2, GATHER_TM, d), F32), pltpu.VMEM((2, GATHER_TM, d), F32),
                            pltpu.SemaphoreType.DMA((2, 2))]),
        out_shape=jax.ShapeDtypeStruct((s, d), F32),
        compiler_params=_cparams("arbitrary"),
        name="moe_combine",
    )(d1, d2, h, rt, pw, y)


def _mm_kernel(x_ref, w_ref, o_ref):
    o_ref[...] = jnp.dot(x_ref[...], w_ref[...], preferred_element_type=F32).astype(o_ref.dtype)


def matmul(x, w, out_dtype, tm, tn):
    m, k = x.shape
    n = w.shape[1]
    return pl.pallas_call(
        _mm_kernel,
        grid=(m // tm, n // tn),
        in_specs=[pl.BlockSpec((tm, k), lambda i, j: (i, 0)),
                  pl.BlockSpec((k, tn), lambda i, j: (0, j))],
        out_specs=pl.BlockSpec((tm, tn), lambda i, j: (i, j)),
        out_shape=jax.ShapeDtypeStruct((m, n), out_dtype),
        compiler_params=_cparams("parallel", "arbitrary"),
        name="in_proj",
    )(x, w)


def _out_proj_kernel(ya_ref, yb_ref, wa_ref, wb_ref, r_ref, o_ref):
    acc = jnp.dot(ya_ref[...], wa_ref[...], preferred_element_type=F32)
    acc += jnp.dot(yb_ref[...], wb_ref[...], preferred_element_type=F32)
    o_ref[...] = r_ref[...] + acc


def out_proj(y_ssd, y_ml, w_out, resid, tm=1024, tn=1024):
    m, ka = y_ssd.shape
    n = w_out.shape[1]
    return pl.pallas_call(
        _out_proj_kernel,
        grid=(m // tm, n // tn),
        in_specs=[pl.BlockSpec((tm, ka), lambda i, j: (i, 0)),
                  pl.BlockSpec((tm, ka), lambda i, j: (i, 0)),
                  pl.BlockSpec((ka, tn), lambda i, j: (0, j)),
                  pl.BlockSpec((ka, tn), lambda i, j: (1, j)),
                  pl.BlockSpec((tm, tn), lambda i, j: (i, j))],
        out_specs=pl.BlockSpec((tm, tn), lambda i, j: (i, j)),
        out_shape=jax.ShapeDtypeStruct((m, n), F32),
        compiler_params=_cparams("parallel", "arbitrary"),
        name="out_proj",
    )(y_ssd, y_ml, w_out, w_out, resid)


def _silu(x):
    return x * jax.nn.sigmoid(x)


class UpPlan(NamedTuple):
    counts: jax.Array
    run_of_tile: jax.Array
    first_of_run: jax.Array
    phase_expert: jax.Array
    phase_hidden: jax.Array


def _up_plan(tiles, n_tiles_max, n_hidden_tiles):
    ne = tiles.shape[0]
    tile_end = jnp.cumsum(tiles)
    nt = tile_end[-1]
    tid = jnp.minimum(jnp.arange(n_tiles_max, dtype=jnp.int32), nt - 1)
    eot = jnp.sum((tid[:, None] >= tile_end[None, :]).astype(jnp.int32), axis=1)
    has = tiles > 0
    n_runs = jnp.sum(has.astype(jnp.int32))
    run_of_expert = jnp.cumsum(has.astype(jnp.int32)) - 1
    run_expert = jnp.argsort(jnp.logical_not(has), stable=True).astype(jnp.int32)
    k = jnp.arange(ne * n_hidden_tiles, dtype=jnp.int32)
    plan = UpPlan(counts=jnp.stack([nt, n_runs, n_runs * n_hidden_tiles]).astype(jnp.int32),
                  run_of_tile=run_of_expert[eot],
                  first_of_run=(tid == (tile_end - tiles)[eot]).astype(jnp.int32),
                  phase_expert=run_expert[k % n_runs],
                  phase_hidden=jnp.minimum(k // n_runs, n_hidden_tiles - 1))
    return plan, eot


def _grouped_up_kernel(cnt_ref, run_ref, first_ref, pe_ref, ph_ref, x_ref, wg_hbm, wu_hbm, o_ref,
                       wf_ref, wb_ref, sem, *, tf):
    j, t = pl.program_id(0), pl.program_id(1)
    nt, n_runs, n_phases = cnt_ref[0], cnt_ref[1], cnt_ref[2]

    def fetch(p):
        cols = pl.ds(pl.multiple_of(ph_ref[p] * tf, tf), tf)
        return (pltpu.make_async_copy(wg_hbm.at[pe_ref[p], :, cols], wf_ref.at[0], sem.at[0]),
                pltpu.make_async_copy(wu_hbm.at[pe_ref[p], :, cols], wf_ref.at[1], sem.at[1]))

    def start(p):
        for c in fetch(p):
            c.start()

    active = t < nt
    p = j * n_runs + run_ref[t]

    @pl.when(active & (j == 0) & (t == 0))
    def _():
        start(0)

    @pl.when(active & (first_ref[t] == 1))
    def _():
        for c in fetch(p):
            c.wait()
        wb_ref[0] = wf_ref[0].astype(BF16)
        wb_ref[1] = wf_ref[1].astype(BF16)

        @pl.when(p + 1 < n_phases)
        def _():
            start(p + 1)

    @pl.when(active)
    def _():
        x = x_ref[...]
        g = jnp.dot(x, wb_ref[0], preferred_element_type=F32)
        u = jnp.dot(x, wb_ref[1], preferred_element_type=F32)
        o_ref[...] = (_silu(g) * u).astype(o_ref.dtype)

    @pl.when(jnp.logical_not(active))
    def _():
        o_ref[...] = jnp.zeros(o_ref.shape, o_ref.dtype)


def grouped_swiglu_up(plan, xs, wg, wu, tm, tf):
    r, k = xs.shape
    f = wg.shape[2]
    row = lambda j, t, cnt, *_: (jnp.maximum(jnp.minimum(t, cnt[0] - 1), 0), 0)
    return pl.pallas_call(
        functools.partial(_grouped_up_kernel, tf=tf),
        grid_spec=pltpu.PrefetchScalarGridSpec(
            num_scalar_prefetch=5,
            grid=(f // tf, r // tm),
            in_specs=[pl.BlockSpec((tm, k), row),
                      pl.BlockSpec(memory_space=pl.ANY),
                      pl.BlockSpec(memory_space=pl.ANY)],
            out_specs=pl.BlockSpec((tm, tf), lambda j, t, *_: (t, j)),
            scratch_shapes=[pltpu.VMEM((2, k, tf), F32), pltpu.VMEM((2, k, tf), BF16),
                            pltpu.SemaphoreType.DMA((2,))]),
        out_shape=jax.ShapeDtypeStruct((r, f), BF16),
        compiler_params=_cparams("arbitrary", "arbitrary"),
        name="grouped_swiglu_up",
    )(*plan, xs, wg, wu)


def _grouped_down_kernel(cnt_ref, run_ref, first_ref, pe_ref, ph_ref, a_ref, w_hbm, o_ref, wf_ref, wb_ref, sem, *, tn):
    j, t = pl.program_id(0), pl.program_id(1)
    nt, n_runs, n_phases = cnt_ref[0], cnt_ref[1], cnt_ref[2]

    def fetch(p):
        cols = pl.ds(pl.multiple_of(ph_ref[p] * tn, tn), tn)
        return pltpu.make_async_copy(w_hbm.at[pe_ref[p], :, cols], wf_ref, sem)

    active = t < nt
    p = j * n_runs + run_ref[t]

    @pl.when(active & (j == 0) & (t == 0))
    def _():
        fetch(0).start()

    @pl.when(active & (first_ref[t] == 1))
    def _():
        fetch(p).wait()
        wb_ref[...] = wf_ref[...].astype(BF16)

        @pl.when(p + 1 < n_phases)
        def _():
            fetch(p + 1).start()

    @pl.when(active)
    def _():
        o_ref[...] = jnp.dot(a_ref[...], wb_ref[...], preferred_element_type=F32)

    @pl.when(jnp.logical_not(active))
    def _():
        o_ref[...] = jnp.zeros(o_ref.shape, o_ref.dtype)


def grouped_down(plan, a, wd, tn):
    r, f = a.shape
    n = wd.shape[2]
    row = lambda j, t, cnt, *_: (jnp.maximum(jnp.minimum(t, cnt[0] - 1), 0), 0)
    return pl.pallas_call(
        functools.partial(_grouped_down_kernel, tn=tn),
        grid_spec=pltpu.PrefetchScalarGridSpec(
            num_scalar_prefetch=5,
            grid=(n // tn, r // MOE_TM),
            in_specs=[pl.BlockSpec((MOE_TM, f), row),
                      pl.BlockSpec(memory_space=pl.ANY)],
            out_specs=pl.BlockSpec((MOE_TM, tn), lambda j, t, *_: (t, j)),
            scratch_shapes=[pltpu.VMEM((f, tn), F32), pltpu.VMEM((f, tn), BF16), pltpu.SemaphoreType.DMA(())]),
        out_shape=jax.ShapeDtypeStruct((r, n), F32),
        compiler_params=_cparams("arbitrary", "arbitrary"),
        name="grouped_down",
    )(*plan, a, wd)


def moe_swiglu(h, norm_w, router_w, wg, wu, wd, post_norm_w=None):
    s, d = h.shape
    n_tiles_max = (2 * s) // MOE_TM + N_EXPERTS
    n_rows = n_tiles_max * MOE_TM
    rt = router(h, norm_w, router_w)
    dest, meta = moe_plan(rt)
    d1, d2 = dest[:, 0], dest[:, 1]
    plan, _ = _up_plan(meta[1, :N_EXPERTS], n_tiles_max, wg.shape[2] // UP_TF)
    plan_down, _ = _up_plan(meta[1, :N_EXPERTS], n_tiles_max, d // DOWN_TN)
    nt = plan.counts[0:1]
    src = moe_invert(d1, d2, n_rows)
    xs = moe_gather_norm(src, nt, h, norm_w)
    a = grouped_swiglu_up(plan, xs, wg, wu, MOE_TM, UP_TF)
    y = grouped_down(plan_down, a, wd, DOWN_TN)
    return moe_combine(d1, d2, h, rt, y, post_norm_w)


def _down_kernel(a_ref, w_ref, r_ref, o_ref):
    kk = pl.program_id(2)

    @pl.when(kk == 0)
    def _():
        o_ref[...] = r_ref[...]

    o_ref[...] += jnp.dot(a_ref[...], w_ref[...], preferred_element_type=F32)


def down_proj(a, w, resid, tk, tm=1024, tn=1024):
    m, k = a.shape
    n = w.shape[1]
    return pl.pallas_call(
        _down_kernel,
        grid=(m // tm, n // tn, k // tk),
        in_specs=[pl.BlockSpec((tm, tk), lambda i, j, kk: (i, kk)),
                  pl.BlockSpec((tk, tn), lambda i, j, kk: (kk, j)),
                  pl.BlockSpec((tm, tn), lambda i, j, kk: (i, j))],
        out_specs=pl.BlockSpec((tm, tn), lambda i, j, kk: (i, j)),
        out_shape=jax.ShapeDtypeStruct((m, n), F32),
        compiler_params=_cparams("parallel", "arbitrary", "arbitrary"),
        name="down_proj",
    )(a, w, resid)


def _softplus(x):
    return jnp.maximum(x, 0.0) + jnp.log1p(jnp.exp(-jnp.abs(x)))


def _tril_ones(n):
    r = lax.broadcasted_iota(jnp.int32, (n, n), 0)
    c = lax.broadcasted_iota(jnp.int32, (n, n), 1)
    return r >= c


def _ssd_kernel(z_ref, x_ref, b_ref, c_ref, sm_ref, convw_ref, convb_ref, smb_ref, alog_ref,
                dskip_ref, nw_ref, expand_ref, o_ref, tail_ref, state_ref):
    L = SSD_CHUNK
    HI = lax.Precision.HIGHEST

    @pl.when(pl.program_id(0) == 0)
    def _():
        tail_ref[...] = jnp.zeros(tail_ref.shape, F32)
        state_ref[...] = jnp.zeros(state_ref.shape, F32)

    row8 = lax.broadcasted_iota(jnp.int32, (SUBLANES, LANES), 0)

    def conv_silu(u, col0):
        w = u.shape[1]
        cols = slice(col0, col0 + w)
        tail = tail_ref[:, cols]
        acc = convb_ref[:, cols] + u * convw_ref[SSD_CONV - 1:SSD_CONV, cols]
        for k in range(1, SSD_CONV):
            rolled = pltpu.roll(u, k, axis=0)
            fix = jnp.where(jnp.tile(row8, (1, w // LANES)) < k, pltpu.roll(tail, k, axis=0), rolled[0:SUBLANES])
            shifted = jnp.concatenate([fix, rolled[SUBLANES:]], axis=0)
            acc = acc + shifted * convw_ref[SSD_CONV - 1 - k:SSD_CONV - k, cols]
        tail_ref[:, cols] = u[L - SUBLANES:L]
        return _silu(acc)

    xs = conv_silu(x_ref[...], 0)
    bmat = conv_silu(b_ref[...], SSD_INNER)
    cmat = conv_silu(c_ref[...], SSD_INNER + SSD_GROUPS * SSD_STATE)

    lane = lax.broadcasted_iota(jnp.int32, (L, LANES), 1)

    def expand_heads(v):
        hi = v.astype(BF16).astype(F32)
        r1 = v - hi
        mid = r1.astype(BF16).astype(F32)
        lo = r1 - mid
        packed = jnp.where(lane < 32, hi,
                           jnp.where(lane < 64, pltpu.roll(mid, 32, axis=1),
                                     jnp.where(lane < 96, pltpu.roll(lo, 64, axis=1), 0.0)))
        return jnp.dot(packed.astype(BF16), expand_ref[...], preferred_element_type=F32)

    causal = _tril_ones(L)
    dt = jnp.where(lane < SSD_HEADS, _softplus(sm_ref[...] + smb_ref[...]), 0.0)
    da = dt * (-jnp.exp(alog_ref[...]))
    acs = jnp.dot(causal.astype(F32), da, preferred_element_type=F32, precision=HI)
    acs_t = acs.T
    a_last = acs[L - 1:L, :]
    dt_e = expand_heads(dt)
    eacs_e = expand_heads(jnp.exp(acs))
    dst_e = expand_heads(jnp.exp(a_last - acs))
    xdt = xs * dt_e
    xdec = (xdt * dst_e).astype(BF16)
    xdt_b = xdt.astype(BF16)
    chunk_decay_e = eacs_e[L - 1:L, :]

    y = xs * dskip_ref[...]
    zgate = _silu(z_ref[...])
    lo_half = lane < SSD_HEAD_DIM

    def group(g):
        gs = slice(g * SSD_GROUP_WIDTH, (g + 1) * SSD_GROUP_WIDTH)
        bg = bmat[:, g * SSD_STATE:(g + 1) * SSD_STATE]
        cg = cmat[:, g * SSD_STATE:(g + 1) * SSD_STATE]
        cg_b = cg.astype(BF16)
        cb = lax.dot_general(cg_b, bg.astype(BF16), (((1,), (1,)), ((), ())),
                             preferred_element_type=F32)
        h_enter = state_ref[g]
        y_off = jnp.dot(cg_b, h_enter.astype(BF16), preferred_element_type=F32)
        st_new = jnp.dot(bg.T.astype(BF16), xdec[:, gs], preferred_element_type=F32)
        state_ref[g] = chunk_decay_e[:, gs] * h_enter + st_new
        yd_parts = []
        for p in range(4):
            j0 = g * 8 + 2 * p
            ms = []
            for j in (j0, j0 + 1):
                seg = jnp.where(causal, acs[:, j:j + 1] - acs_t[j:j + 1, :], -jnp.inf)
                ms.append((cb * jnp.exp(seg)).astype(BF16))
            xp = xdt_b[:, g * SSD_GROUP_WIDTH + p * LANES:g * SSD_GROUP_WIDTH + (p + 1) * LANES]
            zero = jnp.zeros_like(xp)
            rhs = jnp.concatenate([jnp.where(lo_half, xp, zero), jnp.where(lo_half, zero, xp)], axis=0)
            yd_parts.append(jnp.dot(jnp.concatenate(ms, axis=1), rhs, preferred_element_type=F32))
        y_diag = jnp.concatenate(yd_parts, axis=1)
        yg = (y[:, gs] + y_diag + y_off * eacs_e[:, gs]) * zgate[:, gs]
        msq = jnp.mean(yg * yg, axis=1, keepdims=True)
        o_ref[:, gs] = (yg * lax.rsqrt(msq + NORM_EPS) * nw_ref[:, gs]).astype(o_ref.dtype)

    return group


def _log_sigmoid(x):
    return jnp.minimum(x, 0.0) - jnp.log1p(jnp.exp(-jnp.abs(x)))


def _mlstm_kernel(q_ref, k_ref, v_ref, o_ref, sm_ref, smb_ref, nw_ref, out_ref, c_ref, n_ref, m_ref):
    L = MLSTM_CHUNK
    HI = lax.Precision.HIGHEST

    @pl.when(pl.program_id(0) == 0)
    def _():
        c_ref[...] = jnp.zeros(c_ref.shape, F32)
        n_ref[...] = jnp.zeros(n_ref.shape, F32)
        m_ref[...] = jnp.zeros(m_ref.shape, F32)

    causal = _tril_ones(L)
    sm = sm_ref[...] + smb_ref[...]
    bcs = jnp.dot(causal.astype(F32), _log_sigmoid(sm), preferred_element_type=F32, precision=HI)
    sm_t = sm.T
    bcs_t = bcs.T

    def head(h):
        li_col = sm[:, SMALL_I + h:SMALL_I + h + 1]
        b_col = bcs[:, SMALL_F + h:SMALL_F + h + 1]
        r_row = sm_t[SMALL_I + h:SMALL_I + h + 1, :] - bcs_t[SMALL_F + h:SMALL_F + h + 1, :]
        m_prev = m_ref[h:h + 1, 0:1]
        inter = b_col + m_prev
        dm = jnp.where(causal, b_col + r_row, -jnp.inf)
        m_t = jnp.maximum(inter, jnp.max(dm, axis=1, keepdims=True))
        w_intra = jnp.exp(dm - m_t)
        w_inter = jnp.exp(inter - m_t)
        qf = q_ref[:, h * MLSTM_DQK:(h + 1) * MLSTM_DQK] * (MLSTM_DQK ** -0.5)
        kf = k_ref[:, h * MLSTM_DQK:(h + 1) * MLSTM_DQK]
        qb = qf.astype(BF16)
        vb = v_ref[:, h * MLSTM_DV:(h + 1) * MLSTM_DV].astype(BF16)
        sqk = lax.dot_general(qb, kf.astype(BF16), (((1,), (1,)), ((), ())),
                              preferred_element_type=F32) * w_intra
        c_st = c_ref[h]
        n_row = n_ref[h:h + 1, :]
        num = (w_inter * jnp.dot(qb, c_st.astype(BF16), preferred_element_type=F32)
               + jnp.dot(sqk.astype(BF16), vb, preferred_element_type=F32))
        qn = jnp.sum(qf * n_row, axis=1, keepdims=True)
        den = w_inter * qn + jnp.sum(sqk, axis=1, keepdims=True)
        hout = num / jnp.maximum(jnp.abs(den), jnp.exp(-m_t))
        b_last = b_col[L - 1:L, :]
        g_col = b_last - b_col + li_col
        m_new = jnp.maximum(b_last + m_prev, jnp.max(g_col, axis=0, keepdims=True))
        decay = jnp.exp(b_last + m_prev - m_new)
        kw = kf * jnp.exp(g_col - m_new)
        c_ref[h] = decay * c_st + jnp.dot(kw.T.astype(BF16), vb, preferred_element_type=F32)
        n_ref[h:h + 1, :] = decay * n_row + jnp.sum(kw, axis=0, keepdims=True)
        m_ref[h:h + 1, :] = jnp.broadcast_to(m_new, (1, LANES))
        vs = slice(h * MLSTM_DV, (h + 1) * MLSTM_DV)
        msq = jnp.mean(hout * hout, axis=1, keepdims=True)
        hn = hout * lax.rsqrt(msq + NORM_EPS) * nw_ref[:, vs]
        out_ref[:, vs] = (jax.nn.sigmoid(o_ref[:, vs]) * hn).astype(out_ref.dtype)

    return head


assert MLSTM_CHUNK == SSD_CHUNK


def _mixer_kernel(z_ref, x_ref, b_ref, c_ref, q_ref, k_ref, v_ref, og_ref, sm_ref,
                  convw_ref, convb_ref, smb_ref, alog_ref, dskip_ref, ssd_nw_ref, expand_ref, ml_nw_ref,
                  y_ssd_ref, y_ml_ref, tail_ref, state_ref, cst_ref, nst_ref, mst_ref):
    mlstm_head = _mlstm_kernel(q_ref, k_ref, v_ref, og_ref, sm_ref, smb_ref, ml_nw_ref, y_ml_ref,
                               cst_ref, nst_ref, mst_ref)
    ssd_group = _ssd_kernel(z_ref, x_ref, b_ref, c_ref, sm_ref, convw_ref, convb_ref, smb_ref, alog_ref,
                            dskip_ref, ssd_nw_ref, expand_ref, y_ssd_ref, tail_ref, state_ref)
    for h in range(MLSTM_HEADS):
        mlstm_head(h)
    for g in range(SSD_GROUPS):
        ssd_group(g)


def hybrid_mixer(proj_a, proj_b, proj_s, conv_w, conv_b, small_bias, a_log, d_skip, ssd_norm_w, ml_norm_w):
    s = proj_a.shape[0]
    L = SSD_CHUNK
    conv_dim = conv_w.shape[1]
    qk_w = MLSTM_HEADS * MLSTM_DQK
    alog = jnp.pad(a_log, (0, LANES - SSD_HEADS)).reshape(1, LANES)
    dskip_e = jnp.repeat(d_skip, SSD_HEAD_DIM).reshape(1, SSD_INNER)
    piece_row = jnp.arange(LANES)[:, None]
    expand = ((jnp.arange(SSD_INNER)[None, :] // SSD_HEAD_DIM == piece_row % SSD_HEADS)
              & (piece_row < 3 * SSD_HEADS)).astype(BF16)
    full = lambda shp: pl.BlockSpec(shp, lambda i: (0,) * len(shp))
    blk = lambda w, col: pl.BlockSpec((L, w), lambda i: (i, col // w))
    return pl.pallas_call(
        _mixer_kernel,
        grid=(s // L,),
        in_specs=[blk(SSD_INNER, PA_Z), blk(SSD_INNER, PA_X), blk(512, PA_B), blk(512, PA_C),
                  blk(qk_w, PB_Q), blk(qk_w, PB_K), blk(MLSTM_WIDTH, PB_V), blk(MLSTM_WIDTH, PB_O),
                  blk(LANES, 0),
                  full((SSD_CONV, conv_dim)), full((1, conv_dim)), full((1, LANES)), full((1, LANES)),
                  full((1, SSD_INNER)), full((1, SSD_INNER)), full((LANES, SSD_INNER)),
                  full((1, MLSTM_WIDTH))],
        out_specs=[pl.BlockSpec((L, SSD_INNER), lambda i: (i, 0)),
                   pl.BlockSpec((L, MLSTM_WIDTH), lambda i: (i, 0))],
        out_shape=[jax.ShapeDtypeStruct((s, SSD_INNER), BF16),
                   jax.ShapeDtypeStruct((s, MLSTM_WIDTH), BF16)],
        scratch_shapes=[pltpu.VMEM((SUBLANES, conv_dim), F32),
                        pltpu.VMEM((SSD_GROUPS, SSD_STATE, SSD_GROUP_WIDTH), F32),
                        pltpu.VMEM((MLSTM_HEADS, MLSTM_DQK, MLSTM_DV), F32),
                        pltpu.VMEM((SUBLANES, MLSTM_DQK), F32),
                        pltpu.VMEM((SUBLANES, LANES), F32)],
        compiler_params=_cparams("arbitrary"),
        name="hybrid_mixer",
    )(proj_a, proj_a, proj_a, proj_a, proj_b, proj_b, proj_b, proj_b, proj_s,
      conv_w, conv_b.reshape(1, conv_dim), small_bias, alog, dskip_e,
      ssd_norm_w.reshape(1, SSD_INNER), expand, ml_norm_w.reshape(1, MLSTM_WIDTH))


def _split_w_in(w_in):
    w_a = w_in[:, :PA_WIDTH].astype(BF16)
    w_b = w_in[:, PB_START:PB_START + PB_WIDTH].astype(BF16)
    w_s = jnp.concatenate([w_in[:, IN_DT:IN_DT + SSD_HEADS], w_in[:, IN_I:IN_I + MLSTM_HEADS],
                           w_in[:, IN_F:IN_F + MLSTM_HEADS],
                           jnp.zeros((w_in.shape[0], LANES - SSD_HEADS - 2 * MLSTM_HEADS), w_in.dtype)],
                          axis=1).astype(BF16)
    return w_a, w_b, w_s


def _small_bias(dt_bias, i_bias, f_bias):
    v = jnp.concatenate([dt_bias, i_bias, f_bias, jnp.zeros((LANES - 40,), F32)])
    return v.reshape(1, LANES)


def kernel(x, norm_mix_w, w_in, conv_w, conv_b, dt_bias, a_log, d_skip, ssd_norm_w, mlstm_i_bias,
           mlstm_f_bias, mlstm_norm_w, w_out, norm_ffn_w, ffn_w_gate, ffn_w_up, ffn_w_down, router_w,
           moe_w_gate, moe_w_up, moe_w_down, final_norm_w):
    bsz, s, d = x.shape
    depth = w_in.shape[0]
    outs = []
    for b in range(bsz):
        h = x[b]
        normed = False
        for layer in range(depth):
            w_a, w_b, w_s = _split_w_in(w_in[layer])
            u, proj_s = rmsnorm_small(h, norm_mix_w[layer], w_s)
            proj_a = matmul(u, w_a, F32, tm=1024, tn=PA_TN)
            proj_b = matmul(u, w_b, F32, tm=1024, tn=PB_TN)
            sbias = _small_bias(dt_bias[layer], mlstm_i_bias[layer], mlstm_f_bias[layer])
            y_ssd, y_ml = hybrid_mixer(proj_a, proj_b, proj_s, conv_w[layer], conv_b[layer], sbias,
                                       a_log[layer], d_skip[layer], ssd_norm_w[layer], mlstm_norm_w[layer])
            h = out_proj(y_ssd, y_ml, w_out[layer].astype(BF16), h)
            j = layer // 2
            if layer % 2 == 0:
                u = rmsnorm(h, norm_ffn_w[layer], BF16)
                plan, _ = _up_plan(jnp.full((1,), s // DENSE_TM, jnp.int32), s // DENSE_TM, FF_DENSE // UP_TF)
                a = grouped_swiglu_up(plan, u, ffn_w_gate[j][None], ffn_w_up[j][None], DENSE_TM, UP_TF)
                h = down_proj(a, ffn_w_down[j].astype(BF16), h, tk=FF_DENSE // 4)
            else:
                normed = layer == depth - 1
                h = moe_swiglu(h, norm_ffn_w[layer], router_w[j], moe_w_gate[j], moe_w_up[j],
                               moe_w_down[j], final_norm_w if normed else None)
        outs.append(h if normed else rmsnorm(h, final_norm_w, F32))
    return jnp.stack(outs, axis=0)
```
